```python
import math
import jax, jax.numpy as jnp
from jax import lax
import numpy as np

D_MODEL = 1024
BATCH = 2
SEQ = 8192
DEPTH = 1

D_CONV = D_MODEL
CONV_WIDTH = 3
MLA_HEADS = 8
QK_NOPE = 128
QK_ROPE = 64
V_HEAD = 128
Q_LORA = 256
KV_LORA = 128
ROPE_THETA = 10000.0
Q_BLOCK = 128
MEM_LEN = 256
MEM_HEADS = 4
MEM_HEAD_DIM = D_MODEL // MEM_HEADS
N_EXPERTS = 16
EXPERT_FF = 1024
CAPACITY_FACTOR = 2
EPS = 1e-6
IN_COLS = 3 * D_CONV + Q_LORA + KV_LORA + QK_ROPE + 2 * D_MODEL

kernel_name = "hybrid_conv_mla_ec_moe_encoder"


def rmsnorm(x, g):
    xf = x.astype(jnp.float32)
    y = xf * lax.rsqrt(jnp.mean(xf * xf, axis=-1, keepdims=True) + EPS)
    return (y * g.astype(jnp.float32)).astype(x.dtype)


def rotary_tables(seq):
    inv = 1.0 / (ROPE_THETA ** (jnp.arange(0, QK_ROPE, 2, dtype=jnp.float32) / QK_ROPE))
    ang = jnp.arange(seq, dtype=jnp.float32)[:, None] * inv[None, :]
    return jnp.cos(ang), jnp.sin(ang)


def apply_rope(v, cos, sin):
    vf = v.astype(jnp.float32)
    v1, v2 = jnp.split(vf, 2, axis=-1)
    out = jnp.concatenate([v1 * cos - v2 * sin, v1 * sin + v2 * cos], axis=-1)
    return out.astype(v.dtype)


def short_conv(v, w, b):
    y = lax.conv_general_dilated(
        v, w[:, None, :].astype(v.dtype), window_strides=(1,), padding=((1, 1),),
        dimension_numbers=("NWC", "WIO", "NWC"), feature_group_count=v.shape[-1])
    return y + b.astype(v.dtype)


def mla_attention(q_nope, q_rope, k_nope, k_rope, v):
    b, s, h, _ = q_nope.shape
    nb = s // Q_BLOCK
    scale = (QK_NOPE + QK_ROPE) ** -0.5
    qn = q_nope.reshape(b, nb, Q_BLOCK, h, QK_NOPE).transpose(1, 0, 2, 3, 4)
    qr = q_rope.reshape(b, nb, Q_BLOCK, h, QK_ROPE).transpose(1, 0, 2, 3, 4)

    def block(args):
        qn_b, qr_b = args
        sc = (jnp.einsum("bqhd,bkhd->bhqk", qn_b, k_nope)
              + jnp.einsum("bqhr,bkr->bhqk", qr_b, k_rope))
        p = jax.nn.softmax(sc.astype(jnp.float32) * scale, axis=-1).astype(v.dtype)
        return jnp.einsum("bhqk,bkhd->bqhd", p, v)

    o = lax.map(block, (qn, qr))
    return o.transpose(1, 0, 2, 3, 4).reshape(b, s, h * V_HEAD)


def memory_cross_attention(h, m, w_q, w_kv, w_o):
    b, s, _ = h.shape
    ml = m.shape[1]
    q = (h @ w_q).reshape(b, s, MEM_HEADS, MEM_HEAD_DIM)
    kv = (m @ w_kv).reshape(b, ml, 2, MEM_HEADS, MEM_HEAD_DIM)
    k, v = kv[:, :, 0], kv[:, :, 1]
    sc = jnp.einsum("bqhd,bkhd->bhqk", q, k).astype(jnp.float32) * (MEM_HEAD_DIM ** -0.5)
    p = jax.nn.softmax(sc, axis=-1).astype(v.dtype)
    o = jnp.einsum("bhqk,bkhd->bqhd", p, v).reshape(b, s, MEM_HEADS * MEM_HEAD_DIM)
    return o @ w_o


def expert_choice_moe(h, w_router, w_gate, w_up, w_down):
    b, s, d = h.shape
    cap = max(1, CAPACITY_FACTOR * s // N_EXPERTS)
    aff = jax.nn.softmax((h @ w_router).astype(jnp.float32), axis=-1)
    gates, idx = lax.top_k(jnp.swapaxes(aff, 1, 2), cap)
    xg = jax.vmap(lambda hb, ib: hb[ib])(h, idx)
    a = jax.nn.silu(jnp.einsum("becd,edf->becf", xg, w_gate)) * jnp.einsum("becd,edf->becf", xg, w_up)
    yo = jnp.einsum("becf,efd->becd", a, w_down) * gates[..., None].astype(h.dtype)
    return jax.vmap(lambda ib, vb: jnp.zeros((s, d), vb.dtype).at[ib.reshape(-1)].add(vb.reshape(-1, d)))(idx, yo)


def setup_inputs(seed: int = 0) -> dict:
    key = jax.random.key(seed)
    ks = jax.random.split(key, 32)

    def w(k, shape, fan_in):
        return jax.random.normal(k, shape, jnp.float32) * fan_in ** -0.5

    def g(k, shape):
        return 1.0 + 0.02 * jax.random.normal(k, shape, jnp.float32)

    def bias(k, shape):
        return 0.02 * jax.random.normal(k, shape, jnp.float32)

    L = DEPTH
    return {
        "x": jax.random.normal(ks[0], (BATCH, SEQ, D_MODEL), jnp.float32),
        "mem": jax.random.normal(ks[1], (BATCH, MEM_LEN, D_MODEL), jnp.float32),
        "norm_mix_g": g(ks[2], (L, D_MODEL)),
        "w_in": w(ks[3], (L, D_MODEL, IN_COLS), D_MODEL),
        "conv_w": w(ks[4], (L, CONV_WIDTH, D_CONV), CONV_WIDTH),
        "conv_b": bias(ks[5], (L, D_CONV)),
        "w_conv_out": w(ks[6], (L, D_CONV, D_MODEL), D_CONV),
        "q_norm_g": g(ks[7], (L, Q_LORA)),
        "w_uq": w(ks[8], (L, Q_LORA, MLA_HEADS * (QK_NOPE + QK_ROPE)), Q_LORA),
        "kv_norm_g": g(ks[9], (L, KV_LORA)),
        "w_ukv": w(ks[10], (L, KV_LORA, MLA_HEADS * (QK_NOPE + V_HEAD)), KV_LORA),
        "w_mla_out": w(ks[11], (L, MLA_HEADS * V_HEAD, D_MODEL), MLA_HEADS * V_HEAD),
        "b_gate": bias(ks[12], (L, 2 * D_MODEL)),
        "w_mix_out": w(ks[13], (L, D_MODEL, D_MODEL), D_MODEL),
        "norm_mem_g": g(ks[14], (L, D_MODEL)),
        "norm_memkv_g": g(ks[15], (L, D_MODEL)),
        "w_mem_q": w(ks[16], (L, D_MODEL, MEM_HEADS * MEM_HEAD_DIM), D_MODEL),
        "w_mem_kv": w(ks[17], (L, D_MODEL, 2 * MEM_HEADS * MEM_HEAD_DIM), D_MODEL),
        "w_mem_out": w(ks[18], (L, MEM_HEADS * MEM_HEAD_DIM, D_MODEL), MEM_HEADS * MEM_HEAD_DIM),
        "norm_moe_g": g(ks[19], (L, D_MODEL)),
        "w_router": w(ks[20], (L, D_MODEL, N_EXPERTS), D_MODEL),
        "w_exp_gate": w(ks[21], (L, N_EXPERTS, D_MODEL, EXPERT_FF), D_MODEL),
        "w_exp_up": w(ks[22], (L, N_EXPERTS, D_MODEL, EXPERT_FF), D_MODEL),
        "w_exp_down": w(ks[23], (L, N_EXPERTS, EXPERT_FF, D_MODEL), EXPERT_FF),
        "norm_final_g": g(ks[24], (D_MODEL,)),
    }


def reference(x, mem, norm_mix_g, w_in, conv_w, conv_b, w_conv_out, q_norm_g, w_uq, kv_norm_g,
              w_ukv, w_mla_out, b_gate, w_mix_out, norm_mem_g, norm_memkv_g, w_mem_q, w_mem_kv,
              w_mem_out, norm_moe_g, w_router, w_exp_gate, w_exp_up, w_exp_down, norm_final_g):
    b, s, _ = x.shape
    cos, sin = rotary_tables(s)
    splits = list(np.cumsum([D_CONV, D_CONV, D_CONV, Q_LORA, KV_LORA, QK_ROPE]))
    for l in range(DEPTH):
        h = rmsnorm(x, norm_mix_g[l])
        u = h @ w_in[l]
        xc, gb, gc, cq, ckv, kr, glog = jnp.split(u, splits, axis=-1)
        y_conv = (gb * short_conv(gc * xc, conv_w[l], conv_b[l])) @ w_conv_out[l]
        q = (rmsnorm(cq, q_norm_g[l]) @ w_uq[l]).reshape(b, s, MLA_HEADS, QK_NOPE + QK_ROPE)
        q_nope, q_rope = q[..., :QK_NOPE], apply_rope(q[..., QK_NOPE:], cos[None, :, None, :], sin[None, :, None, :])
        kv = (rmsnorm(ckv, kv_norm_g[l]) @ w_ukv[l]).reshape(b, s, MLA_HEADS, QK_NOPE + V_HEAD)
        k_nope, v = kv[..., :QK_NOPE], kv[..., QK_NOPE:]
        k_rope = apply_rope(kr, cos[None], sin[None])
        y_mla = mla_attention(q_nope, q_rope, k_nope, k_rope, v) @ w_mla_out[l]
        gates = jax.nn.sigmoid((glog + b_gate[l]).astype(jnp.float32)).astype(x.dtype)
        g_conv, g_mla = gates[..., :D_MODEL], gates[..., D_MODEL:]
        x = x + (g_conv * y_conv + g_mla * y_mla) @ w_mix_out[l]
        x = x + memory_cross_attention(rmsnorm(x, norm_mem_g[l]), rmsnorm(mem, norm_memkv_g[l]),
                                       w_mem_q[l], w_mem_kv[l], w_mem_out[l])
        x = x + expert_choice_moe(rmsnorm(x, norm_moe_g[l]), w_router[l], w_exp_gate[l],
                                  w_exp_up[l], w_exp_down[l])
    return rmsnorm(x, norm_final_g)
```

```python
import functools
import math

import jax
import jax.numpy as jnp
from jax import lax
from jax.experimental import pallas as pl
from jax.experimental.pallas import tpu as pltpu

MLA_HEADS = 8
QK_NOPE = 128
QK_ROPE = 64
QK_DIM = QK_NOPE + QK_ROPE
V_HEAD = 128
ROPE_THETA = 10000.0
MEM_HEADS = 4
N_EXPERTS = 16
CAPACITY_FACTOR = 2
CONV_WIDTH = 3
EPS = 1e-6

LANES = 128
SUBLANES = 8
BF16_SUBLANES = 16
VMEM_LIMIT_BYTES = 56 * 1024 * 1024
POST_SUBTILE = 256

F32 = jnp.float32
BF16 = jnp.bfloat16


def _const_spec(shape):
    nd = len(shape)
    return pl.BlockSpec(shape, lambda *_: (0,) * nd, pipeline_mode=pl.Buffered(1))


def _rms(x, g):
    return x * lax.rsqrt(jnp.mean(x * x, axis=-1, keepdims=True) + EPS) * g


def _dot(a, b):
    return jnp.dot(a, b, preferred_element_type=F32)


def _dot_t(a, b):
    return lax.dot_general(a, b, (((1,), (1,)), ((), ())), preferred_element_type=F32)


def _inproj_kernel(cols, x_ref, g_ref, win_ref, qg_ref, kvg_ref, bg_ref, wq_ref, wqr_ref, wk_ref, wv_ref,
                   qc_ref, qs_ref, kc_ref, ks_ref,
                   v_ref, gb_ref, gates_ref, qt_ref, k_ref, vt_ref):
    d = x_ref.shape[1]
    hb = _rms(x_ref[...], g_ref[...]).astype(BF16)

    def proj(name, width):
        lo = cols[name]
        return _dot(hb, win_ref[:, lo:lo + width])

    v_ref[...] = (proj("gc", d) * proj("xc", d)).astype(BF16)
    gb_ref[...] = proj("gb", d).astype(BF16)
    gates_ref[...] = jax.nn.sigmoid(proj("glog", 2 * d) + bg_ref[...]).astype(BF16)

    q_lora = qg_ref.shape[1]
    kv_lora = kvg_ref.shape[1]
    qn = _rms(proj("cq", q_lora), qg_ref[...]).astype(BF16)
    kvn = _rms(proj("ckv", kv_lora), kvg_ref[...]).astype(BF16)
    kr2 = proj("kr", 2 * QK_ROPE)
    k_rope = kr2[:, :QK_ROPE] * kc_ref[...] + kr2[:, QK_ROPE:] * ks_ref[...]
    qc = qc_ref[...]
    qs = qs_ref[...]
    for h in range(MLA_HEADS):
        qt_h = _dot_t(wq_ref[h], qn) * qc + _dot_t(wqr_ref[h], qn) * qs
        qt_ref[0, h] = qt_h.astype(BF16)
        k_ref[0, h] = jnp.concatenate([_dot(kvn, wk_ref[h]), k_rope], axis=-1).astype(BF16)
        vt_ref[0, h] = _dot_t(wv_ref[h], kvn).astype(BF16)


def _inproj(x2d, batch, seq, g, win, cols, qg, kvg, bg, wq, wqr, wk, wv, qc, qs, kc, ks, tm):
    t, d = x2d.shape
    nb = seq // tm
    tok = lambda i: (i, 0)
    pos = lambda i: (i % nb, 0)
    pos_t = lambda i: (0, i % nb)
    head_blk = lambda i: (i // nb, 0, i % nb, 0)
    head_blk_t = lambda i: (i // nb, 0, 0, i % nb)
    in_specs = [
        pl.BlockSpec((tm, d), tok),
        _const_spec(g.shape), _const_spec(win.shape), _const_spec(qg.shape), _const_spec(kvg.shape),
        _const_spec(bg.shape), _const_spec(wq.shape), _const_spec(wqr.shape), _const_spec(wk.shape),
        _const_spec(wv.shape),
        pl.BlockSpec((QK_DIM, tm), pos_t), pl.BlockSpec((QK_DIM, tm), pos_t),
        pl.BlockSpec((tm, QK_ROPE), pos), pl.BlockSpec((tm, QK_ROPE), pos),
    ]
    out_shape = [
        jax.ShapeDtypeStruct((t, d), BF16),
        jax.ShapeDtypeStruct((t, d), BF16),
        jax.ShapeDtypeStruct((t, 2 * d), BF16),
        jax.ShapeDtypeStruct((batch, MLA_HEADS, QK_DIM, seq), BF16),
        jax.ShapeDtypeStruct((batch, MLA_HEADS, seq, QK_DIM), BF16),
        jax.ShapeDtypeStruct((batch, MLA_HEADS, V_HEAD, seq), BF16),
    ]
    out_specs = [
        pl.BlockSpec((tm, d), tok), pl.BlockSpec((tm, d), tok), pl.BlockSpec((tm, 2 * d), tok),
        pl.BlockSpec((1, MLA_HEADS, QK_DIM, tm), head_blk_t),
        pl.BlockSpec((1, MLA_HEADS, tm, QK_DIM), head_blk),
        pl.BlockSpec((1, MLA_HEADS, V_HEAD, tm), head_blk_t),
    ]
    return pl.pallas_call(
        functools.partial(_inproj_kernel, cols),
        grid=(t // tm,), in_specs=in_specs, out_specs=out_specs, out_shape=out_shape,
        compiler_params=pltpu.CompilerParams(dimension_semantics=("arbitrary",),
                                             vmem_limit_bytes=VMEM_LIMIT_BYTES),
        name="inproj",
    )(x2d, g, win, qg, kvg, bg, wq, wqr, wk, wv, qc, qs, kc, ks)


def _attn_kernel(tk, qt_ref, k_ref, vt_ref, o_ref, s_ref):
    qt = qt_ref[0, 0]
    tq = qt.shape[1]
    nk = k_ref.shape[2] // tk

    def scores(j, slot):
        start = pl.multiple_of(j * tk, tk)
        s = _dot(k_ref[0, 0, pl.ds(start, tk), :], qt)
        s_ref[slot] = s
        return jnp.max(s, axis=0, keepdims=True)

    def update(j, slot, mx, m, l, acc):
        start = pl.multiple_of(j * tk, tk)
        m_new = jnp.maximum(m, mx)
        p = jnp.exp2(s_ref[slot] - m_new)
        alpha = jnp.exp2(m - m_new)
        l = alpha * l + jnp.sum(p, axis=0, keepdims=True)
        acc = alpha * acc + _dot(vt_ref[0, 0, :, pl.ds(start, tk)], p.astype(BF16))
        return m_new, l, acc

    def body(i, carry):
        mx0, m, l, acc = carry
        j = 2 * i
        mx1 = scores(j + 1, 1)
        m, l, acc = update(j, 0, mx0, m, l, acc)
        mx0 = scores(j + 2, 0)
        m, l, acc = update(j + 1, 1, mx1, m, l, acc)
        return mx0, m, l, acc

    init = (scores(0, 0), jnp.full((1, tq), -1e30, F32), jnp.zeros((1, tq), F32), jnp.zeros((V_HEAD, tq), F32))
    mx0, m, l, acc = lax.fori_loop(0, nk // 2 - 1, body, init)
    mx1 = scores(nk - 1, 1)
    m, l, acc = update(nk - 2, 0, mx0, m, l, acc)
    _, l, acc = update(nk - 1, 1, mx1, m, l, acc)
    o_ref[0] = (acc / l).T.astype(BF16)


def _attention(qt, k, vt, tq, tk):
    batch, heads, seq, _ = k.shape
    return pl.pallas_call(
        functools.partial(_attn_kernel, tk),
        grid=(batch, heads, seq // tq),
        in_specs=[
            pl.BlockSpec((1, 1, QK_DIM, tq), lambda b, h, i: (b, h, 0, i)),
            pl.BlockSpec((1, 1, seq, QK_DIM), lambda b, h, i: (b, h, 0, 0)),
            pl.BlockSpec((1, 1, V_HEAD, seq), lambda b, h, i: (b, h, 0, 0)),
        ],
        out_specs=pl.BlockSpec((1, tq, V_HEAD), lambda b, h, i: (b, i, h)),
        out_shape=jax.ShapeDtypeStruct((batch, seq, heads * V_HEAD), BF16),
        scratch_shapes=[pltpu.VMEM((2, tk, tq), F32)],
        compiler_params=pltpu.CompilerParams(
            dimension_semantics=("arbitrary", "arbitrary", "arbitrary"),
            vmem_limit_bytes=VMEM_LIMIT_BYTES),
        name="mla_attention",
    )(qt, k, vt)


def _memkv_kernel(m_ref, g_ref, w_ref, k_ref, v_ref):
    d = m_ref.shape[1]
    kv = _dot(_rms(m_ref[...], g_ref[...]).astype(BF16), w_ref[...])
    k_ref[...] = kv[:, :d].astype(BF16)
    v_ref[...] = kv[:, d:].astype(BF16)


def _memkv(mem2d, g, w):
    n, d = mem2d.shape
    return pl.pallas_call(
        _memkv_kernel,
        out_shape=[jax.ShapeDtypeStruct((n, d), BF16), jax.ShapeDtypeStruct((n, d), BF16)],
        compiler_params=pltpu.CompilerParams(vmem_limit_bytes=VMEM_LIMIT_BYTES),
        name="memkv",
    )(mem2d, g, w)


def _pack_bf16_pairs(lo, hi):
    return pltpu.bitcast(pltpu.pack_elementwise([lo, hi], packed_dtype=BF16), jnp.uint32)


def _unpack_bf16_pair(words, index):
    return pltpu.unpack_elementwise(words, index=index, packed_dtype=BF16, unpacked_dtype=F32).astype(BF16)


def _post_kernel(nb, sub, x_ref, v_ref, vprev_ref, vnext_ref, gb_ref, gates_ref, att_ref,
                 cw_ref, cb_ref, wco_ref, wmo_ref, wmix_ref, gmem_ref, wmq_ref, mk_ref, mv_ref, wmout_ref,
                 gmoe_ref, wrt_ref,
                 x2_ref, hp_ref, aff_ref):
    i = pl.program_id(0)
    tm, d = x_ref.shape
    v = v_ref[...].astype(F32)
    first = (i % nb) == 0
    last = (i % nb) == nb - 1
    halo_prev = jnp.where(first, 0.0, vprev_ref[...].astype(F32)[BF16_SUBLANES - 1:BF16_SUBLANES, :])
    halo_next = jnp.where(last, 0.0, vnext_ref[...].astype(F32)[0:1, :])
    row = lax.broadcasted_iota(jnp.int32, (tm, d), 0)
    v_prev = jnp.where(row == 0, halo_prev, pltpu.roll(v, 1, 0))
    v_next = jnp.where(row == tm - 1, halo_next, pltpu.roll(v, tm - 1, 0))
    conv = cw_ref[0:1, :] * v_prev + cw_ref[1:2, :] * v + cw_ref[2:3, :] * v_next + cb_ref[...]
    conv_in = (gb_ref[...].astype(F32) * conv).astype(BF16)
    hd = d // MEM_HEADS
    for r0 in range(0, tm, sub):
        rows = pl.ds(r0, sub)
        y_conv = _dot(conv_in[r0:r0 + sub], wco_ref[...])
        y_mla = _dot(att_ref[rows, :], wmo_ref[...])
        mixed = gates_ref[rows, :d].astype(F32) * y_conv + gates_ref[rows, d:].astype(F32) * y_mla
        x1 = x_ref[rows, :] + _dot(mixed.astype(BF16), wmix_ref[...])
        qm = (_dot(_rms(x1, gmem_ref[...]).astype(BF16), wmq_ref[...]) * (hd ** -0.5)).astype(BF16)
        outs = []
        for h in range(MEM_HEADS):
            sl = slice(h * hd, (h + 1) * hd)
            s = _dot_t(qm[:, sl], mk_ref[:, sl])
            p = jnp.exp(s - jnp.max(s, axis=-1, keepdims=True))
            p = p / jnp.sum(p, axis=-1, keepdims=True)
            outs.append(_dot(p.astype(BF16), mv_ref[:, sl]).astype(BF16))
        x2 = x1 + _dot(jnp.concatenate(outs, axis=-1), wmout_ref[...])
        x2_ref[rows, :] = x2
        h3 = _rms(x2, gmoe_ref[...])
        hp_ref[rows, :] = _pack_bf16_pairs(h3[:, :d // 2], h3[:, d // 2:])
        logits = lax.dot_general(wrt_ref[...], h3, (((1,), (1,)), ((), ())),
                                 precision=lax.Precision.HIGHEST, preferred_element_type=F32)
        e = jnp.exp(logits - jnp.max(logits, axis=0, keepdims=True))
        aff_ref[0, :, rows] = e / jnp.sum(e, axis=0, keepdims=True)


def _post(x2d, batch, seq, v, gb, gates, att, cw, cb, wco, wmo, wmix, gmem, wmq, mk, mv, wmout, gmoe, wrt, tm):
    t, d = x2d.shape
    nb = seq // tm
    hb = tm // BF16_SUBLANES
    n_halo = t // BF16_SUBLANES
    tok = lambda i: (i, 0)
    mem_len = mk.shape[0] // batch
    in_specs = [
        pl.BlockSpec((tm, d), tok), pl.BlockSpec((tm, d), tok),
        pl.BlockSpec((BF16_SUBLANES, d), lambda i: (jnp.maximum(i * hb - 1, 0), 0)),
        pl.BlockSpec((BF16_SUBLANES, d), lambda i: (jnp.minimum((i + 1) * hb, n_halo - 1), 0)),
        pl.BlockSpec((tm, d), tok), pl.BlockSpec((tm, 2 * d), tok), pl.BlockSpec((tm, d), tok),
        _const_spec(cw.shape), _const_spec(cb.shape), _const_spec(wco.shape), _const_spec(wmo.shape),
        _const_spec(wmix.shape), _const_spec(gmem.shape), _const_spec(wmq.shape),
        pl.BlockSpec((mem_len, d), lambda i: (i // nb, 0)),
        pl.BlockSpec((mem_len, d), lambda i: (i // nb, 0)),
        _const_spec(wmout.shape), _const_spec(gmoe.shape), _const_spec(wrt.shape),
    ]
    out_shape = [
        jax.ShapeDtypeStruct((t, d), F32),
        jax.ShapeDtypeStruct((t, d // 2), jnp.uint32),
        jax.ShapeDtypeStruct((batch, N_EXPERTS, seq), F32),
    ]
    out_specs = [
        pl.BlockSpec((tm, d), tok), pl.BlockSpec((tm, d // 2), tok),
        pl.BlockSpec((1, N_EXPERTS, tm), lambda i: (i // nb, 0, i % nb)),
    ]
    return pl.pallas_call(
        functools.partial(_post_kernel, nb, min(tm, POST_SUBTILE)),
        grid=(t // tm,), in_specs=in_specs, out_specs=out_specs, out_shape=out_shape,
        compiler_params=pltpu.CompilerParams(dimension_semantics=("arbitrary",),
                                             vmem_limit_bytes=VMEM_LIMIT_BYTES),
        name="post",
    )(x2d, v, v, v, gb, gates, att, cw, cb, wco, wmo, wmix, gmem, wmq, mk, mv, wmout, gmoe, wrt)


def _select_kernel(cap, aff_ref, key_ref, offs_ref):
    rows, seq = aff_ref.shape
    n_chunks = seq // LANES
    aff = aff_ref[...]

    def count_ge(x):
        return jnp.sum((aff >= x).astype(F32), axis=-1, keepdims=True)

    def search(b, t):
        cand = t | (jnp.int32(1) << (30 - b))
        return jnp.where(count_ge(pltpu.bitcast(cand, F32)) >= cap, cand, t)

    thr = lax.fori_loop(0, 31, search, jnp.zeros((rows, 1), jnp.int32))

    def refine(_, lohi):
        lo, hi = lohi
        mid = 0.5 * (lo + hi)
        take = count_ge(mid) >= cap
        return jnp.where(take, mid, lo), jnp.where(take, hi, mid)

    lo, hi = lax.fori_loop(0, 30, refine, (pltpu.bitcast(thr, F32), pltpu.bitcast(thr + 1, F32)))
    gt = aff >= hi
    eq = (aff >= lo) & (aff < hi)
    need = cap - jnp.sum(gt.astype(F32), axis=-1, keepdims=True)

    tri = (lax.broadcasted_iota(jnp.int32, (LANES, LANES), 0)
           <= lax.broadcasted_iota(jnp.int32, (LANES, LANES), 1)).astype(BF16)
    ones = jnp.ones((LANES, LANES), BF16)
    lane = lax.broadcasted_iota(jnp.int32, (rows, LANES), 1)
    run_eq = jnp.zeros((rows, LANES), F32)
    run_sel = jnp.zeros((rows, LANES), F32)
    offs = jnp.zeros((rows, LANES), F32)
    for j in range(n_chunks):
        sl = slice(j * LANES, (j + 1) * LANES)
        eq_j = eq[:, sl]
        eq_b = eq_j.astype(F32).astype(BF16)
        eq_rank = _dot(eq_b, tri) + run_eq
        run_eq = run_eq + _dot(eq_b, ones)
        sel_j = gt[:, sl] | (eq_j & (eq_rank <= need))
        sel_b = sel_j.astype(F32).astype(BF16)
        pos = _dot(sel_b, tri) + run_sel
        run_sel = run_sel + _dot(sel_b, ones)
        key_ref[:, sl] = jnp.where(sel_j, pos, 0.0)
        offs = jnp.where(lane == j, run_sel, offs)
    offs_ref[...] = offs.astype(jnp.int32)


def _compact_kernel(cap, n_chunks, offs_ref, key_ref, aff_ref, idx_ref, gsel_ref):
    r = pl.program_id(0)
    n_cblk = cap // LANES
    slot = lax.broadcasted_iota(jnp.int32, (LANES, LANES), 0) + 1
    lane = lax.broadcasted_iota(jnp.int32, (LANES, LANES), 1)
    obase = r * LANES
    j0 = jnp.int32(0)
    for cb in range(n_cblk):
        want = (slot + cb * LANES).astype(F32)
        j0 = lax.while_loop(lambda j: (j < n_chunks - 1) & (offs_ref[obase + j] <= cb * LANES),
                            lambda j: j + 1, j0)
        j1 = lax.while_loop(lambda j: (j < n_chunks - 1) & (offs_ref[obase + j] < (cb + 1) * LANES),
                            lambda j: j + 1, j0)

        def per_chunk(j, carry):
            tok_sel, g_sel = carry
            start = pl.multiple_of(j * LANES, LANES)
            hit = key_ref[0, :, pl.ds(start, LANES)] == want
            tok_sel = jnp.where(hit, (lane + j * LANES).astype(F32), tok_sel)
            g_sel = jnp.where(hit, aff_ref[0, :, pl.ds(start, LANES)], g_sel)
            return tok_sel, g_sel

        zero = jnp.zeros((LANES, LANES), F32)
        tok_sel, g_sel = lax.fori_loop(j0, j1 + 1, per_chunk, (zero, zero))
        idx_ref[0, cb:cb + 1, :] = jnp.sum(tok_sel.T, axis=0, keepdims=True).astype(jnp.int32)
        gsel_ref[0, cb:cb + 1, :] = jnp.sum(g_sel.T, axis=0, keepdims=True)
        j0 = j1


def _route(aff2d, cap):
    rows, seq = aff2d.shape
    n_chunks = seq // LANES
    n_cblk = cap // LANES
    assert n_chunks <= LANES
    key, offs = pl.pallas_call(
        functools.partial(_select_kernel, cap),
        out_shape=[jax.ShapeDtypeStruct((rows, seq), F32), jax.ShapeDtypeStruct((rows, LANES), jnp.int32)],
        compiler_params=pltpu.CompilerParams(vmem_limit_bytes=VMEM_LIMIT_BYTES),
        name="route_select",
    )(aff2d)
    row_blk = lambda r, offs: (r, 0, 0)
    grid_spec = pltpu.PrefetchScalarGridSpec(
        num_scalar_prefetch=1, grid=(rows,),
        in_specs=[pl.BlockSpec((1, 1, seq), row_blk), pl.BlockSpec((1, 1, seq), row_blk)],
        out_specs=[pl.BlockSpec((1, n_cblk, LANES), row_blk), pl.BlockSpec((1, n_cblk, LANES), row_blk)],
    )
    return pl.pallas_call(
        functools.partial(_compact_kernel, cap, n_chunks),
        grid_spec=grid_spec,
        out_shape=[jax.ShapeDtypeStruct((rows, n_cblk, LANES), jnp.int32),
                   jax.ShapeDtypeStruct((rows, n_cblk, LANES), F32)],
        compiler_params=pltpu.CompilerParams(dimension_semantics=("arbitrary",),
                                             vmem_limit_bytes=VMEM_LIMIT_BYTES),
        name="route_compact",
    )(offs.reshape(-1), key.reshape(rows, 1, seq), aff2d.reshape(rows, 1, seq))


def _experts_kernel(cap, tc, idx_ref, hp_ref, wg_ref, wu_ref, wd_ref, y_ref, xg_ref):
    b = pl.program_id(0)
    e = pl.program_id(1)
    base = (b * N_EXPERTS + e) * cap
    half = hp_ref.shape[2]

    def gather(c, _):
        xg_ref[pl.ds(c, 1), :] = hp_ref[0, pl.ds(idx_ref[base + c], 1), :]
        return 0

    lax.fori_loop(0, cap, gather, 0, unroll=8)

    def chunk(ci, _):
        start = pl.multiple_of(ci * tc, tc)
        words = xg_ref[pl.ds(start, tc), :]
        x_lo = _unpack_bf16_pair(words, 0)
        x_hi = _unpack_bf16_pair(words, 1)
        gate = _dot(x_lo, wg_ref[0, :half, :]) + _dot(x_hi, wg_ref[0, half:, :])
        up = _dot(x_lo, wu_ref[0, :half, :]) + _dot(x_hi, wu_ref[0, half:, :])
        act = (gate * jax.nn.sigmoid(gate) * up).astype(BF16)
        y_ref[0, 0, pl.ds(start, tc), :] = _dot(act, wd_ref[0])
        return 0

    lax.fori_loop(0, cap // tc, chunk, 0)


def _experts(idx_flat, hp, wg, wu, wd, cap, tc):
    batch, seq, half = hp.shape
    d = 2 * half
    ff = wg.shape[2]
    grid_spec = pltpu.PrefetchScalarGridSpec(
        num_scalar_prefetch=1,
        grid=(batch, N_EXPERTS),
        in_specs=[
            pl.BlockSpec((1, seq, half), lambda b, e, idx: (b, 0, 0), pipeline_mode=pl.Buffered(1)),
            pl.BlockSpec((1, d, ff), lambda b, e, idx: (e, 0, 0)),
            pl.BlockSpec((1, d, ff), lambda b, e, idx: (e, 0, 0)),
            pl.BlockSpec((1, ff, d), lambda b, e, idx: (e, 0, 0)),
        ],
        out_specs=pl.BlockSpec((1, 1, cap, d), lambda b, e, idx: (b, e, 0, 0)),
        scratch_shapes=[pltpu.VMEM((cap, half), jnp.uint32)],
    )
    return pl.pallas_call(
        functools.partial(_experts_kernel, cap, tc),
        grid_spec=grid_spec,
        out_shape=jax.ShapeDtypeStruct((batch, N_EXPERTS, cap, d), F32),
        compiler_params=pltpu.CompilerParams(dimension_semantics=("arbitrary", "arbitrary"),
                                             vmem_limit_bytes=VMEM_LIMIT_BYTES),
        name="experts",
    )(idx_flat, hp, wg, wu, wd)


def _combine_kernel(cap, idx_ref, gsel_ref, y_ref, o_ref):
    b = pl.program_id(0)
    e = pl.program_id(2)
    base = (b * N_EXPERTS + e) * cap

    @pl.when(e == 0)
    def _():
        o_ref[...] = jnp.zeros_like(o_ref)

    def scatter(c, _):
        tok = idx_ref[base + c]
        o_ref[0, pl.ds(tok, 1), :] += gsel_ref[base + c] * y_ref[0, 0, pl.ds(c, 1), :]
        return 0

    lax.fori_loop(0, cap, scatter, 0, unroll=8)


def _combine(idx_flat, gsel_flat, y, seq, n_split):
    batch, _, cap, d = y.shape
    dw = d // n_split
    grid_spec = pltpu.PrefetchScalarGridSpec(
        num_scalar_prefetch=2,
        grid=(batch, n_split, N_EXPERTS),
        in_specs=[pl.BlockSpec((1, 1, cap, dw), lambda b, s, e, idx, gs: (b, e, 0, s))],
        out_specs=pl.BlockSpec((1, seq, dw), lambda b, s, e, idx, gs: (b, 0, s)),
    )
    return pl.pallas_call(
        functools.partial(_combine_kernel, cap),
        grid_spec=grid_spec,
        out_shape=jax.ShapeDtypeStruct((batch, seq, d), F32),
        compiler_params=pltpu.CompilerParams(
            dimension_semantics=("arbitrary", "arbitrary", "arbitrary"),
            vmem_limit_bytes=VMEM_LIMIT_BYTES),
        name="combine",
    )(idx_flat, gsel_flat, y)


def _final_kernel(apply_norm, x_ref, m_ref, g_ref, o_ref):
    x = x_ref[...] + m_ref[...]
    o_ref[...] = _rms(x, g_ref[...]) if apply_norm else x


def _final(x2d, moe2d, g, apply_norm, tm):
    t, d = x2d.shape
    tok = lambda i: (i, 0)
    return pl.pallas_call(
        functools.partial(_final_kernel, apply_norm),
        grid=(t // tm,),
        in_specs=[pl.BlockSpec((tm, d), tok), pl.BlockSpec((tm, d), tok), _const_spec(g.shape)],
        out_specs=pl.BlockSpec((tm, d), tok),
        out_shape=jax.ShapeDtypeStruct((t, d), F32),
        compiler_params=pltpu.CompilerParams(dimension_semantics=("arbitrary",),
                                             vmem_limit_bytes=VMEM_LIMIT_BYTES),
        name="final",
    )(x2d, moe2d, g)


def _rot_cols(w):
    half = w.shape[-1] // 2
    return jnp.concatenate([-w[..., half:], w[..., :half]], axis=-1)


def _tile(n, pref):
    return pref if n % pref == 0 else n


def kernel(x, mem, norm_mix_g, w_in, conv_w, conv_b, w_conv_out, q_norm_g, w_uq, kv_norm_g, w_ukv,
           w_mla_out, b_gate, w_mix_out, norm_mem_g, norm_memkv_g, w_mem_q, w_mem_kv, w_mem_out,
           norm_moe_g, w_router, w_exp_gate, w_exp_up, w_exp_down, norm_final_g):
    batch, seq, d = x.shape
    depth = w_in.shape[0]
    t = batch * seq
    q_lora = q_norm_g.shape[1]
    kv_lora = kv_norm_g.shape[1]
    cap = max(1, CAPACITY_FACTOR * seq // N_EXPERTS)
    assert cap % LANES == 0 and seq % LANES == 0 and d % (2 * LANES) == 0

    tm_in = _tile(seq, 512)
    tm_post = _tile(seq, 512)
    tq = _tile(seq, 1024)
    tk = _tile(seq, 512)
    tc = _tile(cap, 256)

    inv = 1.0 / (ROPE_THETA ** (jnp.arange(0, QK_ROPE, 2, dtype=F32) / QK_ROPE))
    ang = jnp.arange(seq, dtype=F32)[:, None] * inv[None, :]
    cos2 = jnp.concatenate([jnp.cos(ang), jnp.cos(ang)], axis=-1)
    sin2 = jnp.concatenate([jnp.sin(ang), jnp.sin(ang)], axis=-1)
    qf = (QK_DIM ** -0.5) * math.log2(math.e)
    qc = qf * jnp.concatenate([jnp.ones((seq, QK_NOPE), F32), cos2], axis=-1)
    qs = qf * jnp.concatenate([jnp.zeros((seq, QK_NOPE), F32), sin2], axis=-1)

    o_cq = 3 * d
    o_ckv = o_cq + q_lora
    o_kr = o_ckv + kv_lora
    o_gl = o_kr + QK_ROPE
    cols = {"xc": 0, "gb": d, "gc": 2 * d, "cq": o_cq, "ckv": o_ckv, "glog": o_ckv + kv_lora,
            "kr": o_ckv + kv_lora + 2 * d}

    x2d = x.reshape(t, d)
    mem2d = mem.reshape(-1, d)
    row = lambda a: a.reshape(1, -1)

    for l in range(depth):
        w_kr = w_in[l][:, o_kr:o_gl]
        win = jnp.concatenate([w_in[l][:, :o_kr], w_in[l][:, o_gl:], w_kr, _rot_cols(w_kr)], axis=1).astype(BF16)
        wq3 = w_uq[l].reshape(q_lora, MLA_HEADS, QK_DIM)
        wq3r = jnp.concatenate([jnp.zeros_like(wq3[..., :QK_NOPE]), _rot_cols(wq3[..., QK_NOPE:])], axis=-1)
        wq = wq3.transpose(1, 2, 0).astype(BF16)
        wqr = wq3r.transpose(1, 2, 0).astype(BF16)
        wkv3 = w_ukv[l].reshape(kv_lora, MLA_HEADS, QK_NOPE + V_HEAD)
        wk = wkv3[..., :QK_NOPE].transpose(1, 0, 2).astype(BF16)
        wv = wkv3[..., QK_NOPE:].transpose(1, 2, 0).astype(BF16)

        v, gb, gates, qt, k, vt = _inproj(
            x2d, batch, seq, row(norm_mix_g[l]), win, cols, row(q_norm_g[l]), row(kv_norm_g[l]),
            row(b_gate[l]), wq, wqr, wk, wv, qc.T, qs.T, cos2, sin2, tm_in)
        att = _attention(qt, k, vt, tq, tk).reshape(t, MLA_HEADS * V_HEAD)
        mk, mv = _memkv(mem2d, row(norm_memkv_g[l]), w_mem_kv[l].astype(BF16))
        x2, hp, aff = _post(
            x2d, batch, seq, v, gb, gates, att, conv_w[l], row(conv_b[l]), w_conv_out[l].astype(BF16),
            w_mla_out[l].astype(BF16), w_mix_out[l].astype(BF16), row(norm_mem_g[l]),
            w_mem_q[l].astype(BF16), mk, mv, w_mem_out[l].astype(BF16), row(norm_moe_g[l]),
            w_router[l].T, tm_post)
        idx, gsel = _route(aff.reshape(batch * N_EXPERTS, seq), cap)
        idx_flat = idx.reshape(-1)
        y = _experts(idx_flat, hp.reshape(batch, seq, d // 2), w_exp_gate[l].astype(BF16),
                     w_exp_up[l].astype(BF16), w_exp_down[l].astype(BF16), cap, tc)
        moe = _combine(idx_flat, gsel.reshape(-1), y, seq, 2)
        last = l == depth - 1
        x2d = _final(x2, moe.reshape(t, d), row(norm_final_g), last, tm_in)
    return x2d.reshape(batch, seq, d)
```

```python
import functools
import math

import jax
import jax.numpy as jnp
from jax import lax
from jax.experimental import pallas as pl
from jax.experimental.pallas import tpu as pltpu

MLA_HEADS = 8
QK_NOPE = 128
QK_ROPE = 64
QK_DIM = QK_NOPE + QK_ROPE
V_HEAD = 128
ROPE_THETA = 10000.0
MEM_HEADS = 4
N_EXPERTS = 16
CAPACITY_FACTOR = 2
CONV_WIDTH = 3
EPS = 1e-6

LANES = 128
SUBLANES = 8
BF16_SUBLANES = 16
VMEM_LIMIT_BYTES = 56 * 1024 * 1024
POST_SUBTILE = 256

F32 = jnp.float32
BF16 = jnp.bfloat16


def _const_spec(shape):
    nd = len(shape)
    return pl.BlockSpec(shape, lambda *_: (0,) * nd, pipeline_mode=pl.Buffered(1))


def _rms(x, g):
    return x * lax.rsqrt(jnp.mean(x * x, axis=-1, keepdims=True) + EPS) * g


def _dot(a, b):
    return jnp.dot(a, b, preferred_element_type=F32)


def _dot_t(a, b):
    return lax.dot_general(a, b, (((1,), (1,)), ((), ())), preferred_element_type=F32)


def _inproj_kernel(cols, x_ref, g_ref, win_ref, qg_ref, kvg_ref, bg_ref, wq_ref, wqr_ref, wk_ref, wv_ref,
                   qc_ref, qs_ref, kc_ref, ks_ref,
                   v_ref, gb_ref, gates_ref, qt_ref, k_ref, vt_ref):
    d = x_ref.shape[1]
    hb = _rms(x_ref[...], g_ref[...]).astype(BF16)

    def proj(name, width):
        lo = cols[name]
        return _dot(hb, win_ref[:, lo:lo + width])

    v_ref[...] = (proj("gc", d) * proj("xc", d)).astype(BF16)
    gb_ref[...] = proj("gb", d).astype(BF16)
    gates_ref[...] = jax.nn.sigmoid(proj("glog", 2 * d) + bg_ref[...]).astype(BF16)

    q_lora = qg_ref.shape[1]
    kv_lora = kvg_ref.shape[1]
    qn = _rms(proj("cq", q_lora), qg_ref[...]).astype(BF16)
    kvn = _rms(proj("ckv", kv_lora), kvg_ref[...]).astype(BF16)
    kr2 = proj("kr", 2 * QK_ROPE)
    k_rope = kr2[:, :QK_ROPE] * kc_ref[...] + kr2[:, QK_ROPE:] * ks_ref[...]
    qc = qc_ref[...]
    qs = qs_ref[...]
    for h in range(MLA_HEADS):
        qt_h = _dot_t(wq_ref[h], qn) * qc + _dot_t(wqr_ref[h], qn) * qs
        qt_ref[0, h] = qt_h.astype(BF16)
        k_ref[0, h] = jnp.concatenate([_dot(kvn, wk_ref[h]), k_rope], axis=-1).astype(BF16)
        vt_ref[0, h] = _dot_t(wv_ref[h], kvn).astype(BF16)


def _inproj(x2d, batch, seq, g, win, cols, qg, kvg, bg, wq, wqr, wk, wv, qc, qs, kc, ks, tm):
    t, d = x2d.shape
    nb = seq // tm
    tok = lambda i: (i, 0)
    pos = lambda i: (i % nb, 0)
    pos_t = lambda i: (0, i % nb)
    head_blk = lambda i: (i // nb, 0, i % nb, 0)
    head_blk_t = lambda i: (i // nb, 0, 0, i % nb)
    in_specs = [
        pl.BlockSpec((tm, d), tok),
        _const_spec(g.shape), _const_spec(win.shape), _const_spec(qg.shape), _const_spec(kvg.shape),
        _const_spec(bg.shape), _const_spec(wq.shape), _const_spec(wqr.shape), _const_spec(wk.shape),
        _const_spec(wv.shape),
        pl.BlockSpec((QK_DIM, tm), pos_t), pl.BlockSpec((QK_DIM, tm), pos_t),
        pl.BlockSpec((tm, QK_ROPE), pos), pl.BlockSpec((tm, QK_ROPE), pos),
    ]
    out_shape = [
        jax.ShapeDtypeStruct((t, d), BF16),
        jax.ShapeDtypeStruct((t, d), BF16),
        jax.ShapeDtypeStruct((t, 2 * d), BF16),
        jax.ShapeDtypeStruct((batch, MLA_HEADS, QK_DIM, seq), BF16),
        jax.ShapeDtypeStruct((batch, MLA_HEADS, seq, QK_DIM), BF16),
        jax.ShapeDtypeStruct((batch, MLA_HEADS, V_HEAD, seq), BF16),
    ]
    out_specs = [
        pl.BlockSpec((tm, d), tok), pl.BlockSpec((tm, d), tok), pl.BlockSpec((tm, 2 * d), tok),
        pl.BlockSpec((1, MLA_HEADS, QK_DIM, tm), head_blk_t),
        pl.BlockSpec((1, MLA_HEADS, tm, QK_DIM), head_blk),
        pl.BlockSpec((1, MLA_HEADS, V_HEAD, tm), head_blk_t),
    ]
    return pl.pallas_call(
        functools.partial(_inproj_kernel, cols),
        grid=(t // tm,), in_specs=in_specs, out_specs=out_specs, out_shape=out_shape,
        compiler_params=pltpu.CompilerParams(dimension_semantics=("arbitrary",),
                                             vmem_limit_bytes=VMEM_LIMIT_BYTES),
        name="inproj",
    )(x2d, g, win, qg, kvg, bg, wq, wqr, wk, wv, qc, qs, kc, ks)


def _attn_kernel(tk, unroll, qt_ref, k_ref, vt_ref, o_ref, s_ref):
    qt = qt_ref[0, 0]
    tq = qt.shape[1]
    nk = k_ref.shape[2] // tk

    def scores(j, slot):
        start = pl.multiple_of(j * tk, tk)
        s = _dot(k_ref[0, 0, pl.ds(start, tk), :], qt)
        s_ref[slot] = s
        return jnp.max(s, axis=0, keepdims=True)

    def update(j, slot, mx, m, l, acc):
        start = pl.multiple_of(j * tk, tk)
        m_new = jnp.maximum(m, mx)
        p = jnp.exp2(s_ref[slot] - m_new)
        alpha = jnp.exp2(m - m_new)
        l = alpha * l + jnp.sum(p, axis=0, keepdims=True)
        acc = alpha * acc + _dot(vt_ref[0, 0, :, pl.ds(start, tk)], p.astype(BF16))
        return m_new, l, acc

    def chunks(j0, carry, last):
        mx, m, l, acc = carry
        for u in range(unroll):
            if not (last and u == unroll - 1):
                mx_next = scores(j0 + u + 1, (u + 1) % 2)
            m, l, acc = update(j0 + u, u % 2, mx, m, l, acc)
            mx = mx_next
        return mx, m, l, acc

    init = (scores(0, 0), jnp.full((1, tq), -1e30, F32), jnp.zeros((1, tq), F32), jnp.zeros((V_HEAD, tq), F32))
    carry = lax.fori_loop(0, nk // unroll - 1, lambda i, c: chunks(unroll * i, c, False), init)
    _, _, l, acc = chunks(nk - unroll, carry, True)
    o_ref[0] = (acc / l).T.astype(BF16)


def _attention(qt, k, vt, tq, tk):
    batch, heads, seq, _ = k.shape
    nk = seq // tk
    assert nk % 2 == 0
    unroll = 4 if nk % 4 == 0 else 2
    return pl.pallas_call(
        functools.partial(_attn_kernel, tk, unroll),
        grid=(batch, heads, seq // tq),
        in_specs=[
            pl.BlockSpec((1, 1, QK_DIM, tq), lambda b, h, i: (b, h, 0, i)),
            pl.BlockSpec((1, 1, seq, QK_DIM), lambda b, h, i: (b, h, 0, 0)),
            pl.BlockSpec((1, 1, V_HEAD, seq), lambda b, h, i: (b, h, 0, 0)),
        ],
        out_specs=pl.BlockSpec((1, tq, V_HEAD), lambda b, h, i: (b, i, h)),
        out_shape=jax.ShapeDtypeStruct((batch, seq, heads * V_HEAD), BF16),
        scratch_shapes=[pltpu.VMEM((2, tk, tq), F32)],
        compiler_params=pltpu.CompilerParams(
            dimension_semantics=("arbitrary", "arbitrary", "arbitrary"),
            vmem_limit_bytes=VMEM_LIMIT_BYTES),
        name="mla_attention",
    )(qt, k, vt)


def _memkv_kernel(m_ref, g_ref, w_ref, k_ref, v_ref):
    d = m_ref.shape[1]
    kv = _dot(_rms(m_ref[...], g_ref[...]).astype(BF16), w_ref[...])
    k_ref[...] = kv[:, :d].astype(BF16)
    v_ref[...] = kv[:, d:].astype(BF16)


def _memkv(mem2d, g, w):
    n, d = mem2d.shape
    return pl.pallas_call(
        _memkv_kernel,
        out_shape=[jax.ShapeDtypeStruct((n, d), BF16), jax.ShapeDtypeStruct((n, d), BF16)],
        compiler_params=pltpu.CompilerParams(vmem_limit_bytes=VMEM_LIMIT_BYTES),
        name="memkv",
    )(mem2d, g, w)


def _pack_bf16_pairs(lo, hi):
    return pltpu.bitcast(pltpu.pack_elementwise([lo, hi], packed_dtype=BF16), jnp.uint32)


def _unpack_bf16_pair(words, index):
    return pltpu.unpack_elementwise(words, index=index, packed_dtype=BF16, unpacked_dtype=F32).astype(BF16)


def _post_kernel(nb, sub, x_ref, v_ref, vprev_ref, vnext_ref, gb_ref, gates_ref, att_ref,
                 cw_ref, cb_ref, wco_ref, wmo_ref, wmix_ref, gmem_ref, wmq_ref, mk_ref, mv_ref, wmout_ref,
                 gmoe_ref, wrh_ref, wrl_ref,
                 x2_ref, hp_ref, aff_ref):
    i = pl.program_id(0)
    tm, d = x_ref.shape
    v = v_ref[...].astype(F32)
    first = (i % nb) == 0
    last = (i % nb) == nb - 1
    halo_prev = jnp.where(first, 0.0, vprev_ref[...].astype(F32)[BF16_SUBLANES - 1:BF16_SUBLANES, :])
    halo_next = jnp.where(last, 0.0, vnext_ref[...].astype(F32)[0:1, :])
    row = lax.broadcasted_iota(jnp.int32, (tm, d), 0)
    v_prev = jnp.where(row == 0, halo_prev, pltpu.roll(v, 1, 0))
    v_next = jnp.where(row == tm - 1, halo_next, pltpu.roll(v, tm - 1, 0))
    conv = cw_ref[0:1, :] * v_prev + cw_ref[1:2, :] * v + cw_ref[2:3, :] * v_next + cb_ref[...]
    conv_in = (gb_ref[...].astype(F32) * conv).astype(BF16)
    hd = d // MEM_HEADS
    tiles = [pl.ds(r0, sub) for r0 in range(0, tm, sub)]
    y_conv = [_dot(conv_in[r0:r0 + sub], wco_ref[...]) for r0 in range(0, tm, sub)]
    y_mla = [_dot(att_ref[rows, :], wmo_ref[...]) for rows in tiles]
    mixed = [(gates_ref[rows, :d].astype(F32) * yc + gates_ref[rows, d:].astype(F32) * ym).astype(BF16)
             for rows, yc, ym in zip(tiles, y_conv, y_mla)]
    x1 = [x_ref[rows, :] + _dot(mx, wmix_ref[...]) for rows, mx in zip(tiles, mixed)]
    qm = [(_dot(_rms(x, gmem_ref[...]).astype(BF16), wmq_ref[...]) * (hd ** -0.5)).astype(BF16) for x in x1]
    att_m = []
    for q in qm:
        outs = []
        for h in range(MEM_HEADS):
            sl = slice(h * hd, (h + 1) * hd)
            s = _dot_t(q[:, sl], mk_ref[:, sl])
            p = jnp.exp(s - jnp.max(s, axis=-1, keepdims=True))
            p = p / jnp.sum(p, axis=-1, keepdims=True)
            outs.append(_dot(p.astype(BF16), mv_ref[:, sl]).astype(BF16))
        att_m.append(jnp.concatenate(outs, axis=-1))
    x2 = [x + _dot(o, wmout_ref[...]) for x, o in zip(x1, att_m)]
    for rows, x in zip(tiles, x2):
        x2_ref[rows, :] = x
        h3 = _rms(x, gmoe_ref[...])
        hp_ref[rows, :] = _pack_bf16_pairs(h3[:, :d // 2], h3[:, d // 2:])
        h_hi = h3.astype(BF16)
        h_lo = (h3 - h_hi.astype(F32)).astype(BF16)
        logits = (_dot_t(wrh_ref[...], h_hi) + _dot_t(wrh_ref[...], h_lo)
                  + _dot_t(wrl_ref[...], h_hi) + _dot_t(wrl_ref[...], h_lo))
        e = jnp.exp(logits - jnp.max(logits, axis=0, keepdims=True))
        aff_ref[0, :, rows] = e / jnp.sum(e, axis=0, keepdims=True)


def _post(x2d, batch, seq, v, gb, gates, att, cw, cb, wco, wmo, wmix, gmem, wmq, mk, mv, wmout, gmoe, wrh, wrl, tm):
    t, d = x2d.shape
    nb = seq // tm
    hb = tm // BF16_SUBLANES
    n_halo = t // BF16_SUBLANES
    tok = lambda i: (i, 0)
    mem_len = mk.shape[0] // batch
    in_specs = [
        pl.BlockSpec((tm, d), tok), pl.BlockSpec((tm, d), tok),
        pl.BlockSpec((BF16_SUBLANES, d), lambda i: (jnp.maximum(i * hb - 1, 0), 0)),
        pl.BlockSpec((BF16_SUBLANES, d), lambda i: (jnp.minimum((i + 1) * hb, n_halo - 1), 0)),
        pl.BlockSpec((tm, d), tok), pl.BlockSpec((tm, 2 * d), tok), pl.BlockSpec((tm, d), tok),
        _const_spec(cw.shape), _const_spec(cb.shape), _const_spec(wco.shape), _const_spec(wmo.shape),
        _const_spec(wmix.shape), _const_spec(gmem.shape), _const_spec(wmq.shape),
        pl.BlockSpec((mem_len, d), lambda i: (i // nb, 0)),
        pl.BlockSpec((mem_len, d), lambda i: (i // nb, 0)),
        _const_spec(wmout.shape), _const_spec(gmoe.shape), _const_spec(wrh.shape), _const_spec(wrl.shape),
    ]
    out_shape = [
        jax.ShapeDtypeStruct((t, d), F32),
        jax.ShapeDtypeStruct((t, d // 2), jnp.uint32),
        jax.ShapeDtypeStruct((batch, N_EXPERTS, seq), F32),
    ]
    out_specs = [
        pl.BlockSpec((tm, d), tok), pl.BlockSpec((tm, d // 2), tok),
        pl.BlockSpec((1, N_EXPERTS, tm), lambda i: (i // nb, 0, i % nb)),
    ]
    return pl.pallas_call(
        functools.partial(_post_kernel, nb, min(tm, POST_SUBTILE)),
        grid=(t // tm,), in_specs=in_specs, out_specs=out_specs, out_shape=out_shape,
        compiler_params=pltpu.CompilerParams(dimension_semantics=("arbitrary",),
                                             vmem_limit_bytes=VMEM_LIMIT_BYTES),
        name="post",
    )(x2d, v, v, v, gb, gates, att, cw, cb, wco, wmo, wmix, gmem, wmq, mk, mv, wmout, gmoe, wrh, wrl)


def _select_kernel(cap, aff_ref, key_ref, offs_ref):
    rows, seq = aff_ref.shape
    n_chunks = seq // LANES
    aff = aff_ref[...]

    def count_ge(x):
        return jnp.sum((aff >= x).astype(F32), axis=-1, keepdims=True)

    def search(b, t):
        cand = t | (jnp.int32(1) << (30 - b))
        return jnp.where(count_ge(pltpu.bitcast(cand, F32)) >= cap, cand, t)

    thr = lax.fori_loop(0, 31, search, jnp.zeros((rows, 1), jnp.int32))

    def refine(_, lohi):
        lo, hi = lohi
        mid = 0.5 * (lo + hi)
        take = count_ge(mid) >= cap
        return jnp.where(take, mid, lo), jnp.where(take, hi, mid)

    lo, hi = lax.fori_loop(0, 30, refine, (pltpu.bitcast(thr, F32), pltpu.bitcast(thr + 1, F32)))
    gt = aff >= hi
    eq = (aff >= lo) & (aff < hi)
    need = cap - jnp.sum(gt.astype(F32), axis=-1, keepdims=True)

    tri = (lax.broadcasted_iota(jnp.int32, (LANES, LANES), 0)
           <= lax.broadcasted_iota(jnp.int32, (LANES, LANES), 1)).astype(BF16)
    ones = jnp.ones((LANES, LANES), BF16)
    lane = lax.broadcasted_iota(jnp.int32, (rows, LANES), 1)
    run_eq = jnp.zeros((rows, LANES), F32)
    run_sel = jnp.zeros((rows, LANES), F32)
    offs = jnp.zeros((rows, LANES), F32)
    for j in range(n_chunks):
        sl = slice(j * LANES, (j + 1) * LANES)
        eq_j = eq[:, sl]
        eq_b = eq_j.astype(F32).astype(BF16)
        eq_rank = _dot(eq_b, tri) + run_eq
        run_eq = run_eq + _dot(eq_b, ones)
        sel_j = gt[:, sl] | (eq_j & (eq_rank <= need))
        sel_b = sel_j.astype(F32).astype(BF16)
        pos = _dot(sel_b, tri) + run_sel
        run_sel = run_sel + _dot(sel_b, ones)
        key_ref[:, sl] = jnp.where(sel_j, pos, 0.0)
        offs = jnp.where(lane == j, run_sel, offs)
    offs_ref[...] = offs.astype(jnp.int32)


def _compact_kernel(cap, n_chunks, offs_ref, key_ref, aff_ref, idx_ref, gsel_ref):
    r = pl.program_id(0)
    n_cblk = cap // LANES
    slot = lax.broadcasted_iota(jnp.int32, (LANES, LANES), 0) + 1
    lane = lax.broadcasted_iota(jnp.int32, (LANES, LANES), 1)
    obase = r * LANES
    j0 = jnp.int32(0)
    for cb in range(n_cblk):
        want = (slot + cb * LANES).astype(F32)
        j0 = lax.while_loop(lambda j: (j < n_chunks - 1) & (offs_ref[obase + j] <= cb * LANES),
                            lambda j: j + 1, j0)
        j1 = lax.while_loop(lambda j: (j < n_chunks - 1) & (offs_ref[obase + j] < (cb + 1) * LANES),
                            lambda j: j + 1, j0)

        def per_chunk(j, carry):
            tok_sel, g_sel = carry
            start = pl.multiple_of(j * LANES, LANES)
            hit = key_ref[0, :, pl.ds(start, LANES)] == want
            tok_sel = jnp.where(hit, (lane + j * LANES).astype(F32), tok_sel)
            g_sel = jnp.where(hit, aff_ref[0, :, pl.ds(start, LANES)], g_sel)
            return tok_sel, g_sel

        zero = jnp.zeros((LANES, LANES), F32)
        tok_sel, g_sel = lax.fori_loop(j0, j1 + 1, per_chunk, (zero, zero))
        idx_ref[0, cb:cb + 1, :] = jnp.sum(tok_sel.T, axis=0, keepdims=True).astype(jnp.int32)
        gsel_ref[0, cb:cb + 1, :] = jnp.sum(g_sel.T, axis=0, keepdims=True)
        j0 = j1


def _route(aff2d, cap):
    rows, seq = aff2d.shape
    n_chunks = seq // LANES
    n_cblk = cap // LANES
    assert n_chunks <= LANES
    key, offs = pl.pallas_call(
        functools.partial(_select_kernel, cap),
        out_shape=[jax.ShapeDtypeStruct((rows, seq), F32), jax.ShapeDtypeStruct((rows, LANES), jnp.int32)],
        compiler_params=pltpu.CompilerParams(vmem_limit_bytes=VMEM_LIMIT_BYTES),
        name="route_select",
    )(aff2d)
    row_blk = lambda r, offs: (r, 0, 0)
    grid_spec = pltpu.PrefetchScalarGridSpec(
        num_scalar_prefetch=1, grid=(rows,),
        in_specs=[pl.BlockSpec((1, 1, seq), row_blk), pl.BlockSpec((1, 1, seq), row_blk)],
        out_specs=[pl.BlockSpec((1, n_cblk, LANES), row_blk), pl.BlockSpec((1, n_cblk, LANES), row_blk)],
    )
    return pl.pallas_call(
        functools.partial(_compact_kernel, cap, n_chunks),
        grid_spec=grid_spec,
        out_shape=[jax.ShapeDtypeStruct((rows, n_cblk, LANES), jnp.int32),
                   jax.ShapeDtypeStruct((rows, n_cblk, LANES), F32)],
        compiler_params=pltpu.CompilerParams(dimension_semantics=("arbitrary",),
                                             vmem_limit_bytes=VMEM_LIMIT_BYTES),
        name="route_compact",
    )(offs.reshape(-1), key.reshape(rows, 1, seq), aff2d.reshape(rows, 1, seq))


def _gather_kernel(cap, idx_ref, hp_ref, xg_ref):
    b = pl.program_id(0)
    e = pl.program_id(1)
    base = (b * N_EXPERTS + e) * cap

    def group(g, _):
        for k in range(SUBLANES):
            xg_ref[0, 0, g, k:k + 1, :] = hp_ref[0, pl.ds(idx_ref[base + g * SUBLANES + k], 1), :]
        return 0

    lax.fori_loop(0, cap // SUBLANES, group, 0)


def _gather(idx_flat, hp, cap):
    batch, seq, half = hp.shape
    grid_spec = pltpu.PrefetchScalarGridSpec(
        num_scalar_prefetch=1,
        grid=(batch, N_EXPERTS),
        in_specs=[pl.BlockSpec((1, seq, half), lambda b, e, idx: (b, 0, 0), pipeline_mode=pl.Buffered(1))],
        out_specs=pl.BlockSpec((1, 1, cap // SUBLANES, SUBLANES, half), lambda b, e, idx: (b, e, 0, 0, 0)),
    )
    xg = pl.pallas_call(
        functools.partial(_gather_kernel, cap),
        grid_spec=grid_spec,
        out_shape=jax.ShapeDtypeStruct((batch, N_EXPERTS, cap // SUBLANES, SUBLANES, half), hp.dtype),
        compiler_params=pltpu.CompilerParams(dimension_semantics=("arbitrary", "arbitrary"),
                                             vmem_limit_bytes=VMEM_LIMIT_BYTES),
        name="gather",
    )(idx_flat, hp)
    return xg.reshape(batch, N_EXPERTS, cap, half)


def _experts_kernel(cap, tc, idx_ref, gsel_ref, xg_ref, wg_ref, wu_ref, wd_ref, o_hbm, acc_ref, y_ref, sem):
    b = pl.program_id(0)
    e = pl.program_id(1)
    base = (b * N_EXPERTS + e) * cap
    half = xg_ref.shape[3]
    n_chunks = cap // tc

    @pl.when(e == 0)
    def _():
        acc_ref[...] = jnp.zeros_like(acc_ref)

    def ffn(ci):
        words = xg_ref[0, 0, ci * tc:(ci + 1) * tc, :]
        x_lo = _unpack_bf16_pair(words, 0)
        x_hi = _unpack_bf16_pair(words, 1)
        gate = _dot(x_lo, wg_ref[0, :half, :]) + _dot(x_hi, wg_ref[0, half:, :])
        up = _dot(x_lo, wu_ref[0, :half, :]) + _dot(x_hi, wu_ref[0, half:, :])
        act = (gate * jax.nn.sigmoid(gate) * up).astype(BF16)
        y_ref[ci % 2] = _dot(act, wd_ref[0])

    def scatter(ci):
        for r0 in range(0, tc, SUBLANES):
            toks = [idx_ref[base + ci * tc + r0 + k] for k in range(SUBLANES)]
            rows = [acc_ref[pl.ds(tok, 1), :] + gsel_ref[base + ci * tc + r0 + k] * y_ref[ci % 2, r0 + k:r0 + k + 1, :]
                    for k, tok in enumerate(toks)]
            for tok, row in zip(toks, rows):
                acc_ref[pl.ds(tok, 1), :] = row

    for ci in range(n_chunks):
        ffn(ci)
        if ci > 0:
            scatter(ci - 1)
    scatter(n_chunks - 1)

    @pl.when(e == N_EXPERTS - 1)
    def _():
        copy = pltpu.make_async_copy(acc_ref, o_hbm.at[b], sem)
        copy.start()
        copy.wait()


def _experts(idx_flat, gsel_flat, xg, wg, wu, wd, seq, tc):
    batch, _, cap, half = xg.shape
    d = 2 * half
    ff = wg.shape[2]
    grid_spec = pltpu.PrefetchScalarGridSpec(
        num_scalar_prefetch=2,
        grid=(batch, N_EXPERTS),
        in_specs=[
            pl.BlockSpec((1, 1, cap, half), lambda b, e, idx, gs: (b, e, 0, 0)),
            pl.BlockSpec((1, d, ff), lambda b, e, idx, gs: (e, 0, 0)),
            pl.BlockSpec((1, d, ff), lambda b, e, idx, gs: (e, 0, 0)),
            pl.BlockSpec((1, ff, d), lambda b, e, idx, gs: (e, 0, 0)),
        ],
        out_specs=pl.BlockSpec(memory_space=pl.ANY),
        scratch_shapes=[pltpu.VMEM((seq, d), F32), pltpu.VMEM((2, tc, d), F32), pltpu.SemaphoreType.DMA],
    )
    return pl.pallas_call(
        functools.partial(_experts_kernel, cap, tc),
        grid_spec=grid_spec,
        out_shape=jax.ShapeDtypeStruct((batch, seq, d), F32),
        compiler_params=pltpu.CompilerParams(dimension_semantics=("arbitrary", "arbitrary"),
                                             vmem_limit_bytes=VMEM_LIMIT_BYTES),
        name="experts",
    )(idx_flat, gsel_flat, xg, wg, wu, wd)


def _final_kernel(apply_norm, x_ref, m_ref, g_ref, o_ref):
    x = x_ref[...] + m_ref[...]
    o_ref[...] = _rms(x, g_ref[...]) if apply_norm else x


def _final(x2d, moe2d, g, apply_norm, tm):
    t, d = x2d.shape
    tok = lambda i: (i, 0)
    return pl.pallas_call(
        functools.partial(_final_kernel, apply_norm),
        grid=(t // tm,),
        in_specs=[pl.BlockSpec((tm, d), tok), pl.BlockSpec((tm, d), tok), _const_spec(g.shape)],
        out_specs=pl.BlockSpec((tm, d), tok),
        out_shape=jax.ShapeDtypeStruct((t, d), F32),
        compiler_params=pltpu.CompilerParams(dimension_semantics=("arbitrary",),
                                             vmem_limit_bytes=VMEM_LIMIT_BYTES),
        name="final",
    )(x2d, moe2d, g)


def _rot_cols(w):
    half = w.shape[-1] // 2
    return jnp.concatenate([-w[..., half:], w[..., :half]], axis=-1)


def _tile(n, pref):
    return pref if n % pref == 0 else n


def kernel(x, mem, norm_mix_g, w_in, conv_w, conv_b, w_conv_out, q_norm_g, w_uq, kv_norm_g, w_ukv,
           w_mla_out, b_gate, w_mix_out, norm_mem_g, norm_memkv_g, w_mem_q, w_mem_kv, w_mem_out,
           norm_moe_g, w_router, w_exp_gate, w_exp_up, w_exp_down, norm_final_g):
    batch, seq, d = x.shape
    depth = w_in.shape[0]
    t = batch * seq
    q_lora = q_norm_g.shape[1]
    kv_lora = kv_norm_g.shape[1]
    cap = max(1, CAPACITY_FACTOR * seq // N_EXPERTS)
    assert cap % LANES == 0 and seq % LANES == 0 and d % (2 * LANES) == 0

    tm_in = _tile(seq, 512)
    tm_post = _tile(seq, 512)
    tq = _tile(seq, 1024)
    tk = _tile(seq, 512)
    tc = _tile(cap, 256)

    inv = 1.0 / (ROPE_THETA ** (jnp.arange(0, QK_ROPE, 2, dtype=F32) / QK_ROPE))
    ang = jnp.arange(seq, dtype=F32)[:, None] * inv[None, :]
    cos2 = jnp.concatenate([jnp.cos(ang), jnp.cos(ang)], axis=-1)
    sin2 = jnp.concatenate([jnp.sin(ang), jnp.sin(ang)], axis=-1)
    qf = (QK_DIM ** -0.5) * math.log2(math.e)
    qc = qf * jnp.concatenate([jnp.ones((seq, QK_NOPE), F32), cos2], axis=-1)
    qs = qf * jnp.concatenate([jnp.zeros((seq, QK_NOPE), F32), sin2], axis=-1)

    o_cq = 3 * d
    o_ckv = o_cq + q_lora
    o_kr = o_ckv + kv_lora
    o_gl = o_kr + QK_ROPE
    cols = {"xc": 0, "gb": d, "gc": 2 * d, "cq": o_cq, "ckv": o_ckv, "glog": o_ckv + kv_lora,
            "kr": o_ckv + kv_lora + 2 * d}

    x2d = x.reshape(t, d)
    mem2d = mem.reshape(-1, d)
    row = lambda a: a.reshape(1, -1)

    for l in range(depth):
        w_kr = w_in[l][:, o_kr:o_gl]
        win = jnp.concatenate([w_in[l][:, :o_kr], w_in[l][:, o_gl:], w_kr, _rot_cols(w_kr)], axis=1).astype(BF16)
        wq3 = w_uq[l].reshape(q_lora, MLA_HEADS, QK_DIM)
        wq3r = jnp.concatenate([jnp.zeros_like(wq3[..., :QK_NOPE]), _rot_cols(wq3[..., QK_NOPE:])], axis=-1)
        wq = wq3.transpose(1, 2, 0).astype(BF16)
        wqr = wq3r.transpose(1, 2, 0).astype(BF16)
        wkv3 = w_ukv[l].reshape(kv_lora, MLA_HEADS, QK_NOPE + V_HEAD)
        wk = wkv3[..., :QK_NOPE].transpose(1, 0, 2).astype(BF16)
        wv = wkv3[..., QK_NOPE:].transpose(1, 2, 0).astype(BF16)

        wr_hi = w_router[l].T.astype(BF16)
        wr_lo = (w_router[l].T - wr_hi.astype(F32)).astype(BF16)

        v, gb, gates, qt, k, vt = _inproj(
            x2d, batch, seq, row(norm_mix_g[l]), win, cols, row(q_norm_g[l]), row(kv_norm_g[l]),
            row(b_gate[l]), wq, wqr, wk, wv, qc.T, qs.T, cos2, sin2, tm_in)
        att = _attention(qt, k, vt, tq, tk).reshape(t, MLA_HEADS * V_HEAD)
        mk, mv = _memkv(mem2d, row(norm_memkv_g[l]), w_mem_kv[l].astype(BF16))
        x2, hp, aff = _post(
            x2d, batch, seq, v, gb, gates, att, conv_w[l], row(conv_b[l]), w_conv_out[l].astype(BF16),
            w_mla_out[l].astype(BF16), w_mix_out[l].astype(BF16), row(norm_mem_g[l]),
            w_mem_q[l].astype(BF16), mk, mv, w_mem_out[l].astype(BF16), row(norm_moe_g[l]),
            wr_hi, wr_lo, tm_post)
        idx, gsel = _route(aff.reshape(batch * N_EXPERTS, seq), cap)
        idx_flat = idx.reshape(-1)
        xg = _gather(idx_flat, hp.reshape(batch, seq, d // 2), cap)
        moe = _experts(idx_flat, gsel.reshape(-1), xg, w_exp_gate[l].astype(BF16),
                       w_exp_up[l].astype(BF16), w_exp_down[l].astype(BF16), seq, tc)
        last = l == depth - 1
        x2d = _final(x2, moe.reshape(t, d), row(norm_final_g), last, tm_in)
    return x2d.reshape(batch, seq, d)
```

```python
import functools
import math

import jax
import jax.numpy as jnp
import numpy as np
from jax import lax
from jax.experimental import pallas as pl
from jax.experimental.pallas import tpu as pltpu

MLA_HEADS = 8
QK_NOPE = 128
QK_ROPE = 64
QK_DIM = QK_NOPE + QK_ROPE
V_HEAD = 128
ROPE_THETA = 10000.0
MEM_HEADS = 4
N_EXPERTS = 16
CAPACITY_FACTOR = 2
CONV_WIDTH = 3
EPS = 1e-6

LANES = 128
SUBLANES = 8
BF16_SUBLANES = 16
VMEM_LIMIT_BYTES = 56 * 1024 * 1024
POST_SUBTILE = 256
ATTN_MAX_UNROLL = 4

F32 = jnp.float32
BF16 = jnp.bfloat16


def _const_spec(shape):
    nd = len(shape)
    return pl.BlockSpec(shape, lambda *_: (0,) * nd, pipeline_mode=pl.Buffered(1))


def _rms(x, g):
    return x * lax.rsqrt(jnp.mean(x * x, axis=-1, keepdims=True) + EPS) * g


def _dot(a, b):
    return jnp.dot(a, b, preferred_element_type=F32)


def _dot_t(a, b):
    return lax.dot_general(a, b, (((1,), (1,)), ((), ())), preferred_element_type=F32)


def _inproj_kernel(cols, q_scale, x_ref, g_ref, win_ref, wgl_ref, wkr_ref, qg_ref, kvg_ref, bg_ref,
                   wq_ref, wqr_ref, wk_ref, wv_ref, qc_ref, qs_ref, kc_ref, ks_ref,
                   v_ref, gb_ref, gates_ref, qt_ref, k_ref, vt_ref):
    d = x_ref.shape[1]
    hb = _rms(x_ref[...], g_ref[...]).astype(BF16)

    def proj(name, width):
        lo = cols[name]
        return _dot(hb, win_ref[:, lo:lo + width])

    v_ref[...] = (proj("gc", d) * proj("xc", d)).astype(BF16)
    gb_ref[...] = proj("gb", d).astype(BF16)
    gates_ref[...] = jax.nn.sigmoid(_dot(hb, wgl_ref[...]) + bg_ref[...]).astype(BF16)

    q_lora = qg_ref.shape[1]
    kv_lora = kvg_ref.shape[1]
    qn = _rms(proj("cq", q_lora), qg_ref[...]).astype(BF16)
    kvn = _rms(proj("ckv", kv_lora), kvg_ref[...]).astype(BF16)
    kr2 = _dot(hb, wkr_ref[...])
    k_rope = kr2[:, :QK_ROPE] * kc_ref[...] + kr2[:, QK_ROPE:] * ks_ref[...]
    qc = qc_ref[...]
    qs = qs_ref[...]
    for h in range(MLA_HEADS):
        qt_h = _dot_t(wq_ref[h], qn)
        rope = qt_h[QK_NOPE:] * qc + _dot_t(wqr_ref[h], qn) * qs
        qt_ref[0, h] = jnp.concatenate([qt_h[:QK_NOPE] * q_scale, rope], axis=0).astype(BF16)
        k_ref[0, h] = jnp.concatenate([_dot(kvn, wk_ref[h]), k_rope], axis=-1).astype(BF16)
        vt_ref[0, h] = _dot_t(wv_ref[h], kvn).astype(BF16)


def _inproj(x2d, batch, seq, g, win, wgl, wkr, cols, qg, kvg, bg, wq, wqr, wk, wv, qc, qs, kc, ks, q_scale, tm):
    t, d = x2d.shape
    nb = seq // tm
    tok = lambda i: (i, 0)
    pos = lambda i: (i % nb, 0)
    pos_t = lambda i: (0, i % nb)
    head_blk = lambda i: (i // nb, 0, i % nb, 0)
    head_blk_t = lambda i: (i // nb, 0, 0, i % nb)
    in_specs = [
        pl.BlockSpec((tm, d), tok),
        _const_spec(g.shape), _const_spec(win.shape), _const_spec(wgl.shape), _const_spec(wkr.shape),
        _const_spec(qg.shape), _const_spec(kvg.shape),
        _const_spec(bg.shape), _const_spec(wq.shape), _const_spec(wqr.shape), _const_spec(wk.shape),
        _const_spec(wv.shape),
        pl.BlockSpec((QK_ROPE, tm), pos_t), pl.BlockSpec((QK_ROPE, tm), pos_t),
        pl.BlockSpec((tm, QK_ROPE), pos), pl.BlockSpec((tm, QK_ROPE), pos),
    ]
    out_shape = [
        jax.ShapeDtypeStruct((t, d), BF16),
        jax.ShapeDtypeStruct((t, d), BF16),
        jax.ShapeDtypeStruct((t, 2 * d), BF16),
        jax.ShapeDtypeStruct((batch, MLA_HEADS, QK_DIM, seq), BF16),
        jax.ShapeDtypeStruct((batch, MLA_HEADS, seq, QK_DIM), BF16),
        jax.ShapeDtypeStruct((batch, MLA_HEADS, V_HEAD, seq), BF16),
    ]
    out_specs = [
        pl.BlockSpec((tm, d), tok), pl.BlockSpec((tm, d), tok), pl.BlockSpec((tm, 2 * d), tok),
        pl.BlockSpec((1, MLA_HEADS, QK_DIM, tm), head_blk_t),
        pl.BlockSpec((1, MLA_HEADS, tm, QK_DIM), head_blk),
        pl.BlockSpec((1, MLA_HEADS, V_HEAD, tm), head_blk_t),
    ]
    return pl.pallas_call(
        functools.partial(_inproj_kernel, cols, q_scale),
        grid=(t // tm,), in_specs=in_specs, out_specs=out_specs, out_shape=out_shape,
        compiler_params=pltpu.CompilerParams(dimension_semantics=("arbitrary",),
                                             vmem_limit_bytes=VMEM_LIMIT_BYTES),
        name="inproj",
    )(x2d, g, win, wgl, wkr, qg, kvg, bg, wq, wqr, wk, wv, qc, qs, kc, ks)


def _attn_kernel(tk, unroll, qt_ref, k_ref, vt_ref, o_ref, s_ref):
    qt = qt_ref[0, 0]
    tq = qt.shape[1]
    nk = k_ref.shape[2] // tk

    def scores(j, slot):
        start = pl.multiple_of(j * tk, tk)
        s = _dot(k_ref[0, 0, pl.ds(start, tk), :], qt)
        s_ref[slot] = s
        return jnp.max(s, axis=0, keepdims=True)

    def update(j, slot, mx, m, l, acc):
        start = pl.multiple_of(j * tk, tk)
        m_new = jnp.maximum(m, mx)
        p = jnp.exp2(s_ref[slot] - m_new)
        alpha = jnp.exp2(m - m_new)
        l = alpha * l + jnp.sum(p, axis=0, keepdims=True)
        acc = alpha * acc + _dot(vt_ref[0, 0, :, pl.ds(start, tk)], p.astype(BF16))
        return m_new, l, acc

    def chunks(j0, carry, last):
        mx, m, l, acc = carry
        for u in range(unroll):
            if not (last and u == unroll - 1):
                mx_next = scores(j0 + u + 1, (u + 1) % 2)
            m, l, acc = update(j0 + u, u % 2, mx, m, l, acc)
            mx = mx_next
        return mx, m, l, acc

    init = (scores(0, 0), jnp.full((1, tq), -1e30, F32), jnp.zeros((1, tq), F32), jnp.zeros((V_HEAD, tq), F32))
    carry = lax.fori_loop(0, nk // unroll - 1, lambda i, c: chunks(unroll * i, c, False), init)
    _, _, l, acc = chunks(nk - unroll, carry, True)
    o_ref[0] = (acc / l).T.astype(BF16)


def _attention(qt, k, vt, tq, tk):
    batch, heads, seq, _ = k.shape
    nk = seq // tk
    assert nk % 2 == 0
    unroll = ATTN_MAX_UNROLL if nk % ATTN_MAX_UNROLL == 0 else 2
    return pl.pallas_call(
        functools.partial(_attn_kernel, tk, unroll),
        grid=(batch, heads, seq // tq),
        in_specs=[
            pl.BlockSpec((1, 1, QK_DIM, tq), lambda b, h, i: (b, h, 0, i)),
            pl.BlockSpec((1, 1, seq, QK_DIM), lambda b, h, i: (b, h, 0, 0)),
            pl.BlockSpec((1, 1, V_HEAD, seq), lambda b, h, i: (b, h, 0, 0)),
        ],
        out_specs=pl.BlockSpec((1, tq, V_HEAD), lambda b, h, i: (b, i, h)),
        out_shape=jax.ShapeDtypeStruct((batch, seq, heads * V_HEAD), BF16),
        scratch_shapes=[pltpu.VMEM((2, tk, tq), F32)],
        compiler_params=pltpu.CompilerParams(
            dimension_semantics=("arbitrary", "arbitrary", "arbitrary"),
            vmem_limit_bytes=VMEM_LIMIT_BYTES),
        name="mla_attention",
    )(qt, k, vt)


def _memkv_kernel(m_ref, g_ref, w_ref, k_ref, v_ref):
    d = m_ref.shape[1]
    kv = _dot(_rms(m_ref[...], g_ref[...]).astype(BF16), w_ref[...])
    k_ref[...] = kv[:, :d].astype(BF16)
    v_ref[...] = kv[:, d:].astype(BF16)


def _memkv(mem2d, g, w):
    n, d = mem2d.shape
    return pl.pallas_call(
        _memkv_kernel,
        out_shape=[jax.ShapeDtypeStruct((n, d), BF16), jax.ShapeDtypeStruct((n, d), BF16)],
        compiler_params=pltpu.CompilerParams(vmem_limit_bytes=VMEM_LIMIT_BYTES),
        name="memkv",
    )(mem2d, g, w)


def _pack_bf16_pairs(lo, hi):
    return pltpu.bitcast(pltpu.pack_elementwise([lo, hi], packed_dtype=BF16), jnp.uint32)


def _unpack_bf16_pair(words, index):
    return pltpu.unpack_elementwise(words, index=index, packed_dtype=BF16, unpacked_dtype=F32).astype(BF16)


def _post_kernel(nb, sub, x_ref, v_ref, vprev_ref, vnext_ref, gb_ref, gates_ref, att_ref,
                 cw_ref, cb_ref, wco_ref, wmo_ref, wmix_ref, gmem_ref, wmq_ref, mk_ref, mv_ref, wmout_ref,
                 gmoe_ref, wrh_ref, wrl_ref,
                 x2_ref, hp_ref, aff_ref):
    i = pl.program_id(0)
    tm, d = x_ref.shape
    v = v_ref[...].astype(F32)
    first = (i % nb) == 0
    last = (i % nb) == nb - 1
    halo_prev = jnp.where(first, 0.0, vprev_ref[...].astype(F32)[BF16_SUBLANES - 1:BF16_SUBLANES, :])
    halo_next = jnp.where(last, 0.0, vnext_ref[...].astype(F32)[0:1, :])
    row = lax.broadcasted_iota(jnp.int32, (tm, d), 0)
    v_prev = jnp.where(row == 0, halo_prev, pltpu.roll(v, 1, 0))
    v_next = jnp.where(row == tm - 1, halo_next, pltpu.roll(v, tm - 1, 0))
    conv = cw_ref[0:1, :] * v_prev + cw_ref[1:2, :] * v + cw_ref[2:3, :] * v_next + cb_ref[...]
    conv_in = (gb_ref[...].astype(F32) * conv).astype(BF16)
    hd = d // MEM_HEADS
    tiles = [pl.ds(r0, sub) for r0 in range(0, tm, sub)]
    y_conv = [_dot(conv_in[r0:r0 + sub], wco_ref[...]) for r0 in range(0, tm, sub)]
    y_mla = [_dot(att_ref[rows, :], wmo_ref[...]) for rows in tiles]
    mixed = [(gates_ref[rows, :d].astype(F32) * yc + gates_ref[rows, d:].astype(F32) * ym).astype(BF16)
             for rows, yc, ym in zip(tiles, y_conv, y_mla)]
    x1 = [x_ref[rows, :] + _dot(mx, wmix_ref[...]) for rows, mx in zip(tiles, mixed)]
    qm = [(_dot(_rms(x, gmem_ref[...]).astype(BF16), wmq_ref[...]) * (hd ** -0.5)).astype(BF16) for x in x1]
    att_m = []
    for q in qm:
        outs = []
        for h in range(MEM_HEADS):
            sl = slice(h * hd, (h + 1) * hd)
            s = _dot_t(q[:, sl], mk_ref[:, sl])
            p = jnp.exp(s - jnp.max(s, axis=-1, keepdims=True))
            p = p / jnp.sum(p, axis=-1, keepdims=True)
            outs.append(_dot(p.astype(BF16), mv_ref[:, sl]).astype(BF16))
        att_m.append(jnp.concatenate(outs, axis=-1))
    x2 = [x + _dot(o, wmout_ref[...]) for x, o in zip(x1, att_m)]
    for rows, x in zip(tiles, x2):
        x2_ref[rows, :] = x
        h3 = _rms(x, gmoe_ref[...])
        hp_ref[rows, :] = _pack_bf16_pairs(h3[:, :d // 2], h3[:, d // 2:])
        h_hi = h3.astype(BF16)
        h_lo = (h3 - h_hi.astype(F32)).astype(BF16)
        logits = (_dot_t(wrh_ref[...], h_hi) + _dot_t(wrh_ref[...], h_lo)
                  + _dot_t(wrl_ref[...], h_hi) + _dot_t(wrl_ref[...], h_lo))
        e = jnp.exp(logits - jnp.max(logits, axis=0, keepdims=True))
        aff_ref[0, :, rows] = e / jnp.sum(e, axis=0, keepdims=True)


def _post(x2d, batch, seq, v, gb, gates, att, cw, cb, wco, wmo, wmix, gmem, wmq, mk, mv, wmout, gmoe, wrh, wrl, tm):
    t, d = x2d.shape
    nb = seq // tm
    hb = tm // BF16_SUBLANES
    n_halo = t // BF16_SUBLANES
    tok = lambda i: (i, 0)
    mem_len = mk.shape[0] // batch
    in_specs = [
        pl.BlockSpec((tm, d), tok), pl.BlockSpec((tm, d), tok),
        pl.BlockSpec((BF16_SUBLANES, d), lambda i: (jnp.maximum(i * hb - 1, 0), 0)),
        pl.BlockSpec((BF16_SUBLANES, d), lambda i: (jnp.minimum((i + 1) * hb, n_halo - 1), 0)),
        pl.BlockSpec((tm, d), tok), pl.BlockSpec((tm, 2 * d), tok), pl.BlockSpec((tm, d), tok),
        _const_spec(cw.shape), _const_spec(cb.shape), _const_spec(wco.shape), _const_spec(wmo.shape),
        _const_spec(wmix.shape), _const_spec(gmem.shape), _const_spec(wmq.shape),
        pl.BlockSpec((mem_len, d), lambda i: (i // nb, 0)),
        pl.BlockSpec((mem_len, d), lambda i: (i // nb, 0)),
        _const_spec(wmout.shape), _const_spec(gmoe.shape), _const_spec(wrh.shape), _const_spec(wrl.shape),
    ]
    out_shape = [
        jax.ShapeDtypeStruct((t, d), F32),
        jax.ShapeDtypeStruct((t, d // 2), jnp.uint32),
        jax.ShapeDtypeStruct((batch, N_EXPERTS, seq), F32),
    ]
    out_specs = [
        pl.BlockSpec((tm, d), tok), pl.BlockSpec((tm, d // 2), tok),
        pl.BlockSpec((1, N_EXPERTS, tm), lambda i: (i // nb, 0, i % nb)),
    ]
    return pl.pallas_call(
        functools.partial(_post_kernel, nb, min(tm, POST_SUBTILE)),
        grid=(t // tm,), in_specs=in_specs, out_specs=out_specs, out_shape=out_shape,
        compiler_params=pltpu.CompilerParams(dimension_semantics=("arbitrary",),
                                             vmem_limit_bytes=VMEM_LIMIT_BYTES),
        name="post",
    )(x2d, v, v, v, gb, gates, att, cw, cb, wco, wmo, wmix, gmem, wmq, mk, mv, wmout, gmoe, wrh, wrl)


def _select_kernel(cap, aff_ref, key_ref, offs_ref):
    rows, seq = aff_ref.shape
    n_chunks = seq // LANES
    aff = aff_ref[...]

    def count_ge(x):
        return jnp.sum((aff >= x).astype(F32), axis=-1, keepdims=True)

    def search(b, t):
        cand = t | (jnp.int32(1) << (30 - b))
        return jnp.where(count_ge(pltpu.bitcast(cand, F32)) >= cap, cand, t)

    thr = lax.fori_loop(0, 31, search, jnp.zeros((rows, 1), jnp.int32))

    def refine(_, lohi):
        lo, hi = lohi
        mid = 0.5 * (lo + hi)
        take = count_ge(mid) >= cap
        return jnp.where(take, mid, lo), jnp.where(take, hi, mid)

    lo, hi = lax.fori_loop(0, 30, refine, (pltpu.bitcast(thr, F32), pltpu.bitcast(thr + 1, F32)))
    gt = aff >= hi
    eq = (aff >= lo) & (aff < hi)
    need = cap - jnp.sum(gt.astype(F32), axis=-1, keepdims=True)

    tri = (lax.broadcasted_iota(jnp.int32, (LANES, LANES), 0)
           <= lax.broadcasted_iota(jnp.int32, (LANES, LANES), 1)).astype(BF16)
    ones = jnp.ones((LANES, LANES), BF16)
    lane = lax.broadcasted_iota(jnp.int32, (rows, LANES), 1)
    run_eq = jnp.zeros((rows, LANES), F32)
    run_sel = jnp.zeros((rows, LANES), F32)
    offs = jnp.zeros((rows, LANES), F32)
    for j in range(n_chunks):
        sl = slice(j * LANES, (j + 1) * LANES)
        eq_j = eq[:, sl]
        eq_b = eq_j.astype(F32).astype(BF16)
        eq_rank = _dot(eq_b, tri) + run_eq
        run_eq = run_eq + _dot(eq_b, ones)
        sel_j = gt[:, sl] | (eq_j & (eq_rank <= need))
        sel_b = sel_j.astype(F32).astype(BF16)
        pos = _dot(sel_b, tri) + run_sel
        run_sel = run_sel + _dot(sel_b, ones)
        key_ref[:, sl] = jnp.where(sel_j, pos, 0.0)
        offs = jnp.where(lane == j, run_sel, offs)
    offs_ref[...] = offs.astype(jnp.int32)


def _compact_kernel(cap, n_chunks, offs_ref, key_ref, aff_ref, idx_ref, gsel_ref):
    r = pl.program_id(0)
    n_cblk = cap // LANES
    slot = lax.broadcasted_iota(jnp.int32, (LANES, LANES), 0) + 1
    lane = lax.broadcasted_iota(jnp.int32, (LANES, LANES), 1)
    obase = r * LANES
    j0 = jnp.int32(0)
    for cb in range(n_cblk):
        want = (slot + cb * LANES).astype(F32)
        j0 = lax.while_loop(lambda j: (j < n_chunks - 1) & (offs_ref[obase + j] <= cb * LANES),
                            lambda j: j + 1, j0)
        j1 = lax.while_loop(lambda j: (j < n_chunks - 1) & (offs_ref[obase + j] < (cb + 1) * LANES),
                            lambda j: j + 1, j0)

        def per_chunk(j, carry):
            tok_sel, g_sel = carry
            start = pl.multiple_of(j * LANES, LANES)
            hit = key_ref[0, :, pl.ds(start, LANES)] == want
            tok_sel = jnp.where(hit, (lane + j * LANES).astype(F32), tok_sel)
            g_sel = jnp.where(hit, aff_ref[0, :, pl.ds(start, LANES)], g_sel)
            return tok_sel, g_sel

        zero = jnp.zeros((LANES, LANES), F32)
        tok_sel, g_sel = lax.fori_loop(j0, j1 + 1, per_chunk, (zero, zero))
        idx_ref[0, cb:cb + 1, :] = jnp.sum(tok_sel.T, axis=0, keepdims=True).astype(jnp.int32)
        gsel_ref[0, cb:cb + 1, :] = jnp.sum(g_sel.T, axis=0, keepdims=True)
        j0 = j1


def _route(aff2d, cap):
    rows, seq = aff2d.shape
    n_chunks = seq // LANES
    n_cblk = cap // LANES
    assert n_chunks <= LANES
    key, offs = pl.pallas_call(
        functools.partial(_select_kernel, cap),
        out_shape=[jax.ShapeDtypeStruct((rows, seq), F32), jax.ShapeDtypeStruct((rows, LANES), jnp.int32)],
        compiler_params=pltpu.CompilerParams(vmem_limit_bytes=VMEM_LIMIT_BYTES),
        name="route_select",
    )(aff2d)
    row_blk = lambda r, offs: (r, 0, 0)
    grid_spec = pltpu.PrefetchScalarGridSpec(
        num_scalar_prefetch=1, grid=(rows,),
        in_specs=[pl.BlockSpec((1, 1, seq), row_blk), pl.BlockSpec((1, 1, seq), row_blk)],
        out_specs=[pl.BlockSpec((1, n_cblk, LANES), row_blk), pl.BlockSpec((1, n_cblk, LANES), row_blk)],
    )
    return pl.pallas_call(
        functools.partial(_compact_kernel, cap, n_chunks),
        grid_spec=grid_spec,
        out_shape=[jax.ShapeDtypeStruct((rows, n_cblk, LANES), jnp.int32),
                   jax.ShapeDtypeStruct((rows, n_cblk, LANES), F32)],
        compiler_params=pltpu.CompilerParams(dimension_semantics=("arbitrary",),
                                             vmem_limit_bytes=VMEM_LIMIT_BYTES),
        name="route_compact",
    )(offs.reshape(-1), key.reshape(rows, 1, seq), aff2d.reshape(rows, 1, seq))


def _gather_kernel(cap, idx_ref, hp_ref, xg_ref):
    b = pl.program_id(0)
    e = pl.program_id(1)
    base = (b * N_EXPERTS + e) * cap

    def group(g, _):
        for k in range(SUBLANES):
            xg_ref[0, 0, g, k:k + 1, :] = hp_ref[0, pl.ds(idx_ref[base + g * SUBLANES + k], 1), :]
        return 0

    lax.fori_loop(0, cap // SUBLANES, group, 0)


def _gather(idx_flat, hp, cap):
    batch, seq, half = hp.shape
    grid_spec = pltpu.PrefetchScalarGridSpec(
        num_scalar_prefetch=1,
        grid=(batch, N_EXPERTS),
        in_specs=[pl.BlockSpec((1, seq, half), lambda b, e, idx: (b, 0, 0), pipeline_mode=pl.Buffered(1))],
        out_specs=pl.BlockSpec((1, 1, cap // SUBLANES, SUBLANES, half), lambda b, e, idx: (b, e, 0, 0, 0)),
    )
    xg = pl.pallas_call(
        functools.partial(_gather_kernel, cap),
        grid_spec=grid_spec,
        out_shape=jax.ShapeDtypeStruct((batch, N_EXPERTS, cap // SUBLANES, SUBLANES, half), hp.dtype),
        compiler_params=pltpu.CompilerParams(dimension_semantics=("arbitrary", "arbitrary"),
                                             vmem_limit_bytes=VMEM_LIMIT_BYTES),
        name="gather",
    )(idx_flat, hp)
    return xg.reshape(batch, N_EXPERTS, cap, half)


def _experts_kernel(cap, tc, idx_ref, gsel_ref, xg_ref, wg_ref, wu_ref, wd_ref, o_hbm, acc_ref, y_ref, sem):
    b = pl.program_id(0)
    e = pl.program_id(1)
    base = (b * N_EXPERTS + e) * cap
    half = xg_ref.shape[3]
    n_chunks = cap // tc

    @pl.when(e == 0)
    def _():
        acc_ref[...] = jnp.zeros_like(acc_ref)

    def ffn(ci):
        words = xg_ref[0, 0, ci * tc:(ci + 1) * tc, :]
        x_lo = _unpack_bf16_pair(words, 0)
        x_hi = _unpack_bf16_pair(words, 1)
        gate = _dot(x_lo, wg_ref[0, :half, :]) + _dot(x_hi, wg_ref[0, half:, :])
        up = _dot(x_lo, wu_ref[0, :half, :]) + _dot(x_hi, wu_ref[0, half:, :])
        act = (gate * jax.nn.sigmoid(gate) * up).astype(BF16)
        y_ref[ci % 2] = _dot(act, wd_ref[0])

    def scatter(ci):
        for r0 in range(0, tc, SUBLANES):
            toks = [idx_ref[base + ci * tc + r0 + k] for k in range(SUBLANES)]
            rows = [acc_ref[pl.ds(tok, 1), :] + gsel_ref[base + ci * tc + r0 + k] * y_ref[ci % 2, r0 + k:r0 + k + 1, :]
                    for k, tok in enumerate(toks)]
            for tok, row in zip(toks, rows):
                acc_ref[pl.ds(tok, 1), :] = row

    for ci in range(n_chunks):
        ffn(ci)
        if ci > 0:
            scatter(ci - 1)
    scatter(n_chunks - 1)

    @pl.when(e == N_EXPERTS - 1)
    def _():
        copy = pltpu.make_async_copy(acc_ref, o_hbm.at[b], sem)
        copy.start()
        copy.wait()


def _experts(idx_flat, gsel_flat, xg, wg, wu, wd, seq, tc):
    batch, _, cap, half = xg.shape
    d = 2 * half
    ff = wg.shape[2]
    grid_spec = pltpu.PrefetchScalarGridSpec(
        num_scalar_prefetch=2,
        grid=(batch, N_EXPERTS),
        in_specs=[
            pl.BlockSpec((1, 1, cap, half), lambda b, e, idx, gs: (b, e, 0, 0)),
            pl.BlockSpec((1, d, ff), lambda b, e, idx, gs: (e, 0, 0)),
            pl.BlockSpec((1, d, ff), lambda b, e, idx, gs: (e, 0, 0)),
            pl.BlockSpec((1, ff, d), lambda b, e, idx, gs: (e, 0, 0)),
        ],
        out_specs=pl.BlockSpec(memory_space=pl.ANY),
        scratch_shapes=[pltpu.VMEM((seq, d), F32), pltpu.VMEM((2, tc, d), F32), pltpu.SemaphoreType.DMA],
    )
    return pl.pallas_call(
        functools.partial(_experts_kernel, cap, tc),
        grid_spec=grid_spec,
        out_shape=jax.ShapeDtypeStruct((batch, seq, d), F32),
        compiler_params=pltpu.CompilerParams(dimension_semantics=("arbitrary", "arbitrary"),
                                             vmem_limit_bytes=VMEM_LIMIT_BYTES),
        name="experts",
    )(idx_flat, gsel_flat, xg, wg, wu, wd)


def _final_kernel(apply_norm, x_ref, m_ref, g_ref, o_ref):
    x = x_ref[...] + m_ref[...]
    o_ref[...] = _rms(x, g_ref[...]) if apply_norm else x


def _final(x2d, moe2d, g, apply_norm, tm):
    t, d = x2d.shape
    tok = lambda i: (i, 0)
    return pl.pallas_call(
        functools.partial(_final_kernel, apply_norm),
        grid=(t // tm,),
        in_specs=[pl.BlockSpec((tm, d), tok), pl.BlockSpec((tm, d), tok), _const_spec(g.shape)],
        out_specs=pl.BlockSpec((tm, d), tok),
        out_shape=jax.ShapeDtypeStruct((t, d), F32),
        compiler_params=pltpu.CompilerParams(dimension_semantics=("arbitrary",),
                                             vmem_limit_bytes=VMEM_LIMIT_BYTES),
        name="final",
    )(x2d, moe2d, g)


def _rot_cols(w):
    half = w.shape[-1] // 2
    return jnp.concatenate([-w[..., half:], w[..., :half]], axis=-1)


def _tile(n, pref):
    return pref if n % pref == 0 else n


def kernel(x, mem, norm_mix_g, w_in, conv_w, conv_b, w_conv_out, q_norm_g, w_uq, kv_norm_g, w_ukv,
           w_mla_out, b_gate, w_mix_out, norm_mem_g, norm_memkv_g, w_mem_q, w_mem_kv, w_mem_out,
           norm_moe_g, w_router, w_exp_gate, w_exp_up, w_exp_down, norm_final_g):
    batch, seq, d = x.shape
    depth = w_in.shape[0]
    t = batch * seq
    q_lora = q_norm_g.shape[1]
    kv_lora = kv_norm_g.shape[1]
    cap = max(1, CAPACITY_FACTOR * seq // N_EXPERTS)
    assert cap % LANES == 0 and seq % LANES == 0 and d % (2 * LANES) == 0

    tm_in = _tile(seq, 512)
    tm_post = _tile(seq, 512)
    tq = _tile(seq, 1024)
    tk = _tile(seq, 512)
    tc = _tile(cap, 256)

    inv = 1.0 / (ROPE_THETA ** (np.arange(0, QK_ROPE, 2, dtype=np.float64) / QK_ROPE))
    ang = np.arange(seq, dtype=np.float64)[:, None] * inv[None, :]
    cos2 = np.concatenate([np.cos(ang), np.cos(ang)], axis=-1)
    sin2 = np.concatenate([np.sin(ang), np.sin(ang)], axis=-1)
    qf = (QK_DIM ** -0.5) * math.log2(math.e)
    kc, ks = jnp.asarray(cos2, F32), jnp.asarray(sin2, F32)
    qc, qs = jnp.asarray(qf * cos2.T, F32), jnp.asarray(qf * sin2.T, F32)

    o_cq = 3 * d
    o_ckv = o_cq + q_lora
    o_kr = o_ckv + kv_lora
    o_gl = o_kr + QK_ROPE
    cols = {"xc": 0, "gb": d, "gc": 2 * d, "cq": o_cq, "ckv": o_ckv}

    x2d = x.reshape(t, d)
    mem2d = mem.reshape(-1, d)
    row = lambda a: a.reshape(1, -1)

    for l in range(depth):
        win = w_in[l].astype(BF16)
        wgl = win[:, o_gl:]
        wkr = jnp.concatenate([win[:, o_kr:o_gl], _rot_cols(win[:, o_kr:o_gl])], axis=1)
        wq3 = w_uq[l].reshape(q_lora, MLA_HEADS, QK_DIM)
        wq = wq3.transpose(1, 2, 0).astype(BF16)
        wqr = _rot_cols(wq3[..., QK_NOPE:]).transpose(1, 2, 0).astype(BF16)
        wkv3 = w_ukv[l].reshape(kv_lora, MLA_HEADS, QK_NOPE + V_HEAD)
        wk = wkv3[..., :QK_NOPE].transpose(1, 0, 2).astype(BF16)
        wv = wkv3[..., QK_NOPE:].transpose(1, 2, 0).astype(BF16)

        wr_hi = w_router[l].T.astype(BF16)
        wr_lo = (w_router[l].T - wr_hi.astype(F32)).astype(BF16)

        v, gb, gates, qt, k, vt = _inproj(
            x2d, batch, seq, row(norm_mix_g[l]), win, wgl, wkr, cols, row(q_norm_g[l]), row(kv_norm_g[l]),
            row(b_gate[l]), wq, wqr, wk, wv, qc, qs, kc, ks, qf, tm_in)
        att = _attention(qt, k, vt, tq, tk).reshape(t, MLA_HEADS * V_HEAD)
        mk, mv = _memkv(mem2d, row(norm_memkv_g[l]), w_mem_kv[l].astype(BF16))
        x2, hp, aff = _post(
            x2d, batch, seq, v, gb, gates, att, conv_w[l], row(conv_b[l]), w_conv_out[l].astype(BF16),
            w_mla_out[l].astype(BF16), w_mix_out[l].astype(BF16), row(norm_mem_g[l]),
            w_mem_q[l].astype(BF16), mk, mv, w_mem_out[l].astype(BF16), row(norm_moe_g[l]),
            wr_hi, wr_lo, tm_post)
        idx, gsel = _route(aff.reshape(batch * N_EXPERTS, seq), cap)
        idx_flat = idx.reshape(-1)
        xg = _gather(idx_flat, hp.reshape(batch, seq, d // 2), cap)
        moe = _experts(idx_flat, gsel.reshape(-1), xg, w_exp_gate[l].astype(BF16),
                       w_exp_up[l].astype(BF16), w_exp_down[l].astype(BF16), seq, tc)
        last = l == depth - 1
        x2d = _final(x2, moe.reshape(t, d), row(norm_final_g), last, tm_in)
    return x2d.reshape(batch, seq, d)
```

```python
import functools
import math

import jax
import jax.numpy as jnp
import numpy as np
from jax import lax
from jax.experimental import pallas as pl
from jax.experimental.pallas import tpu as pltpu

MLA_HEADS = 8
QK_NOPE = 128
QK_ROPE = 64
QK_DIM = QK_NOPE + QK_ROPE
V_HEAD = 128
V_AUG = V_HEAD + 16
ROPE_THETA = 10000.0
MEM_HEADS = 4
N_EXPERTS = 16
CAPACITY_FACTOR = 2
CONV_WIDTH = 3
EPS = 1e-6

LANES = 128
SUBLANES = 8
BF16_SUBLANES = 16
VMEM_LIMIT_BYTES = 56 * 1024 * 1024
EXPERTS_VMEM_LIMIT_BYTES = 60 * 1024 * 1024
OUT_TILE_ROWS = 256
POST_SUBTILE = 256
ATTN_MAX_UNROLL = 4

F32 = jnp.float32
BF16 = jnp.bfloat16


def _const_spec(shape):
    nd = len(shape)
    return pl.BlockSpec(shape, lambda *_: (0,) * nd, pipeline_mode=pl.Buffered(1))


def _rms(x, g):
    return x * lax.rsqrt(jnp.mean(x * x, axis=-1, keepdims=True) + EPS) * g


def _dot(a, b):
    return jnp.dot(a, b, preferred_element_type=F32)


def _dot_t(a, b):
    return lax.dot_general(a, b, (((1,), (1,)), ((), ())), preferred_element_type=F32)


def _inproj_kernel(cols, q_scale, x_ref, g_ref, win_ref, wgl_ref, wkr_ref, qg_ref, kvg_ref, bg_ref,
                   wq_ref, wqr_ref, wk_ref, wv_ref, qc_ref, qs_ref, kc_ref, ks_ref,
                   v_ref, gb_ref, gates_ref, qt_ref, k_ref, vt_ref):
    d = x_ref.shape[1]
    hb = _rms(x_ref[...], g_ref[...]).astype(BF16)

    def proj(name, width):
        lo = cols[name]
        return _dot(hb, win_ref[:, lo:lo + width])

    v_ref[...] = (proj("gc", d) * proj("xc", d)).astype(BF16)
    gb_ref[...] = proj("gb", d).astype(BF16)
    gates_ref[...] = jax.nn.sigmoid(_dot(hb, wgl_ref[...]) + bg_ref[...]).astype(BF16)

    q_lora = qg_ref.shape[1]
    kv_lora = kvg_ref.shape[1]
    qn = _rms(proj("cq", q_lora), qg_ref[...]).astype(BF16)
    kvn = _rms(proj("ckv", kv_lora), kvg_ref[...]).astype(BF16)
    kr2 = _dot(hb, wkr_ref[...])
    k_rope = kr2[:, :QK_ROPE] * kc_ref[...] + kr2[:, QK_ROPE:] * ks_ref[...]
    qc = qc_ref[...]
    qs = qs_ref[...]
    pad_row = lax.broadcasted_iota(jnp.int32, (V_AUG - V_HEAD, x_ref.shape[0]), 0)
    ones_rows = jnp.where(pad_row == 0, 1.0, 0.0).astype(BF16)
    for h in range(MLA_HEADS):
        qt_h = _dot_t(wq_ref[h], qn)
        rope = qt_h[QK_NOPE:] * qc + _dot_t(wqr_ref[h], qn) * qs
        qt_ref[0, h] = jnp.concatenate([qt_h[:QK_NOPE] * q_scale, rope], axis=0).astype(BF16)
        k_ref[0, h] = jnp.concatenate([_dot(kvn, wk_ref[h]), k_rope], axis=-1).astype(BF16)
        vt_ref[0, h, :V_HEAD, :] = _dot_t(wv_ref[h], kvn).astype(BF16)
        vt_ref[0, h, V_HEAD:, :] = ones_rows


def _inproj(x2d, batch, seq, g, win, wgl, wkr, cols, qg, kvg, bg, wq, wqr, wk, wv, qc, qs, kc, ks, q_scale, tm):
    t, d = x2d.shape
    nb = seq // tm
    tok = lambda i: (i, 0)
    pos = lambda i: (i % nb, 0)
    pos_t = lambda i: (0, i % nb)
    head_blk = lambda i: (i // nb, 0, i % nb, 0)
    head_blk_t = lambda i: (i // nb, 0, 0, i % nb)
    in_specs = [
        pl.BlockSpec((tm, d), tok),
        _const_spec(g.shape), _const_spec(win.shape), _const_spec(wgl.shape), _const_spec(wkr.shape),
        _const_spec(qg.shape), _const_spec(kvg.shape),
        _const_spec(bg.shape), _const_spec(wq.shape), _const_spec(wqr.shape), _const_spec(wk.shape),
        _const_spec(wv.shape),
        pl.BlockSpec((QK_ROPE, tm), pos_t), pl.BlockSpec((QK_ROPE, tm), pos_t),
        pl.BlockSpec((tm, QK_ROPE), pos), pl.BlockSpec((tm, QK_ROPE), pos),
    ]
    out_shape = [
        jax.ShapeDtypeStruct((t, d), BF16),
        jax.ShapeDtypeStruct((t, d), BF16),
        jax.ShapeDtypeStruct((t, 2 * d), BF16),
        jax.ShapeDtypeStruct((batch, MLA_HEADS, QK_DIM, seq), BF16),
        jax.ShapeDtypeStruct((batch, MLA_HEADS, seq, QK_DIM), BF16),
        jax.ShapeDtypeStruct((batch, MLA_HEADS, V_AUG, seq), BF16),
    ]
    out_specs = [
        pl.BlockSpec((tm, d), tok), pl.BlockSpec((tm, d), tok), pl.BlockSpec((tm, 2 * d), tok),
        pl.BlockSpec((1, MLA_HEADS, QK_DIM, tm), head_blk_t),
        pl.BlockSpec((1, MLA_HEADS, tm, QK_DIM), head_blk),
        pl.BlockSpec((1, MLA_HEADS, V_AUG, tm), head_blk_t),
    ]
    return pl.pallas_call(
        functools.partial(_inproj_kernel, cols, q_scale),
        grid=(t // tm,), in_specs=in_specs, out_specs=out_specs, out_shape=out_shape,
        compiler_params=pltpu.CompilerParams(dimension_semantics=("arbitrary",),
                                             vmem_limit_bytes=VMEM_LIMIT_BYTES),
        name="inproj",
    )(x2d, g, win, wgl, wkr, qg, kvg, bg, wq, wqr, wk, wv, qc, qs, kc, ks)


def _attn_kernel(tk, unroll, qt_ref, k_ref, vt_ref, o_ref, s_ref):
    qt = qt_ref[0, 0]
    tq = qt.shape[1]
    nk = k_ref.shape[2] // tk

    def scores(j, slot):
        start = pl.multiple_of(j * tk, tk)
        s = _dot(k_ref[0, 0, pl.ds(start, tk), :], qt)
        s_ref[slot] = s
        return jnp.max(s, axis=0, keepdims=True)

    def update(j, slot, mx, m, acc):
        start = pl.multiple_of(j * tk, tk)
        m_new = jnp.maximum(m, mx)
        p = jnp.exp2(s_ref[slot] - m_new)
        acc = jnp.exp2(m - m_new) * acc + _dot(vt_ref[0, 0, :, pl.ds(start, tk)], p.astype(BF16))
        return m_new, acc

    def chunks(j0, carry, last):
        mx, m, acc = carry
        for u in range(unroll):
            if not (last and u == unroll - 1):
                mx_next = scores(j0 + u + 1, (u + 1) % 2)
            m, acc = update(j0 + u, u % 2, mx, m, acc)
            mx = mx_next
        return mx, m, acc

    init = (scores(0, 0), jnp.full((1, tq), -1e30, F32), jnp.zeros((vt_ref.shape[2], tq), F32))
    carry = lax.fori_loop(0, nk // unroll - 1, lambda i, c: chunks(unroll * i, c, False), init)
    _, _, acc = chunks(nk - unroll, carry, True)
    o_ref[0] = (acc[:V_HEAD] / acc[V_HEAD:V_HEAD + 1]).T.astype(BF16)


def _attention(qt, k, vt, tq, tk):
    batch, heads, seq, _ = k.shape
    nk = seq // tk
    assert nk % 2 == 0
    unroll = ATTN_MAX_UNROLL if nk % ATTN_MAX_UNROLL == 0 else 2
    return pl.pallas_call(
        functools.partial(_attn_kernel, tk, unroll),
        grid=(batch, heads, seq // tq),
        in_specs=[
            pl.BlockSpec((1, 1, QK_DIM, tq), lambda b, h, i: (b, h, 0, i)),
            pl.BlockSpec((1, 1, seq, QK_DIM), lambda b, h, i: (b, h, 0, 0)),
            pl.BlockSpec((1, 1, V_AUG, seq), lambda b, h, i: (b, h, 0, 0)),
        ],
        out_specs=pl.BlockSpec((1, tq, V_HEAD), lambda b, h, i: (b, i, h)),
        out_shape=jax.ShapeDtypeStruct((batch, seq, heads * V_HEAD), BF16),
        scratch_shapes=[pltpu.VMEM((2, tk, tq), F32)],
        compiler_params=pltpu.CompilerParams(
            dimension_semantics=("arbitrary", "arbitrary", "arbitrary"),
            vmem_limit_bytes=VMEM_LIMIT_BYTES),
        name="mla_attention",
    )(qt, k, vt)


def _memkv_kernel(m_ref, g_ref, w_ref, k_ref, v_ref):
    d = m_ref.shape[1]
    kv = _dot(_rms(m_ref[...], g_ref[...]).astype(BF16), w_ref[...])
    k_ref[...] = kv[:, :d].astype(BF16)
    v_ref[...] = kv[:, d:].astype(BF16)


def _memkv(mem2d, g, w):
    n, d = mem2d.shape
    return pl.pallas_call(
        _memkv_kernel,
        out_shape=[jax.ShapeDtypeStruct((n, d), BF16), jax.ShapeDtypeStruct((n, d), BF16)],
        compiler_params=pltpu.CompilerParams(vmem_limit_bytes=VMEM_LIMIT_BYTES),
        name="memkv",
    )(mem2d, g, w)


def _pack_bf16_pairs(lo, hi):
    return pltpu.bitcast(pltpu.pack_elementwise([lo, hi], packed_dtype=BF16), jnp.uint32)


def _unpack_bf16_pair(words, index):
    return pltpu.unpack_elementwise(words, index=index, packed_dtype=BF16, unpacked_dtype=F32).astype(BF16)


def _post_kernel(nb, sub, x_ref, v_ref, vprev_ref, vnext_ref, gb_ref, gates_ref, att_ref,
                 cw_ref, cb_ref, wco_ref, wmo_ref, wmix_ref, gmem_ref, wmq_ref, mk_ref, mv_ref, wmout_ref,
                 gmoe_ref, wrh_ref, wrl_ref,
                 x2_ref, hp_ref, aff_ref):
    i = pl.program_id(0)
    tm, d = x_ref.shape
    v = v_ref[...].astype(F32)
    first = (i % nb) == 0
    last = (i % nb) == nb - 1
    halo_prev = jnp.where(first, 0.0, vprev_ref[...].astype(F32)[BF16_SUBLANES - 1:BF16_SUBLANES, :])
    halo_next = jnp.where(last, 0.0, vnext_ref[...].astype(F32)[0:1, :])
    row = lax.broadcasted_iota(jnp.int32, (tm, d), 0)
    v_prev = jnp.where(row == 0, halo_prev, pltpu.roll(v, 1, 0))
    v_next = jnp.where(row == tm - 1, halo_next, pltpu.roll(v, tm - 1, 0))
    conv = cw_ref[0:1, :] * v_prev + cw_ref[1:2, :] * v + cw_ref[2:3, :] * v_next + cb_ref[...]
    conv_in = (gb_ref[...].astype(F32) * conv).astype(BF16)
    hd = d // MEM_HEADS
    tiles = [pl.ds(r0, sub) for r0 in range(0, tm, sub)]
    y_conv = [_dot(conv_in[r0:r0 + sub], wco_ref[...]) for r0 in range(0, tm, sub)]
    y_mla = [_dot(att_ref[rows, :], wmo_ref[...]) for rows in tiles]
    mixed = [(gates_ref[rows, :d].astype(F32) * yc + gates_ref[rows, d:].astype(F32) * ym).astype(BF16)
             for rows, yc, ym in zip(tiles, y_conv, y_mla)]
    x1 = [x_ref[rows, :] + _dot(mx, wmix_ref[...]) for rows, mx in zip(tiles, mixed)]
    qm = [(_dot(_rms(x, gmem_ref[...]).astype(BF16), wmq_ref[...]) * (hd ** -0.5)).astype(BF16) for x in x1]
    att_m = []
    for q in qm:
        outs = []
        for h in range(MEM_HEADS):
            sl = slice(h * hd, (h + 1) * hd)
            s = _dot_t(q[:, sl], mk_ref[:, sl])
            p = jnp.exp(s - jnp.max(s, axis=-1, keepdims=True))
            p = p / jnp.sum(p, axis=-1, keepdims=True)
            outs.append(_dot(p.astype(BF16), mv_ref[:, sl]).astype(BF16))
        att_m.append(jnp.concatenate(outs, axis=-1))
    x2 = [x + _dot(o, wmout_ref[...]) for x, o in zip(x1, att_m)]
    for rows, x in zip(tiles, x2):
        x2_ref[rows, :] = x
        h3 = _rms(x, gmoe_ref[...])
        hp_ref[rows, :] = _pack_bf16_pairs(h3[:, :d // 2], h3[:, d // 2:])
        h_hi = h3.astype(BF16)
        h_lo = (h3 - h_hi.astype(F32)).astype(BF16)
        logits = (_dot_t(wrh_ref[...], h_hi) + _dot_t(wrh_ref[...], h_lo)
                  + _dot_t(wrl_ref[...], h_hi) + _dot_t(wrl_ref[...], h_lo))
        e = jnp.exp(logits - jnp.max(logits, axis=0, keepdims=True))
        aff_ref[0, :, rows] = e / jnp.sum(e, axis=0, keepdims=True)


def _post(x2d, batch, seq, v, gb, gates, att, cw, cb, wco, wmo, wmix, gmem, wmq, mk, mv, wmout, gmoe, wrh, wrl, tm):
    t, d = x2d.shape
    nb = seq // tm
    hb = tm // BF16_SUBLANES
    n_halo = t // BF16_SUBLANES
    tok = lambda i: (i, 0)
    mem_len = mk.shape[0] // batch
    in_specs = [
        pl.BlockSpec((tm, d), tok), pl.BlockSpec((tm, d), tok),
        pl.BlockSpec((BF16_SUBLANES, d), lambda i: (jnp.maximum(i * hb - 1, 0), 0)),
        pl.BlockSpec((BF16_SUBLANES, d), lambda i: (jnp.minimum((i + 1) * hb, n_halo - 1), 0)),
        pl.BlockSpec((tm, d), tok), pl.BlockSpec((tm, 2 * d), tok), pl.BlockSpec((tm, d), tok),
        _const_spec(cw.shape), _const_spec(cb.shape), _const_spec(wco.shape), _const_spec(wmo.shape),
        _const_spec(wmix.shape), _const_spec(gmem.shape), _const_spec(wmq.shape),
        pl.BlockSpec((mem_len, d), lambda i: (i // nb, 0)),
        pl.BlockSpec((mem_len, d), lambda i: (i // nb, 0)),
        _const_spec(wmout.shape), _const_spec(gmoe.shape), _const_spec(wrh.shape), _const_spec(wrl.shape),
    ]
    out_shape = [
        jax.ShapeDtypeStruct((t, d), F32),
        jax.ShapeDtypeStruct((t, d // 2), jnp.uint32),
        jax.ShapeDtypeStruct((batch, N_EXPERTS, seq), F32),
    ]
    out_specs = [
        pl.BlockSpec((tm, d), tok), pl.BlockSpec((tm, d // 2), tok),
        pl.BlockSpec((1, N_EXPERTS, tm), lambda i: (i // nb, 0, i % nb)),
    ]
    return pl.pallas_call(
        functools.partial(_post_kernel, nb, min(tm, POST_SUBTILE)),
        grid=(t // tm,), in_specs=in_specs, out_specs=out_specs, out_shape=out_shape,
        compiler_params=pltpu.CompilerParams(dimension_semantics=("arbitrary",),
                                             vmem_limit_bytes=VMEM_LIMIT_BYTES),
        name="post",
    )(x2d, v, v, v, gb, gates, att, cw, cb, wco, wmo, wmix, gmem, wmq, mk, mv, wmout, gmoe, wrh, wrl)


def _select_kernel(cap, aff_ref, key_ref, offs_ref):
    rows, seq = aff_ref.shape
    n_chunks = seq // LANES
    aff = aff_ref[...]

    def count_ge(x):
        return jnp.sum((aff >= x).astype(F32), axis=-1, keepdims=True)

    def search(b, t):
        cand = t | (jnp.int32(1) << (30 - b))
        return jnp.where(count_ge(pltpu.bitcast(cand, F32)) >= cap, cand, t)

    thr = lax.fori_loop(0, 31, search, jnp.zeros((rows, 1), jnp.int32))

    def refine(_, lohi):
        lo, hi = lohi
        mid = 0.5 * (lo + hi)
        take = count_ge(mid) >= cap
        return jnp.where(take, mid, lo), jnp.where(take, hi, mid)

    lo, hi = lax.fori_loop(0, 30, refine, (pltpu.bitcast(thr, F32), pltpu.bitcast(thr + 1, F32)))
    gt = aff >= hi
    eq = (aff >= lo) & (aff < hi)
    need = cap - jnp.sum(gt.astype(F32), axis=-1, keepdims=True)

    tri = (lax.broadcasted_iota(jnp.int32, (LANES, LANES), 0)
           <= lax.broadcasted_iota(jnp.int32, (LANES, LANES), 1)).astype(BF16)
    ones = jnp.ones((LANES, LANES), BF16)
    lane = lax.broadcasted_iota(jnp.int32, (rows, LANES), 1)
    run_eq = jnp.zeros((rows, LANES), F32)
    run_sel = jnp.zeros((rows, LANES), F32)
    offs = jnp.zeros((rows, LANES), F32)
    for j in range(n_chunks):
        sl = slice(j * LANES, (j + 1) * LANES)
        eq_j = eq[:, sl]
        eq_b = eq_j.astype(F32).astype(BF16)
        eq_rank = _dot(eq_b, tri) + run_eq
        run_eq = run_eq + _dot(eq_b, ones)
        sel_j = gt[:, sl] | (eq_j & (eq_rank <= need))
        sel_b = sel_j.astype(F32).astype(BF16)
        pos = _dot(sel_b, tri) + run_sel
        run_sel = run_sel + _dot(sel_b, ones)
        key_ref[:, sl] = jnp.where(sel_j, pos, 0.0)
        offs = jnp.where(lane == j, run_sel, offs)
    offs_ref[...] = offs.astype(jnp.int32)


def _compact_kernel(cap, n_chunks, offs_ref, key_ref, aff_ref, idx_ref, gsel_ref):
    r = pl.program_id(0)
    n_cblk = cap // LANES
    slot = lax.broadcasted_iota(jnp.int32, (LANES, LANES), 0) + 1
    lane = lax.broadcasted_iota(jnp.int32, (LANES, LANES), 1)
    obase = r * LANES
    j0 = jnp.int32(0)
    for cb in range(n_cblk):
        want = (slot + cb * LANES).astype(F32)
        j0 = lax.while_loop(lambda j: (j < n_chunks - 1) & (offs_ref[obase + j] <= cb * LANES),
                            lambda j: j + 1, j0)
        j1 = lax.while_loop(lambda j: (j < n_chunks - 1) & (offs_ref[obase + j] < (cb + 1) * LANES),
                            lambda j: j + 1, j0)

        def per_chunk(j, carry):
            tok_sel, g_sel = carry
            start = pl.multiple_of(j * LANES, LANES)
            hit = key_ref[0, :, pl.ds(start, LANES)] == want
            tok_sel = jnp.where(hit, (lane + j * LANES).astype(F32), tok_sel)
            g_sel = jnp.where(hit, aff_ref[0, :, pl.ds(start, LANES)], g_sel)
            return tok_sel, g_sel

        zero = jnp.zeros((LANES, LANES), F32)
        tok_sel, g_sel = lax.fori_loop(j0, j1 + 1, per_chunk, (zero, zero))
        idx_ref[0, cb:cb + 1, :] = jnp.sum(tok_sel.T, axis=0, keepdims=True).astype(jnp.int32)
        gsel_ref[0, cb:cb + 1, :] = jnp.sum(g_sel.T, axis=0, keepdims=True)
        j0 = j1


def _route(aff2d, cap):
    rows, seq = aff2d.shape
    n_chunks = seq // LANES
    n_cblk = cap // LANES
    assert n_chunks <= LANES
    key, offs = pl.pallas_call(
        functools.partial(_select_kernel, cap),
        out_shape=[jax.ShapeDtypeStruct((rows, seq), F32), jax.ShapeDtypeStruct((rows, LANES), jnp.int32)],
        compiler_params=pltpu.CompilerParams(vmem_limit_bytes=VMEM_LIMIT_BYTES),
        name="route_select",
    )(aff2d)
    row_blk = lambda r, offs: (r, 0, 0)
    grid_spec = pltpu.PrefetchScalarGridSpec(
        num_scalar_prefetch=1, grid=(rows,),
        in_specs=[pl.BlockSpec((1, 1, seq), row_blk), pl.BlockSpec((1, 1, seq), row_blk)],
        out_specs=[pl.BlockSpec((1, n_cblk, LANES), row_blk), pl.BlockSpec((1, n_cblk, LANES), row_blk)],
    )
    return pl.pallas_call(
        functools.partial(_compact_kernel, cap, n_chunks),
        grid_spec=grid_spec,
        out_shape=[jax.ShapeDtypeStruct((rows, n_cblk, LANES), jnp.int32),
                   jax.ShapeDtypeStruct((rows, n_cblk, LANES), F32)],
        compiler_params=pltpu.CompilerParams(dimension_semantics=("arbitrary",),
                                             vmem_limit_bytes=VMEM_LIMIT_BYTES),
        name="route_compact",
    )(offs.reshape(-1), key.reshape(rows, 1, seq), aff2d.reshape(rows, 1, seq))


def _gather_kernel(cap, idx_ref, hp_ref, xg_ref):
    b = pl.program_id(0)
    e = pl.program_id(1)
    base = (b * N_EXPERTS + e) * cap

    def group(g, _):
        for k in range(SUBLANES):
            xg_ref[0, 0, g, k:k + 1, :] = hp_ref[0, pl.ds(idx_ref[base + g * SUBLANES + k], 1), :]
        return 0

    lax.fori_loop(0, cap // SUBLANES, group, 0)


def _gather(idx_flat, hp, cap):
    batch, seq, half = hp.shape
    grid_spec = pltpu.PrefetchScalarGridSpec(
        num_scalar_prefetch=1,
        grid=(batch, N_EXPERTS),
        in_specs=[pl.BlockSpec((1, seq, half), lambda b, e, idx: (b, 0, 0), pipeline_mode=pl.Buffered(1))],
        out_specs=pl.BlockSpec((1, 1, cap // SUBLANES, SUBLANES, half), lambda b, e, idx: (b, e, 0, 0, 0)),
    )
    xg = pl.pallas_call(
        functools.partial(_gather_kernel, cap),
        grid_spec=grid_spec,
        out_shape=jax.ShapeDtypeStruct((batch, N_EXPERTS, cap // SUBLANES, SUBLANES, half), hp.dtype),
        compiler_params=pltpu.CompilerParams(dimension_semantics=("arbitrary", "arbitrary"),
                                             vmem_limit_bytes=VMEM_LIMIT_BYTES),
        name="gather",
    )(idx_flat, hp)
    return xg.reshape(batch, N_EXPERTS, cap, half)


def _experts_kernel(cap, tc, apply_norm, out_rows, idx_ref, gsel_ref, xg_ref, wg_ref, wu_ref, wd_ref, x2_ref, gf_ref,
                    o_hbm, acc_ref, y_ref, sem):
    b = pl.program_id(0)
    e = pl.program_id(1)
    base = (b * N_EXPERTS + e) * cap
    half = xg_ref.shape[3]
    n_chunks = cap // tc

    @pl.when(e == 0)
    def _():
        acc_ref[...] = jnp.zeros_like(acc_ref)

    slab = x2_ref.shape[0]
    slab_rows = pl.ds(pl.multiple_of(e * slab, slab), slab)
    acc_ref[slab_rows, :] = acc_ref[slab_rows, :] + x2_ref[...]

    def ffn(ci):
        words = xg_ref[0, 0, ci * tc:(ci + 1) * tc, :]
        x_lo = _unpack_bf16_pair(words, 0)
        x_hi = _unpack_bf16_pair(words, 1)
        gate = _dot(x_lo, wg_ref[0, :half, :]) + _dot(x_hi, wg_ref[0, half:, :])
        up = _dot(x_lo, wu_ref[0, :half, :]) + _dot(x_hi, wu_ref[0, half:, :])
        act = (gate * jax.nn.sigmoid(gate) * up).astype(BF16)
        y_ref[ci % 2] = _dot(act, wd_ref[0])

    def scatter(ci):
        for r0 in range(0, tc, SUBLANES):
            toks = [idx_ref[base + ci * tc + r0 + k] for k in range(SUBLANES)]
            rows = [acc_ref[pl.ds(tok, 1), :] + gsel_ref[base + ci * tc + r0 + k] * y_ref[ci % 2, r0 + k:r0 + k + 1, :]
                    for k, tok in enumerate(toks)]
            for tok, row in zip(toks, rows):
                acc_ref[pl.ds(tok, 1), :] = row

    for ci in range(n_chunks):
        ffn(ci)
        if ci > 0:
            scatter(ci - 1)
    scatter(n_chunks - 1)

    @pl.when(e == N_EXPERTS - 1)
    def _():
        n_tiles = acc_ref.shape[0] // out_rows

        def out_copy(i):
            rows = pl.ds(pl.multiple_of(i * out_rows, out_rows), out_rows)
            return rows, pltpu.make_async_copy(acc_ref.at[rows], o_hbm.at[b, rows], sem)

        def emit(i, _):
            rows, copy = out_copy(i)
            if apply_norm:
                acc_ref[rows, :] = _rms(acc_ref[rows, :], gf_ref[...])
            copy.start()
            return 0

        def drain(i, _):
            out_copy(i)[1].wait()
            return 0

        lax.fori_loop(0, n_tiles, emit, 0)
        lax.fori_loop(0, n_tiles, drain, 0)


def _experts(idx_flat, gsel_flat, xg, wg, wu, wd, x2, gf, apply_norm, seq, tc):
    batch, _, cap, half = xg.shape
    d = 2 * half
    ff = wg.shape[2]
    slab = seq // N_EXPERTS
    grid_spec = pltpu.PrefetchScalarGridSpec(
        num_scalar_prefetch=2,
        grid=(batch, N_EXPERTS),
        in_specs=[
            pl.BlockSpec((1, 1, cap, half), lambda b, e, idx, gs: (b, e, 0, 0)),
            pl.BlockSpec((1, d, ff), lambda b, e, idx, gs: (e, 0, 0)),
            pl.BlockSpec((1, d, ff), lambda b, e, idx, gs: (e, 0, 0)),
            pl.BlockSpec((1, ff, d), lambda b, e, idx, gs: (e, 0, 0)),
            pl.BlockSpec((slab, d), lambda b, e, idx, gs: (b * N_EXPERTS + e, 0)),
            pl.BlockSpec(gf.shape, lambda b, e, idx, gs: (0, 0)),
        ],
        out_specs=pl.BlockSpec(memory_space=pl.ANY),
        scratch_shapes=[pltpu.VMEM((seq, d), F32), pltpu.VMEM((2, tc, d), F32), pltpu.SemaphoreType.DMA],
    )
    return pl.pallas_call(
        functools.partial(_experts_kernel, cap, tc, apply_norm, min(seq, OUT_TILE_ROWS)),
        grid_spec=grid_spec,
        out_shape=jax.ShapeDtypeStruct((batch, seq, d), F32),
        compiler_params=pltpu.CompilerParams(dimension_semantics=("arbitrary", "arbitrary"),
                                             vmem_limit_bytes=EXPERTS_VMEM_LIMIT_BYTES),
        name="experts",
    )(idx_flat, gsel_flat, xg, wg, wu, wd, x2, gf)


def _rot_cols(w):
    half = w.shape[-1] // 2
    return jnp.concatenate([-w[..., half:], w[..., :half]], axis=-1)


def _tile(n, pref):
    return pref if n % pref == 0 else n


def kernel(x, mem, norm_mix_g, w_in, conv_w, conv_b, w_conv_out, q_norm_g, w_uq, kv_norm_g, w_ukv,
           w_mla_out, b_gate, w_mix_out, norm_mem_g, norm_memkv_g, w_mem_q, w_mem_kv, w_mem_out,
           norm_moe_g, w_router, w_exp_gate, w_exp_up, w_exp_down, norm_final_g):
    batch, seq, d = x.shape
    depth = w_in.shape[0]
    t = batch * seq
    q_lora = q_norm_g.shape[1]
    kv_lora = kv_norm_g.shape[1]
    cap = max(1, CAPACITY_FACTOR * seq // N_EXPERTS)
    assert cap % LANES == 0 and seq % LANES == 0 and d % (2 * LANES) == 0

    tm_in = _tile(seq, 512)
    tm_post = _tile(seq, 512)
    tq = _tile(seq, 1024)
    tk = _tile(seq, 512)
    tc = _tile(cap, 256)

    inv = 1.0 / (ROPE_THETA ** (np.arange(0, QK_ROPE, 2, dtype=np.float64) / QK_ROPE))
    ang = np.arange(seq, dtype=np.float64)[:, None] * inv[None, :]
    cos2 = np.concatenate([np.cos(ang), np.cos(ang)], axis=-1)
    sin2 = np.concatenate([np.sin(ang), np.sin(ang)], axis=-1)
    qf = (QK_DIM ** -0.5) * math.log2(math.e)
    kc, ks = jnp.asarray(cos2, F32), jnp.asarray(sin2, F32)
    qc, qs = jnp.asarray(qf * cos2.T, F32), jnp.asarray(qf * sin2.T, F32)

    o_cq = 3 * d
    o_ckv = o_cq + q_lora
    o_kr = o_ckv + kv_lora
    o_gl = o_kr + QK_ROPE
    cols = {"xc": 0, "gb": d, "gc": 2 * d, "cq": o_cq, "ckv": o_ckv}

    x2d = x.reshape(t, d)
    mem2d = mem.reshape(-1, d)
    row = lambda a: a.reshape(1, -1)

    for l in range(depth):
        win = w_in[l].astype(BF16)
        wgl = win[:, o_gl:]
        wkr = jnp.concatenate([win[:, o_kr:o_gl], _rot_cols(win[:, o_kr:o_gl])], axis=1)
        wq3 = w_uq[l].reshape(q_lora, MLA_HEADS, QK_DIM)
        wq = wq3.transpose(1, 2, 0).astype(BF16)
        wqr = _rot_cols(wq3[..., QK_NOPE:]).transpose(1, 2, 0).astype(BF16)
        wkv3 = w_ukv[l].reshape(kv_lora, MLA_HEADS, QK_NOPE + V_HEAD)
        wk = wkv3[..., :QK_NOPE].transpose(1, 0, 2).astype(BF16)
        wv = wkv3[..., QK_NOPE:].transpose(1, 2, 0).astype(BF16)

        wr_hi = w_router[l].T.astype(BF16)
        wr_lo = (w_router[l].T - wr_hi.astype(F32)).astype(BF16)

        v, gb, gates, qt, k, vt = _inproj(
            x2d, batch, seq, row(norm_mix_g[l]), win, wgl, wkr, cols, row(q_norm_g[l]), row(kv_norm_g[l]),
            row(b_gate[l]), wq, wqr, wk, wv, qc, qs, kc, ks, qf, tm_in)
        att = _attention(qt, k, vt, tq, tk).reshape(t, MLA_HEADS * V_HEAD)
        mk, mv = _memkv(mem2d, row(norm_memkv_g[l]), w_mem_kv[l].astype(BF16))
        x2, hp, aff = _post(
            x2d, batch, seq, v, gb, gates, att, conv_w[l], row(conv_b[l]), w_conv_out[l].astype(BF16),
            w_mla_out[l].astype(BF16), w_mix_out[l].astype(BF16), row(norm_mem_g[l]),
            w_mem_q[l].astype(BF16), mk, mv, w_mem_out[l].astype(BF16), row(norm_moe_g[l]),
            wr_hi, wr_lo, tm_post)
        idx, gsel = _route(aff.reshape(batch * N_EXPERTS, seq), cap)
        idx_flat = idx.reshape(-1)
        xg = _gather(idx_flat, hp.reshape(batch, seq, d // 2), cap)
        x2d = _experts(idx_flat, gsel.reshape(-1), xg, w_exp_gate[l].astype(BF16), w_exp_up[l].astype(BF16),
                       w_exp_down[l].astype(BF16), x2, row(norm_final_g), l == depth - 1, seq, tc).reshape(t, d)
    return x2d.reshape(batch, seq, d)
```

```python
import functools
import math

import jax
import jax.numpy as jnp
import numpy as np
from jax import lax
from jax.experimental import pallas as pl
from jax.experimental.pallas import tpu as pltpu

MLA_HEADS = 8
QK_NOPE = 128
QK_ROPE = 64
QK_DIM = QK_NOPE + QK_ROPE
V_HEAD = 128
V_AUG = V_HEAD + 16
ROPE_THETA = 10000.0
MEM_HEADS = 4
N_EXPERTS = 16
CAPACITY_FACTOR = 2
CONV_WIDTH = 3
EPS = 1e-6

LANES = 128
SUBLANES = 8
BF16_SUBLANES = 16
VMEM_LIMIT_BYTES = 56 * 1024 * 1024
EXPERTS_VMEM_LIMIT_BYTES = 60 * 1024 * 1024
OUT_TILE_ROWS = 256
POST_SUBTILE = 256
ATTN_MAX_UNROLL = 4

F32 = jnp.float32
BF16 = jnp.bfloat16


def _const_spec(shape):
    nd = len(shape)
    return pl.BlockSpec(shape, lambda *_: (0,) * nd, pipeline_mode=pl.Buffered(1))


def _rms(x, g):
    return x * lax.rsqrt(jnp.mean(x * x, axis=-1, keepdims=True) + EPS) * g


def _dot(a, b):
    return jnp.dot(a, b, preferred_element_type=F32)


def _dot_t(a, b):
    return lax.dot_general(a, b, (((1,), (1,)), ((), ())), preferred_element_type=F32)


def _inproj_kernel(cols, q_scale, x_ref, g_ref, win_ref, wgl_ref, wkr_ref, qg_ref, kvg_ref, bg_ref,
                   wq_ref, wqr_ref, wk_ref, wv_ref, qc_ref, qs_ref, kc_ref, ks_ref,
                   v_ref, gb_ref, gates_ref, qt_ref, k_ref, vt_ref):
    d = x_ref.shape[1]
    hb = _rms(x_ref[...], g_ref[...]).astype(BF16)

    def proj(name, width):
        lo = cols[name]
        return _dot(hb, win_ref[:, lo:lo + width])

    v_ref[...] = (proj("gc", d) * proj("xc", d)).astype(BF16)
    gb_ref[...] = proj("gb", d).astype(BF16)
    gates_ref[...] = jax.nn.sigmoid(_dot(hb, wgl_ref[...]) + bg_ref[...]).astype(BF16)

    q_lora = qg_ref.shape[1]
    kv_lora = kvg_ref.shape[1]
    qn = _rms(proj("cq", q_lora), qg_ref[...]).astype(BF16)
    kvn = _rms(proj("ckv", kv_lora), kvg_ref[...]).astype(BF16)
    kr2 = _dot(hb, wkr_ref[...])
    k_rope = kr2[:, :QK_ROPE] * kc_ref[...] + kr2[:, QK_ROPE:] * ks_ref[...]
    qc = qc_ref[...]
    qs = qs_ref[...]
    pad_row = lax.broadcasted_iota(jnp.int32, (V_AUG - V_HEAD, x_ref.shape[0]), 0)
    ones_rows = jnp.where(pad_row == 0, 1.0, 0.0).astype(BF16)
    for h in range(MLA_HEADS):
        qt_h = _dot_t(wq_ref[h], qn)
        rope = qt_h[QK_NOPE:] * qc + _dot_t(wqr_ref[h], qn) * qs
        qt_ref[0, h] = jnp.concatenate([qt_h[:QK_NOPE] * q_scale, rope], axis=0).astype(BF16)
        k_ref[0, h] = jnp.concatenate([_dot(kvn, wk_ref[h]), k_rope], axis=-1).astype(BF16)
        vt_ref[0, h, :V_HEAD, :] = _dot_t(wv_ref[h], kvn).astype(BF16)
        vt_ref[0, h, V_HEAD:, :] = ones_rows


def _inproj(x2d, batch, seq, g, win, wgl, wkr, cols, qg, kvg, bg, wq, wqr, wk, wv, qc, qs, kc, ks, q_scale, tm):
    t, d = x2d.shape
    nb = seq // tm
    tok = lambda i: (i, 0)
    pos = lambda i: (i % nb, 0)
    pos_t = lambda i: (0, i % nb)
    head_blk = lambda i: (i // nb, 0, i % nb, 0)
    head_blk_t = lambda i: (i // nb, 0, 0, i % nb)
    in_specs = [
        pl.BlockSpec((tm, d), tok),
        _const_spec(g.shape), _const_spec(win.shape), _const_spec(wgl.shape), _const_spec(wkr.shape),
        _const_spec(qg.shape), _const_spec(kvg.shape),
        _const_spec(bg.shape), _const_spec(wq.shape), _const_spec(wqr.shape), _const_spec(wk.shape),
        _const_spec(wv.shape),
        pl.BlockSpec((QK_ROPE, tm), pos_t), pl.BlockSpec((QK_ROPE, tm), pos_t),
        pl.BlockSpec((tm, QK_ROPE), pos), pl.BlockSpec((tm, QK_ROPE), pos),
    ]
    out_shape = [
        jax.ShapeDtypeStruct((t, d), BF16),
        jax.ShapeDtypeStruct((t, d), BF16),
        jax.ShapeDtypeStruct((t, 2 * d), BF16),
        jax.ShapeDtypeStruct((batch, MLA_HEADS, QK_DIM, seq), BF16),
        jax.ShapeDtypeStruct((batch, MLA_HEADS, seq, QK_DIM), BF16),
        jax.ShapeDtypeStruct((batch, MLA_HEADS, V_AUG, seq), BF16),
    ]
    out_specs = [
        pl.BlockSpec((tm, d), tok), pl.BlockSpec((tm, d), tok), pl.BlockSpec((tm, 2 * d), tok),
        pl.BlockSpec((1, MLA_HEADS, QK_DIM, tm), head_blk_t),
        pl.BlockSpec((1, MLA_HEADS, tm, QK_DIM), head_blk),
        pl.BlockSpec((1, MLA_HEADS, V_AUG, tm), head_blk_t),
    ]
    return pl.pallas_call(
        functools.partial(_inproj_kernel, cols, q_scale),
        grid=(t // tm,), in_specs=in_specs, out_specs=out_specs, out_shape=out_shape,
        compiler_params=pltpu.CompilerParams(dimension_semantics=("arbitrary",),
                                             vmem_limit_bytes=VMEM_LIMIT_BYTES),
        name="inproj",
    )(x2d, g, win, wgl, wkr, qg, kvg, bg, wq, wqr, wk, wv, qc, qs, kc, ks)


def _attn_kernel(tk, unroll, qt_ref, k_ref, vt_ref, o_ref, s_ref):
    qt = qt_ref[0, 0]
    tq = qt.shape[1]
    nk = k_ref.shape[2] // tk

    def scores(j, slot):
        start = pl.multiple_of(j * tk, tk)
        s = _dot(k_ref[0, 0, pl.ds(start, tk), :], qt)
        s_ref[slot] = s
        return jnp.max(s, axis=0, keepdims=True)

    def update(j, slot, mx, m, acc):
        start = pl.multiple_of(j * tk, tk)
        m_new = jnp.maximum(m, mx)
        p = jnp.exp2(s_ref[slot] - m_new)
        acc = jnp.exp2(m - m_new) * acc + _dot(vt_ref[0, 0, :, pl.ds(start, tk)], p.astype(BF16))
        return m_new, acc

    def chunks(j0, carry, last):
        mx, m, acc = carry
        for u in range(unroll):
            if not (last and u == unroll - 1):
                mx_next = scores(j0 + u + 1, (u + 1) % 2)
            m, acc = update(j0 + u, u % 2, mx, m, acc)
            mx = mx_next
        return mx, m, acc

    init = (scores(0, 0), jnp.full((1, tq), -1e30, F32), jnp.zeros((vt_ref.shape[2], tq), F32))
    carry = lax.fori_loop(0, nk // unroll - 1, lambda i, c: chunks(unroll * i, c, False), init)
    _, _, acc = chunks(nk - unroll, carry, True)
    o_ref[0] = (acc[:V_HEAD] / acc[V_HEAD:V_HEAD + 1]).T.astype(BF16)


def _attention(qt, k, vt, tq, tk):
    batch, heads, seq, _ = k.shape
    nk = seq // tk
    assert nk % 2 == 0
    unroll = ATTN_MAX_UNROLL if nk % ATTN_MAX_UNROLL == 0 else 2
    return pl.pallas_call(
        functools.partial(_attn_kernel, tk, unroll),
        grid=(batch, heads, seq // tq),
        in_specs=[
            pl.BlockSpec((1, 1, QK_DIM, tq), lambda b, h, i: (b, h, 0, i)),
            pl.BlockSpec((1, 1, seq, QK_DIM), lambda b, h, i: (b, h, 0, 0)),
            pl.BlockSpec((1, 1, V_AUG, seq), lambda b, h, i: (b, h, 0, 0)),
        ],
        out_specs=pl.BlockSpec((1, tq, V_HEAD), lambda b, h, i: (b, i, h)),
        out_shape=jax.ShapeDtypeStruct((batch, seq, heads * V_HEAD), BF16),
        scratch_shapes=[pltpu.VMEM((2, tk, tq), F32)],
        compiler_params=pltpu.CompilerParams(
            dimension_semantics=("arbitrary", "arbitrary", "arbitrary"),
            vmem_limit_bytes=VMEM_LIMIT_BYTES),
        name="mla_attention",
    )(qt, k, vt)


def _memkv_kernel(m_ref, g_ref, w_ref, k_ref, v_ref):
    d = m_ref.shape[1]
    kv = _dot(_rms(m_ref[...], g_ref[...]).astype(BF16), w_ref[...])
    k_ref[...] = kv[:, :d].astype(BF16)
    v_ref[...] = kv[:, d:].astype(BF16)


def _memkv(mem2d, g, w):
    n, d = mem2d.shape
    return pl.pallas_call(
        _memkv_kernel,
        out_shape=[jax.ShapeDtypeStruct((n, d), BF16), jax.ShapeDtypeStruct((n, d), BF16)],
        compiler_params=pltpu.CompilerParams(vmem_limit_bytes=VMEM_LIMIT_BYTES),
        name="memkv",
    )(mem2d, g, w)


def _pack_bf16_pairs(lo, hi):
    return pltpu.bitcast(pltpu.pack_elementwise([lo, hi], packed_dtype=BF16), jnp.uint32)


def _unpack_bf16_pair(words, index):
    return pltpu.unpack_elementwise(words, index=index, packed_dtype=BF16, unpacked_dtype=F32).astype(BF16)


def _post_kernel(nb, sub, x_ref, v_ref, vprev_ref, vnext_ref, gb_ref, gates_ref, att_ref,
                 cw_ref, cb_ref, wco_ref, wmo_ref, wmix_ref, gmem_ref, wmq_ref, mk_ref, mv_ref, wmout_ref,
                 gmoe_ref, wrh_ref, wrl_ref,
                 x2_ref, hp_ref, aff_ref):
    i = pl.program_id(0)
    tm, d = x_ref.shape
    v = v_ref[...].astype(F32)
    first = (i % nb) == 0
    last = (i % nb) == nb - 1
    halo_prev = jnp.where(first, 0.0, vprev_ref[...].astype(F32)[BF16_SUBLANES - 1:BF16_SUBLANES, :])
    halo_next = jnp.where(last, 0.0, vnext_ref[...].astype(F32)[0:1, :])
    row = lax.broadcasted_iota(jnp.int32, (tm, d), 0)
    v_prev = jnp.where(row == 0, halo_prev, pltpu.roll(v, 1, 0))
    v_next = jnp.where(row == tm - 1, halo_next, pltpu.roll(v, tm - 1, 0))
    conv = cw_ref[0:1, :] * v_prev + cw_ref[1:2, :] * v + cw_ref[2:3, :] * v_next + cb_ref[...]
    conv_in = (gb_ref[...].astype(F32) * conv).astype(BF16)
    hd = d // MEM_HEADS
    tiles = [pl.ds(r0, sub) for r0 in range(0, tm, sub)]
    y_conv = [_dot(conv_in[r0:r0 + sub], wco_ref[...]) for r0 in range(0, tm, sub)]
    y_mla = [_dot(att_ref[rows, :], wmo_ref[...]) for rows in tiles]
    mixed = [(gates_ref[rows, :d].astype(F32) * yc + gates_ref[rows, d:].astype(F32) * ym).astype(BF16)
             for rows, yc, ym in zip(tiles, y_conv, y_mla)]
    x1 = [x_ref[rows, :] + _dot(mx, wmix_ref[...]) for rows, mx in zip(tiles, mixed)]
    qm = [(_dot((x * gmem_ref[...]).astype(BF16), wmq_ref[...])
           * (lax.rsqrt(jnp.mean(x * x, axis=-1, keepdims=True) + EPS) * (hd ** -0.5))).astype(BF16) for x in x1]
    att_m = []
    for q in qm:
        outs = []
        for h in range(MEM_HEADS):
            sl = slice(h * hd, (h + 1) * hd)
            s = _dot_t(q[:, sl], mk_ref[:, sl])
            p = jnp.exp(s - jnp.max(s, axis=-1, keepdims=True))
            inv_l = 1.0 / jnp.sum(p, axis=-1, keepdims=True)
            outs.append((_dot(p.astype(BF16), mv_ref[:, sl]) * inv_l).astype(BF16))
        att_m.append(jnp.concatenate(outs, axis=-1))
    x2 = [x + _dot(o, wmout_ref[...]) for x, o in zip(x1, att_m)]
    for rows, x in zip(tiles, x2):
        x2_ref[rows, :] = x
        h3 = _rms(x, gmoe_ref[...])
        hp_ref[rows, :] = _pack_bf16_pairs(h3[:, :d // 2], h3[:, d // 2:])
        h_hi = h3.astype(BF16)
        h_lo = (h3 - h_hi.astype(F32)).astype(BF16)
        logits = (_dot_t(wrh_ref[...], h_hi) + _dot_t(wrh_ref[...], h_lo)
                  + _dot_t(wrl_ref[...], h_hi) + _dot_t(wrl_ref[...], h_lo))
        e = jnp.exp(logits - jnp.max(logits, axis=0, keepdims=True))
        aff_ref[0, :, rows] = e / jnp.sum(e, axis=0, keepdims=True)


def _post(x2d, batch, seq, v, gb, gates, att, cw, cb, wco, wmo, wmix, gmem, wmq, mk, mv, wmout, gmoe, wrh, wrl, tm):
    t, d = x2d.shape
    nb = seq // tm
    hb = tm // BF16_SUBLANES
    n_halo = t // BF16_SUBLANES
    tok = lambda i: (i, 0)
    mem_len = mk.shape[0] // batch
    in_specs = [
        pl.BlockSpec((tm, d), tok), pl.BlockSpec((tm, d), tok),
        pl.BlockSpec((BF16_SUBLANES, d), lambda i: (jnp.maximum(i * hb - 1, 0), 0)),
        pl.BlockSpec((BF16_SUBLANES, d), lambda i: (jnp.minimum((i + 1) * hb, n_halo - 1), 0)),
        pl.BlockSpec((tm, d), tok), pl.BlockSpec((tm, 2 * d), tok), pl.BlockSpec((tm, d), tok),
        _const_spec(cw.shape), _const_spec(cb.shape), _const_spec(wco.shape), _const_spec(wmo.shape),
        _const_spec(wmix.shape), _const_spec(gmem.shape), _const_spec(wmq.shape),
        pl.BlockSpec((mem_len, d), lambda i: (i // nb, 0)),
        pl.BlockSpec((mem_len, d), lambda i: (i // nb, 0)),
        _const_spec(wmout.shape), _const_spec(gmoe.shape), _const_spec(wrh.shape), _const_spec(wrl.shape),
    ]
    out_shape = [
        jax.ShapeDtypeStruct((t, d), F32),
        jax.ShapeDtypeStruct((t, d // 2), jnp.uint32),
        jax.ShapeDtypeStruct((batch, N_EXPERTS, seq), F32),
    ]
    out_specs = [
        pl.BlockSpec((tm, d), tok), pl.BlockSpec((tm, d // 2), tok),
        pl.BlockSpec((1, N_EXPERTS, tm), lambda i: (i // nb, 0, i % nb)),
    ]
    return pl.pallas_call(
        functools.partial(_post_kernel, nb, min(tm, POST_SUBTILE)),
        grid=(t // tm,), in_specs=in_specs, out_specs=out_specs, out_shape=out_shape,
        compiler_params=pltpu.CompilerParams(dimension_semantics=("arbitrary",),
                                             vmem_limit_bytes=VMEM_LIMIT_BYTES),
        name="post",
    )(x2d, v, v, v, gb, gates, att, cw, cb, wco, wmo, wmix, gmem, wmq, mk, mv, wmout, gmoe, wrh, wrl)


def _select_kernel(cap, aff_ref, key_ref, offs_ref):
    rows, seq = aff_ref.shape
    n_chunks = seq // LANES
    aff = aff_ref[...]

    def count_ge(x):
        return jnp.sum((aff >= x).astype(F32), axis=-1, keepdims=True)

    def search(b, t):
        cand = t | (jnp.int32(1) << (30 - b))
        return jnp.where(count_ge(pltpu.bitcast(cand, F32)) >= cap, cand, t)

    thr = lax.fori_loop(0, 31, search, jnp.zeros((rows, 1), jnp.int32))

    def refine(_, lohi):
        lo, hi = lohi
        mid = 0.5 * (lo + hi)
        take = count_ge(mid) >= cap
        return jnp.where(take, mid, lo), jnp.where(take, hi, mid)

    lo, hi = lax.fori_loop(0, 30, refine, (pltpu.bitcast(thr, F32), pltpu.bitcast(thr + 1, F32)))
    gt = aff >= hi
    eq = (aff >= lo) & (aff < hi)
    need = cap - jnp.sum(gt.astype(F32), axis=-1, keepdims=True)

    tri = (lax.broadcasted_iota(jnp.int32, (LANES, LANES), 0)
           <= lax.broadcasted_iota(jnp.int32, (LANES, LANES), 1)).astype(BF16)
    ones = jnp.ones((LANES, LANES), BF16)
    lane = lax.broadcasted_iota(jnp.int32, (rows, LANES), 1)
    run_eq = jnp.zeros((rows, LANES), F32)
    run_sel = jnp.zeros((rows, LANES), F32)
    offs = jnp.zeros((rows, LANES), F32)
    for j in range(n_chunks):
        sl = slice(j * LANES, (j + 1) * LANES)
        eq_j = eq[:, sl]
        eq_b = eq_j.astype(F32).astype(BF16)
        eq_rank = _dot(eq_b, tri) + run_eq
        run_eq = run_eq + _dot(eq_b, ones)
        sel_j = gt[:, sl] | (eq_j & (eq_rank <= need))
        sel_b = sel_j.astype(F32).astype(BF16)
        pos = _dot(sel_b, tri) + run_sel
        run_sel = run_sel + _dot(sel_b, ones)
        key_ref[:, sl] = jnp.where(sel_j, pos, 0.0)
        offs = jnp.where(lane == j, run_sel, offs)
    offs_ref[...] = offs.astype(jnp.int32)


def _compact_kernel(cap, n_chunks, offs_ref, key_ref, aff_ref, idx_ref, gsel_ref):
    r = pl.program_id(0)
    n_cblk = cap // LANES
    slot = lax.broadcasted_iota(jnp.int32, (LANES, LANES), 0) + 1
    lane = lax.broadcasted_iota(jnp.int32, (LANES, LANES), 1)
    obase = r * LANES
    j0 = jnp.int32(0)
    for cb in range(n_cblk):
        want = (slot + cb * LANES).astype(F32)
        j0 = lax.while_loop(lambda j: (j < n_chunks - 1) & (offs_ref[obase + j] <= cb * LANES),
                            lambda j: j + 1, j0)
        j1 = lax.while_loop(lambda j: (j < n_chunks - 1) & (offs_ref[obase + j] < (cb + 1) * LANES),
                            lambda j: j + 1, j0)

        def per_chunk(j, carry):
            tok_sel, g_sel = carry
            start = pl.multiple_of(j * LANES, LANES)
            hit = key_ref[0, :, pl.ds(start, LANES)] == want
            tok_sel = jnp.where(hit, (lane + j * LANES).astype(F32), tok_sel)
            g_sel = jnp.where(hit, aff_ref[0, :, pl.ds(start, LANES)], g_sel)
            return tok_sel, g_sel

        zero = jnp.zeros((LANES, LANES), F32)
        tok_sel, g_sel = lax.fori_loop(j0, j1 + 1, per_chunk, (zero, zero))
        idx_ref[0, cb:cb + 1, :] = jnp.sum(tok_sel.T, axis=0, keepdims=True).astype(jnp.int32)
        gsel_ref[0, cb:cb + 1, :] = jnp.sum(g_sel.T, axis=0, keepdims=True)
        j0 = j1


def _route(aff2d, cap):
    rows, seq = aff2d.shape
    n_chunks = seq // LANES
    n_cblk = cap // LANES
    assert n_chunks <= LANES
    key, offs = pl.pallas_call(
        functools.partial(_select_kernel, cap),
        out_shape=[jax.ShapeDtypeStruct((rows, seq), F32), jax.ShapeDtypeStruct((rows, LANES), jnp.int32)],
        compiler_params=pltpu.CompilerParams(vmem_limit_bytes=VMEM_LIMIT_BYTES),
        name="route_select",
    )(aff2d)
    row_blk = lambda r, offs: (r, 0, 0)
    grid_spec = pltpu.PrefetchScalarGridSpec(
        num_scalar_prefetch=1, grid=(rows,),
        in_specs=[pl.BlockSpec((1, 1, seq), row_blk), pl.BlockSpec((1, 1, seq), row_blk)],
        out_specs=[pl.BlockSpec((1, n_cblk, LANES), row_blk), pl.BlockSpec((1, n_cblk, LANES), row_blk)],
    )
    return pl.pallas_call(
        functools.partial(_compact_kernel, cap, n_chunks),
        grid_spec=grid_spec,
        out_shape=[jax.ShapeDtypeStruct((rows, n_cblk, LANES), jnp.int32),
                   jax.ShapeDtypeStruct((rows, n_cblk, LANES), F32)],
        compiler_params=pltpu.CompilerParams(dimension_semantics=("arbitrary",),
                                             vmem_limit_bytes=VMEM_LIMIT_BYTES),
        name="route_compact",
    )(offs.reshape(-1), key.reshape(rows, 1, seq), aff2d.reshape(rows, 1, seq))


def _gather_kernel(cap, idx_ref, hp_ref, wg_ref, wu_ref, wd_ref, xg_ref, wgb_ref, wub_ref, wdb_ref):
    b = pl.program_id(0)
    e = pl.program_id(1)
    base = (b * N_EXPERTS + e) * cap
    wgb_ref[...] = wg_ref[...].astype(BF16)
    wub_ref[...] = wu_ref[...].astype(BF16)
    wdb_ref[...] = wd_ref[...].astype(BF16)
    for c in range(cap):
        g, k = divmod(c, SUBLANES)
        xg_ref[0, 0, g, k:k + 1, :] = hp_ref[0, pl.ds(idx_ref[base + c], 1), :]


def _gather(idx_flat, hp, wg, wu, wd, cap):
    batch, seq, half = hp.shape
    n_exp, rows, cols = wg.shape
    n_steps = batch * N_EXPERTS
    split = n_steps // n_exp
    assert n_steps % n_exp == 0 and rows % (split * BF16_SUBLANES) == 0 and wd.shape == (n_exp, cols, rows)
    w_blk = lambda b, e, idx: ((b * N_EXPERTS + e) // split, (b * N_EXPERTS + e) % split, 0)
    w_spec = lambda w: pl.BlockSpec((1, w.shape[1] // split, w.shape[2]), w_blk)
    grid_spec = pltpu.PrefetchScalarGridSpec(
        num_scalar_prefetch=1,
        grid=(batch, N_EXPERTS),
        in_specs=[pl.BlockSpec((1, seq, half), lambda b, e, idx: (b, 0, 0), pipeline_mode=pl.Buffered(1)),
                  w_spec(wg), w_spec(wu), w_spec(wd)],
        out_specs=[pl.BlockSpec((1, 1, cap // SUBLANES, SUBLANES, half), lambda b, e, idx: (b, e, 0, 0, 0)),
                   w_spec(wg), w_spec(wu), w_spec(wd)],
    )
    xg, wgb, wub, wdb = pl.pallas_call(
        functools.partial(_gather_kernel, cap),
        grid_spec=grid_spec,
        out_shape=[jax.ShapeDtypeStruct((batch, N_EXPERTS, cap // SUBLANES, SUBLANES, half), hp.dtype),
                   jax.ShapeDtypeStruct(wg.shape, BF16), jax.ShapeDtypeStruct(wu.shape, BF16),
                   jax.ShapeDtypeStruct(wd.shape, BF16)],
        compiler_params=pltpu.CompilerParams(dimension_semantics=("arbitrary", "arbitrary"),
                                             vmem_limit_bytes=VMEM_LIMIT_BYTES),
        name="gather",
    )(idx_flat, hp, wg, wu, wd)
    return xg.reshape(batch, N_EXPERTS, cap, half), wgb, wub, wdb


def _experts_kernel(cap, tc, apply_norm, out_rows, idx_ref, gsel_ref, xg_ref, wg_ref, wu_ref, wd_ref, x2_ref, gf_ref,
                    o_hbm, acc_ref, y_ref, sem):
    b = pl.program_id(0)
    e = pl.program_id(1)
    base = (b * N_EXPERTS + e) * cap
    half = xg_ref.shape[3]
    n_chunks = cap // tc

    @pl.when(e == 0)
    def _():
        acc_ref[...] = jnp.zeros_like(acc_ref)

    slab = x2_ref.shape[0]
    slab_rows = pl.ds(pl.multiple_of(e * slab, slab), slab)
    acc_ref[slab_rows, :] = acc_ref[slab_rows, :] + x2_ref[...]

    def ffn(ci):
        words = xg_ref[0, 0, ci * tc:(ci + 1) * tc, :]
        x_lo = _unpack_bf16_pair(words, 0)
        x_hi = _unpack_bf16_pair(words, 1)
        gate = _dot(x_lo, wg_ref[0, :half, :]) + _dot(x_hi, wg_ref[0, half:, :])
        up = _dot(x_lo, wu_ref[0, :half, :]) + _dot(x_hi, wu_ref[0, half:, :])
        act = (gate * jax.nn.sigmoid(gate) * up).astype(BF16)
        y_ref[ci % 2] = _dot(act, wd_ref[0])

    def scatter(ci):
        for r0 in range(0, tc, SUBLANES):
            toks = [idx_ref[base + ci * tc + r0 + k] for k in range(SUBLANES)]
            rows = [acc_ref[pl.ds(tok, 1), :] + gsel_ref[base + ci * tc + r0 + k] * y_ref[ci % 2, r0 + k:r0 + k + 1, :]
                    for k, tok in enumerate(toks)]
            for tok, row in zip(toks, rows):
                acc_ref[pl.ds(tok, 1), :] = row

    for ci in range(n_chunks):
        ffn(ci)
        if ci > 0:
            scatter(ci - 1)
    scatter(n_chunks - 1)

    @pl.when(e == N_EXPERTS - 1)
    def _():
        n_tiles = acc_ref.shape[0] // out_rows

        def out_copy(i):
            rows = pl.ds(pl.multiple_of(i * out_rows, out_rows), out_rows)
            return rows, pltpu.make_async_copy(acc_ref.at[rows], o_hbm.at[b, rows], sem)

        def emit(i, _):
            rows, copy = out_copy(i)
            if apply_norm:
                acc_ref[rows, :] = _rms(acc_ref[rows, :], gf_ref[...])
            copy.start()
            return 0

        def drain(i, _):
            out_copy(i)[1].wait()
            return 0

        lax.fori_loop(0, n_tiles, emit, 0)
        lax.fori_loop(0, n_tiles, drain, 0)


def _experts(idx_flat, gsel_flat, xg, wg, wu, wd, x2, gf, apply_norm, seq, tc):
    batch, _, cap, half = xg.shape
    d = 2 * half
    ff = wg.shape[2]
    slab = seq // N_EXPERTS
    grid_spec = pltpu.PrefetchScalarGridSpec(
        num_scalar_prefetch=2,
        grid=(batch, N_EXPERTS),
        in_specs=[
            pl.BlockSpec((1, 1, cap, half), lambda b, e, idx, gs: (b, e, 0, 0)),
            pl.BlockSpec((1, d, ff), lambda b, e, idx, gs: (e, 0, 0)),
            pl.BlockSpec((1, d, ff), lambda b, e, idx, gs: (e, 0, 0)),
            pl.BlockSpec((1, ff, d), lambda b, e, idx, gs: (e, 0, 0)),
            pl.BlockSpec((slab, d), lambda b, e, idx, gs: (b * N_EXPERTS + e, 0)),
            pl.BlockSpec(gf.shape, lambda b, e, idx, gs: (0, 0)),
        ],
        out_specs=pl.BlockSpec(memory_space=pl.ANY),
        scratch_shapes=[pltpu.VMEM((seq, d), F32), pltpu.VMEM((2, tc, d), F32), pltpu.SemaphoreType.DMA],
    )
    return pl.pallas_call(
        functools.partial(_experts_kernel, cap, tc, apply_norm, min(seq, OUT_TILE_ROWS)),
        grid_spec=grid_spec,
        out_shape=jax.ShapeDtypeStruct((batch, seq, d), F32),
        compiler_params=pltpu.CompilerParams(dimension_semantics=("arbitrary", "arbitrary"),
                                             vmem_limit_bytes=EXPERTS_VMEM_LIMIT_BYTES),
        name="experts",
    )(idx_flat, gsel_flat, xg, wg, wu, wd, x2, gf)


def _rot_cols(w):
    half = w.shape[-1] // 2
    return jnp.concatenate([-w[..., half:], w[..., :half]], axis=-1)


def _tile(n, pref):
    return pref if n % pref == 0 else n


def kernel(x, mem, norm_mix_g, w_in, conv_w, conv_b, w_conv_out, q_norm_g, w_uq, kv_norm_g, w_ukv,
           w_mla_out, b_gate, w_mix_out, norm_mem_g, norm_memkv_g, w_mem_q, w_mem_kv, w_mem_out,
           norm_moe_g, w_router, w_exp_gate, w_exp_up, w_exp_down, norm_final_g):
    batch, seq, d = x.shape
    depth = w_in.shape[0]
    t = batch * seq
    q_lora = q_norm_g.shape[1]
    kv_lora = kv_norm_g.shape[1]
    cap = max(1, CAPACITY_FACTOR * seq // N_EXPERTS)
    assert cap % LANES == 0 and seq % LANES == 0 and d % (2 * LANES) == 0

    tm_in = _tile(seq, 512)
    tm_post = _tile(seq, 512)
    tq = _tile(seq, 1024)
    tk = _tile(seq, 512)
    tc = _tile(cap, 256)

    inv = 1.0 / (ROPE_THETA ** (np.arange(0, QK_ROPE, 2, dtype=np.float64) / QK_ROPE))
    ang = np.arange(seq, dtype=np.float64)[:, None] * inv[None, :]
    cos2 = np.concatenate([np.cos(ang), np.cos(ang)], axis=-1)
    sin2 = np.concatenate([np.sin(ang), np.sin(ang)], axis=-1)
    qf = (QK_DIM ** -0.5) * math.log2(math.e)
    kc, ks = jnp.asarray(cos2, F32), jnp.asarray(sin2, F32)
    qc, qs = jnp.asarray(qf * cos2.T, F32), jnp.asarray(qf * sin2.T, F32)

    o_cq = 3 * d
    o_ckv = o_cq + q_lora
    o_kr = o_ckv + kv_lora
    o_gl = o_kr + QK_ROPE
    cols = {"xc": 0, "gb": d, "gc": 2 * d, "cq": o_cq, "ckv": o_ckv}

    x2d = x.reshape(t, d)
    mem2d = mem.reshape(-1, d)
    row = lambda a: a.reshape(1, -1)

    for l in range(depth):
        win = w_in[l].astype(BF16)
        wgl = win[:, o_gl:]
        wkr = jnp.concatenate([win[:, o_kr:o_gl], _rot_cols(win[:, o_kr:o_gl])], axis=1)
        wq3 = w_uq[l].reshape(q_lora, MLA_HEADS, QK_DIM)
        wq = wq3.transpose(1, 2, 0).astype(BF16)
        wqr = _rot_cols(wq3[..., QK_NOPE:]).transpose(1, 2, 0).astype(BF16)
        wkv3 = w_ukv[l].reshape(kv_lora, MLA_HEADS, QK_NOPE + V_HEAD)
        wk = wkv3[..., :QK_NOPE].transpose(1, 0, 2).astype(BF16)
        wv = wkv3[..., QK_NOPE:].transpose(1, 2, 0).astype(BF16)

        wr_hi = w_router[l].T.astype(BF16)
        wr_lo = (w_router[l].T - wr_hi.astype(F32)).astype(BF16)

        v, gb, gates, qt, k, vt = _inproj(
            x2d, batch, seq, row(norm_mix_g[l]), win, wgl, wkr, cols, row(q_norm_g[l]), row(kv_norm_g[l]),
            row(b_gate[l]), wq, wqr, wk, wv, qc, qs, kc, ks, qf, tm_in)
        att = _attention(qt, k, vt, tq, tk).reshape(t, MLA_HEADS * V_HEAD)
        mk, mv = _memkv(mem2d, row(norm_memkv_g[l]), w_mem_kv[l].astype(BF16))
        x2, hp, aff = _post(
            x2d, batch, seq, v, gb, gates, att, conv_w[l], row(conv_b[l]), w_conv_out[l].astype(BF16),
            w_mla_out[l].astype(BF16), w_mix_out[l].astype(BF16), row(norm_mem_g[l]),
            w_mem_q[l].astype(BF16), mk, mv, w_mem_out[l].astype(BF16), row(norm_moe_g[l]),
            wr_hi, wr_lo, tm_post)
        idx, gsel = _route(aff.reshape(batch * N_EXPERTS, seq), cap)
        idx_flat = idx.reshape(-1)
        xg, wgb, wub, wdb = _gather(idx_flat, hp.reshape(batch, seq, d // 2), w_exp_gate[l], w_exp_up[l],
                                    w_exp_down[l], cap)
        x2d = _experts(idx_flat, gsel.reshape(-1), xg, wgb, wub, wdb, x2, row(norm_final_g),
                       l == depth - 1, seq, tc).reshape(t, d)
    return x2d.reshape(batch, seq, d)
```

```python
import functools
import math

import jax
import jax.numpy as jnp
import numpy as np
from jax import lax
from jax.experimental import pallas as pl
from jax.experimental.pallas import tpu as pltpu

MLA_HEADS = 8
QK_NOPE = 128
QK_ROPE = 64
QK_DIM = QK_NOPE + QK_ROPE
V_HEAD = 128
V_AUG = V_HEAD + 16
ROPE_THETA = 10000.0
MEM_HEADS = 4
N_EXPERTS = 16
CAPACITY_FACTOR = 2
CONV_WIDTH = 3
EPS = 1e-6

LANES = 128
SUBLANES = 8
BF16_SUBLANES = 16
VMEM_LIMIT_BYTES = 56 * 1024 * 1024
EXPERTS_VMEM_LIMIT_BYTES = 60 * 1024 * 1024
OUT_TILE_ROWS = 256
SIDE_CAST_SLICE_BYTES = 1024 * 1024
POST_SUBTILE = 256
ATTN_MAX_UNROLL = 4

F32 = jnp.float32
BF16 = jnp.bfloat16


def _const_spec(shape):
    nd = len(shape)
    return pl.BlockSpec(shape, lambda *_: (0,) * nd, pipeline_mode=pl.Buffered(1))


def _rms(x, g):
    return x * lax.rsqrt(jnp.mean(x * x, axis=-1, keepdims=True) + EPS) * g


def _dot(a, b):
    return jnp.dot(a, b, preferred_element_type=F32)


def _dot_t(a, b):
    return lax.dot_general(a, b, (((1,), (1,)), ((), ())), preferred_element_type=F32)


def _inproj_kernel(cols, q_scale, x_ref, g_ref, win_ref, wgl_ref, wkr_ref, qg_ref, kvg_ref, bg_ref,
                   wq_ref, wqr_ref, wk_ref, wv_ref, qc_ref, qs_ref, kc_ref, ks_ref,
                   v_ref, gb_ref, gates_ref, qt_ref, k_ref, vt_ref):
    d = x_ref.shape[1]
    hb = _rms(x_ref[...], g_ref[...]).astype(BF16)

    def proj(name, width):
        lo = cols[name]
        return _dot(hb, win_ref[:, lo:lo + width])

    v_ref[...] = (proj("gc", d) * proj("xc", d)).astype(BF16)
    gb_ref[...] = proj("gb", d).astype(BF16)
    gates_ref[...] = jax.nn.sigmoid(_dot(hb, wgl_ref[...]) + bg_ref[...]).astype(BF16)

    q_lora = qg_ref.shape[1]
    kv_lora = kvg_ref.shape[1]
    qn = _rms(proj("cq", q_lora), qg_ref[...]).astype(BF16)
    kvn = _rms(proj("ckv", kv_lora), kvg_ref[...]).astype(BF16)
    kr2 = _dot(hb, wkr_ref[...])
    k_rope = kr2[:, :QK_ROPE] * kc_ref[...] + kr2[:, QK_ROPE:] * ks_ref[...]
    qc = qc_ref[...]
    qs = qs_ref[...]
    pad_row = lax.broadcasted_iota(jnp.int32, (V_AUG - V_HEAD, x_ref.shape[0]), 0)
    ones_rows = jnp.where(pad_row == 0, 1.0, 0.0).astype(BF16)
    for h in range(MLA_HEADS):
        qt_h = _dot_t(wq_ref[h], qn)
        rope = qt_h[QK_NOPE:] * qc + _dot_t(wqr_ref[h], qn) * qs
        qt_ref[0, h] = jnp.concatenate([qt_h[:QK_NOPE] * q_scale, rope], axis=0).astype(BF16)
        k_ref[0, h] = jnp.concatenate([_dot(kvn, wk_ref[h]), k_rope], axis=-1).astype(BF16)
        vt_ref[0, h, :V_HEAD, :] = _dot_t(wv_ref[h], kvn).astype(BF16)
        vt_ref[0, h, V_HEAD:, :] = ones_rows


def _inproj(x2d, batch, seq, g, win, wgl, wkr, cols, qg, kvg, bg, wq, wqr, wk, wv, qc, qs, kc, ks, q_scale, tm):
    t, d = x2d.shape
    nb = seq // tm
    tok = lambda i: (i, 0)
    pos = lambda i: (i % nb, 0)
    pos_t = lambda i: (0, i % nb)
    head_blk = lambda i: (i // nb, 0, i % nb, 0)
    head_blk_t = lambda i: (i // nb, 0, 0, i % nb)
    in_specs = [
        pl.BlockSpec((tm, d), tok),
        _const_spec(g.shape), _const_spec(win.shape), _const_spec(wgl.shape), _const_spec(wkr.shape),
        _const_spec(qg.shape), _const_spec(kvg.shape),
        _const_spec(bg.shape), _const_spec(wq.shape), _const_spec(wqr.shape), _const_spec(wk.shape),
        _const_spec(wv.shape),
        pl.BlockSpec((QK_ROPE, tm), pos_t), pl.BlockSpec((QK_ROPE, tm), pos_t),
        pl.BlockSpec((tm, QK_ROPE), pos), pl.BlockSpec((tm, QK_ROPE), pos),
    ]
    out_shape = [
        jax.ShapeDtypeStruct((t, d), BF16),
        jax.ShapeDtypeStruct((t, d), BF16),
        jax.ShapeDtypeStruct((t, 2 * d), BF16),
        jax.ShapeDtypeStruct((batch, MLA_HEADS, QK_DIM, seq), BF16),
        jax.ShapeDtypeStruct((batch, MLA_HEADS, seq, QK_DIM), BF16),
        jax.ShapeDtypeStruct((batch, MLA_HEADS, V_AUG, seq), BF16),
    ]
    out_specs = [
        pl.BlockSpec((tm, d), tok), pl.BlockSpec((tm, d), tok), pl.BlockSpec((tm, 2 * d), tok),
        pl.BlockSpec((1, MLA_HEADS, QK_DIM, tm), head_blk_t),
        pl.BlockSpec((1, MLA_HEADS, tm, QK_DIM), head_blk),
        pl.BlockSpec((1, MLA_HEADS, V_AUG, tm), head_blk_t),
    ]
    return pl.pallas_call(
        functools.partial(_inproj_kernel, cols, q_scale),
        grid=(t // tm,), in_specs=in_specs, out_specs=out_specs, out_shape=out_shape,
        compiler_params=pltpu.CompilerParams(dimension_semantics=("arbitrary",),
                                             vmem_limit_bytes=VMEM_LIMIT_BYTES),
        name="inproj",
    )(x2d, g, win, wgl, wkr, qg, kvg, bg, wq, wqr, wk, wv, qc, qs, kc, ks)


def _attn_kernel(tk, unroll, n_cast, qt_ref, k_ref, vt_ref, *refs):
    cast_in, o_ref, cast_out, s_ref = refs[:n_cast], refs[n_cast], refs[n_cast + 1:2 * n_cast + 1], refs[-1]
    for src, dst in zip(cast_in, cast_out):
        dst[...] = src[...].astype(BF16)
    qt = qt_ref[0, 0]
    tq = qt.shape[1]
    nk = k_ref.shape[2] // tk

    def scores(j, slot):
        start = pl.multiple_of(j * tk, tk)
        s = _dot(k_ref[0, 0, pl.ds(start, tk), :], qt)
        s_ref[slot] = s
        return jnp.max(s, axis=0, keepdims=True)

    def update(j, slot, mx, m, acc):
        start = pl.multiple_of(j * tk, tk)
        m_new = jnp.maximum(m, mx)
        p = jnp.exp2(s_ref[slot] - m_new)
        acc = jnp.exp2(m - m_new) * acc + _dot(vt_ref[0, 0, :, pl.ds(start, tk)], p.astype(BF16))
        return m_new, acc

    def chunks(j0, carry, last):
        mx, m, acc = carry
        for u in range(unroll):
            if not (last and u == unroll - 1):
                mx_next = scores(j0 + u + 1, (u + 1) % 2)
            m, acc = update(j0 + u, u % 2, mx, m, acc)
            mx = mx_next
        return mx, m, acc

    init = (scores(0, 0), jnp.full((1, tq), -1e30, F32), jnp.zeros((vt_ref.shape[2], tq), F32))
    carry = lax.fori_loop(0, nk // unroll - 1, lambda i, c: chunks(unroll * i, c, False), init)
    _, _, acc = chunks(nk - unroll, carry, True)
    o_ref[0] = (acc[:V_HEAD] / acc[V_HEAD:V_HEAD + 1]).T.astype(BF16)


def _side_cast_fits(w, n_steps):
    rows, cols = w.shape
    slice_rows = rows // n_steps
    return (rows % n_steps == 0 and slice_rows % BF16_SUBLANES == 0
            and slice_rows * cols * 4 <= SIDE_CAST_SLICE_BYTES)


def _attention(qt, k, vt, tq, tk, side_casts):
    batch, heads, seq, _ = k.shape
    nk = seq // tk
    nq = seq // tq
    n_steps = batch * heads * nq
    assert nk % 2 == 0
    unroll = ATTN_MAX_UNROLL if nk % ATTN_MAX_UNROLL == 0 else 2
    step = lambda b, h, i: ((b * heads + h) * nq + i, 0)
    cast_specs = [pl.BlockSpec((w.shape[0] // n_steps, w.shape[1]), step) for w in side_casts]
    outs = pl.pallas_call(
        functools.partial(_attn_kernel, tk, unroll, len(side_casts)),
        grid=(batch, heads, nq),
        in_specs=[
            pl.BlockSpec((1, 1, QK_DIM, tq), lambda b, h, i: (b, h, 0, i)),
            pl.BlockSpec((1, 1, seq, QK_DIM), lambda b, h, i: (b, h, 0, 0)),
            pl.BlockSpec((1, 1, V_AUG, seq), lambda b, h, i: (b, h, 0, 0)),
        ] + cast_specs,
        out_specs=[pl.BlockSpec((1, tq, V_HEAD), lambda b, h, i: (b, i, h))] + cast_specs,
        out_shape=[jax.ShapeDtypeStruct((batch, seq, heads * V_HEAD), BF16)]
                  + [jax.ShapeDtypeStruct(w.shape, BF16) for w in side_casts],
        scratch_shapes=[pltpu.VMEM((2, tk, tq), F32)],
        compiler_params=pltpu.CompilerParams(
            dimension_semantics=("arbitrary", "arbitrary", "arbitrary"),
            vmem_limit_bytes=VMEM_LIMIT_BYTES),
        name="mla_attention",
    )(qt, k, vt, *side_casts)
    return outs[0], outs[1:]


def _memkv_kernel(m_ref, g_ref, w_ref, k_ref, v_ref):
    d = m_ref.shape[1]
    kv = _dot(_rms(m_ref[...], g_ref[...]).astype(BF16), w_ref[...])
    k_ref[...] = kv[:, :d].astype(BF16)
    v_ref[...] = kv[:, d:].astype(BF16)


def _memkv(mem2d, g, w):
    n, d = mem2d.shape
    return pl.pallas_call(
        _memkv_kernel,
        out_shape=[jax.ShapeDtypeStruct((n, d), BF16), jax.ShapeDtypeStruct((n, d), BF16)],
        compiler_params=pltpu.CompilerParams(vmem_limit_bytes=VMEM_LIMIT_BYTES),
        name="memkv",
    )(mem2d, g, w)


def _pack_bf16_pairs(lo, hi):
    return pltpu.bitcast(pltpu.pack_elementwise([lo, hi], packed_dtype=BF16), jnp.uint32)


def _unpack_bf16_pair(words, index):
    return pltpu.unpack_elementwise(words, index=index, packed_dtype=BF16, unpacked_dtype=F32).astype(BF16)


def _post_kernel(nb, sub, x_ref, v_ref, vprev_ref, vnext_ref, gb_ref, gates_ref, att_ref,
                 cw_ref, cb_ref, wco_ref, wmo_ref, wmix_ref, gmem_ref, wmq_ref, mk_ref, mv_ref, wmout_ref,
                 gmoe_ref, wrh_ref, wrl_ref,
                 x2_ref, hp_ref, aff_ref):
    i = pl.program_id(0)
    tm, d = x_ref.shape
    v = v_ref[...].astype(F32)
    first = (i % nb) == 0
    last = (i % nb) == nb - 1
    halo_prev = jnp.where(first, 0.0, vprev_ref[...].astype(F32)[BF16_SUBLANES - 1:BF16_SUBLANES, :])
    halo_next = jnp.where(last, 0.0, vnext_ref[...].astype(F32)[0:1, :])
    row = lax.broadcasted_iota(jnp.int32, (tm, d), 0)
    v_prev = jnp.where(row == 0, halo_prev, pltpu.roll(v, 1, 0))
    v_next = jnp.where(row == tm - 1, halo_next, pltpu.roll(v, tm - 1, 0))
    conv = cw_ref[0:1, :] * v_prev + cw_ref[1:2, :] * v + cw_ref[2:3, :] * v_next + cb_ref[...]
    conv_in = (gb_ref[...].astype(F32) * conv).astype(BF16)
    hd = d // MEM_HEADS
    tiles = [pl.ds(r0, sub) for r0 in range(0, tm, sub)]
    y_conv = [_dot(conv_in[r0:r0 + sub], wco_ref[...]) for r0 in range(0, tm, sub)]
    y_mla = [_dot(att_ref[rows, :], wmo_ref[...]) for rows in tiles]
    mixed = [(gates_ref[rows, :d].astype(F32) * yc + gates_ref[rows, d:].astype(F32) * ym).astype(BF16)
             for rows, yc, ym in zip(tiles, y_conv, y_mla)]
    x1 = [x_ref[rows, :] + _dot(mx, wmix_ref[...]) for rows, mx in zip(tiles, mixed)]
    qm = [(_dot((x * gmem_ref[...]).astype(BF16), wmq_ref[...])
           * (lax.rsqrt(jnp.mean(x * x, axis=-1, keepdims=True) + EPS) * (hd ** -0.5))).astype(BF16) for x in x1]
    att_m = []
    for q in qm:
        outs = []
        for h in range(MEM_HEADS):
            sl = slice(h * hd, (h + 1) * hd)
            s = _dot_t(q[:, sl], mk_ref[:, sl])
            p = jnp.exp(s - jnp.max(s, axis=-1, keepdims=True))
            inv_l = 1.0 / jnp.sum(p, axis=-1, keepdims=True)
            outs.append((_dot(p.astype(BF16), mv_ref[:, sl]) * inv_l).astype(BF16))
        att_m.append(jnp.concatenate(outs, axis=-1))
    x2 = [x + _dot(o, wmout_ref[...]) for x, o in zip(x1, att_m)]
    for rows, x in zip(tiles, x2):
        x2_ref[rows, :] = x
        h3 = _rms(x, gmoe_ref[...])
        hp_ref[rows, :] = _pack_bf16_pairs(h3[:, :d // 2], h3[:, d // 2:])
        h_hi = h3.astype(BF16)
        h_lo = (h3 - h_hi.astype(F32)).astype(BF16)
        logits = (_dot_t(wrh_ref[...], h_hi) + _dot_t(wrh_ref[...], h_lo)
                  + _dot_t(wrl_ref[...], h_hi) + _dot_t(wrl_ref[...], h_lo))
        e = jnp.exp(logits - jnp.max(logits, axis=0, keepdims=True))
        aff_ref[0, :, rows] = e / jnp.sum(e, axis=0, keepdims=True)


def _post(x2d, batch, seq, v, gb, gates, att, cw, cb, wco, wmo, wmix, gmem, wmq, mk, mv, wmout, gmoe, wrh, wrl, tm):
    t, d = x2d.shape
    nb = seq // tm
    hb = tm // BF16_SUBLANES
    n_halo = t // BF16_SUBLANES
    tok = lambda i: (i, 0)
    mem_len = mk.shape[0] // batch
    in_specs = [
        pl.BlockSpec((tm, d), tok), pl.BlockSpec((tm, d), tok),
        pl.BlockSpec((BF16_SUBLANES, d), lambda i: (jnp.maximum(i * hb - 1, 0), 0)),
        pl.BlockSpec((BF16_SUBLANES, d), lambda i: (jnp.minimum((i + 1) * hb, n_halo - 1), 0)),
        pl.BlockSpec((tm, d), tok), pl.BlockSpec((tm, 2 * d), tok), pl.BlockSpec((tm, d), tok),
        _const_spec(cw.shape), _const_spec(cb.shape), _const_spec(wco.shape), _const_spec(wmo.shape),
        _const_spec(wmix.shape), _const_spec(gmem.shape), _const_spec(wmq.shape),
        pl.BlockSpec((mem_len, d), lambda i: (i // nb, 0)),
        pl.BlockSpec((mem_len, d), lambda i: (i // nb, 0)),
        _const_spec(wmout.shape), _const_spec(gmoe.shape), _const_spec(wrh.shape), _const_spec(wrl.shape),
    ]
    out_shape = [
        jax.ShapeDtypeStruct((t, d), F32),
        jax.ShapeDtypeStruct((t, d // 2), jnp.uint32),
        jax.ShapeDtypeStruct((batch, N_EXPERTS, seq), F32),
    ]
    out_specs = [
        pl.BlockSpec((tm, d), tok), pl.BlockSpec((tm, d // 2), tok),
        pl.BlockSpec((1, N_EXPERTS, tm), lambda i: (i // nb, 0, i % nb)),
    ]
    return pl.pallas_call(
        functools.partial(_post_kernel, nb, min(tm, POST_SUBTILE)),
        grid=(t // tm,), in_specs=in_specs, out_specs=out_specs, out_shape=out_shape,
        compiler_params=pltpu.CompilerParams(dimension_semantics=("arbitrary",),
                                             vmem_limit_bytes=VMEM_LIMIT_BYTES),
        name="post",
    )(x2d, v, v, v, gb, gates, att, cw, cb, wco, wmo, wmix, gmem, wmq, mk, mv, wmout, gmoe, wrh, wrl)


def _select_kernel(cap, aff_ref, key_ref, offs_ref):
    rows, seq = aff_ref.shape
    n_chunks = seq // LANES
    aff = aff_ref[...]

    def count_ge(x):
        return jnp.sum((aff >= x).astype(F32), axis=-1, keepdims=True)

    def search(b, t):
        cand = t | (jnp.int32(1) << (30 - b))
        return jnp.where(count_ge(pltpu.bitcast(cand, F32)) >= cap, cand, t)

    thr = lax.fori_loop(0, 31, search, jnp.zeros((rows, 1), jnp.int32))

    def refine(_, lohi):
        lo, hi = lohi
        mid = 0.5 * (lo + hi)
        take = count_ge(mid) >= cap
        return jnp.where(take, mid, lo), jnp.where(take, hi, mid)

    lo, hi = lax.fori_loop(0, 30, refine, (pltpu.bitcast(thr, F32), pltpu.bitcast(thr + 1, F32)))
    gt = aff >= hi
    eq = (aff >= lo) & (aff < hi)
    need = cap - jnp.sum(gt.astype(F32), axis=-1, keepdims=True)

    tri = (lax.broadcasted_iota(jnp.int32, (LANES, LANES), 0)
           <= lax.broadcasted_iota(jnp.int32, (LANES, LANES), 1)).astype(BF16)
    ones = jnp.ones((LANES, LANES), BF16)
    lane = lax.broadcasted_iota(jnp.int32, (rows, LANES), 1)
    run_eq = jnp.zeros((rows, LANES), F32)
    run_sel = jnp.zeros((rows, LANES), F32)
    offs = jnp.zeros((rows, LANES), F32)
    for j in range(n_chunks):
        sl = slice(j * LANES, (j + 1) * LANES)
        eq_j = eq[:, sl]
        eq_b = eq_j.astype(F32).astype(BF16)
        eq_rank = _dot(eq_b, tri) + run_eq
        run_eq = run_eq + _dot(eq_b, ones)
        sel_j = gt[:, sl] | (eq_j & (eq_rank <= need))
        sel_b = sel_j.astype(F32).astype(BF16)
        pos = _dot(sel_b, tri) + run_sel
        run_sel = run_sel + _dot(sel_b, ones)
        key_ref[:, sl] = jnp.where(sel_j, pos, 0.0)
        offs = jnp.where(lane == j, run_sel, offs)
    offs_ref[...] = offs.astype(jnp.int32)


def _compact_kernel(cap, n_chunks, offs_ref, key_ref, aff_ref, idx_ref, gsel_ref):
    r = pl.program_id(0)
    n_cblk = cap // LANES
    slot = lax.broadcasted_iota(jnp.int32, (LANES, LANES), 0) + 1
    lane = lax.broadcasted_iota(jnp.int32, (LANES, LANES), 1)
    obase = r * LANES
    j0 = jnp.int32(0)
    for cb in range(n_cblk):
        want = (slot + cb * LANES).astype(F32)
        j0 = lax.while_loop(lambda j: (j < n_chunks - 1) & (offs_ref[obase + j] <= cb * LANES),
                            lambda j: j + 1, j0)
        j1 = lax.while_loop(lambda j: (j < n_chunks - 1) & (offs_ref[obase + j] < (cb + 1) * LANES),
                            lambda j: j + 1, j0)

        def per_chunk(j, carry):
            tok_sel, g_sel = carry
            start = pl.multiple_of(j * LANES, LANES)
            hit = key_ref[0, :, pl.ds(start, LANES)] == want
            tok_sel = jnp.where(hit, (lane + j * LANES).astype(F32), tok_sel)
            g_sel = jnp.where(hit, aff_ref[0, :, pl.ds(start, LANES)], g_sel)
            return tok_sel, g_sel

        zero = jnp.zeros((LANES, LANES), F32)
        tok_sel, g_sel = lax.fori_loop(j0, j1 + 1, per_chunk, (zero, zero))
        idx_ref[0, cb:cb + 1, :] = jnp.sum(tok_sel.T, axis=0, keepdims=True).astype(jnp.int32)
        gsel_ref[0, cb:cb + 1, :] = jnp.sum(g_sel.T, axis=0, keepdims=True)
        j0 = j1


def _route(aff2d, cap):
    rows, seq = aff2d.shape
    n_chunks = seq // LANES
    n_cblk = cap // LANES
    assert n_chunks <= LANES
    key, offs = pl.pallas_call(
        functools.partial(_select_kernel, cap),
        out_shape=[jax.ShapeDtypeStruct((rows, seq), F32), jax.ShapeDtypeStruct((rows, LANES), jnp.int32)],
        compiler_params=pltpu.CompilerParams(vmem_limit_bytes=VMEM_LIMIT_BYTES),
        name="route_select",
    )(aff2d)
    row_blk = lambda r, offs: (r, 0, 0)
    grid_spec = pltpu.PrefetchScalarGridSpec(
        num_scalar_prefetch=1, grid=(rows,),
        in_specs=[pl.BlockSpec((1, 1, seq), row_blk), pl.BlockSpec((1, 1, seq), row_blk)],
        out_specs=[pl.BlockSpec((1, n_cblk, LANES), row_blk), pl.BlockSpec((1, n_cblk, LANES), row_blk)],
    )
    return pl.pallas_call(
        functools.partial(_compact_kernel, cap, n_chunks),
        grid_spec=grid_spec,
        out_shape=[jax.ShapeDtypeStruct((rows, n_cblk, LANES), jnp.int32),
                   jax.ShapeDtypeStruct((rows, n_cblk, LANES), F32)],
        compiler_params=pltpu.CompilerParams(dimension_semantics=("arbitrary",),
                                             vmem_limit_bytes=VMEM_LIMIT_BYTES),
        name="route_compact",
    )(offs.reshape(-1), key.reshape(rows, 1, seq), aff2d.reshape(rows, 1, seq))


def _gather_kernel(cap, idx_ref, hp_ref, xg_ref):
    b = pl.program_id(0)
    e = pl.program_id(1)
    base = (b * N_EXPERTS + e) * cap
    for c in range(cap):
        g, k = divmod(c, SUBLANES)
        xg_ref[0, 0, g, k:k + 1, :] = hp_ref[0, pl.ds(idx_ref[base + c], 1), :]


def _gather(idx_flat, hp, cap):
    batch, seq, half = hp.shape
    grid_spec = pltpu.PrefetchScalarGridSpec(
        num_scalar_prefetch=1,
        grid=(batch, N_EXPERTS),
        in_specs=[pl.BlockSpec((1, seq, half), lambda b, e, idx: (b, 0, 0), pipeline_mode=pl.Buffered(1))],
        out_specs=pl.BlockSpec((1, 1, cap // SUBLANES, SUBLANES, half), lambda b, e, idx: (b, e, 0, 0, 0)),
    )
    xg = pl.pallas_call(
        functools.partial(_gather_kernel, cap),
        grid_spec=grid_spec,
        out_shape=jax.ShapeDtypeStruct((batch, N_EXPERTS, cap // SUBLANES, SUBLANES, half), hp.dtype),
        compiler_params=pltpu.CompilerParams(dimension_semantics=("arbitrary", "arbitrary"),
                                             vmem_limit_bytes=VMEM_LIMIT_BYTES),
        name="gather",
    )(idx_flat, hp)
    return xg.reshape(batch, N_EXPERTS, cap, half)


def _experts_kernel(cap, tc, apply_norm, out_rows, idx_ref, gsel_ref, xg_ref, wg_ref, wu_ref, wd_ref, x2_ref, gf_ref,
                    o_hbm, acc_ref, y_ref, sem):
    b = pl.program_id(0)
    e = pl.program_id(1)
    base = (b * N_EXPERTS + e) * cap
    half = xg_ref.shape[3]
    n_chunks = cap // tc

    @pl.when(e == 0)
    def _():
        acc_ref[...] = jnp.zeros_like(acc_ref)

    slab = x2_ref.shape[0]
    slab_rows = pl.ds(pl.multiple_of(e * slab, slab), slab)
    acc_ref[slab_rows, :] = acc_ref[slab_rows, :] + x2_ref[...]

    def ffn(ci):
        words = xg_ref[0, 0, ci * tc:(ci + 1) * tc, :]
        x_lo = _unpack_bf16_pair(words, 0)
        x_hi = _unpack_bf16_pair(words, 1)
        gate = _dot(x_lo, wg_ref[0, :half, :]) + _dot(x_hi, wg_ref[0, half:, :])
        up = _dot(x_lo, wu_ref[0, :half, :]) + _dot(x_hi, wu_ref[0, half:, :])
        act = (gate * jax.nn.sigmoid(gate) * up).astype(BF16)
        y_ref[ci % 2] = _dot(act, wd_ref[0])

    def scatter(ci, first, enabled=None):
        for r0 in range(0, tc, SUBLANES):
            cs = [first + ci * tc + r0 + k for k in range(SUBLANES)]
            toks = [idx_ref[c] for c in cs]
            adds = [gsel_ref[c] * y_ref[ci % 2, r0 + k:r0 + k + 1, :] for k, c in enumerate(cs)]
            if enabled is not None:
                adds = [jnp.where(enabled, a, 0.0) for a in adds]
            rows = [acc_ref[pl.ds(tok, 1), :] + a for tok, a in zip(toks, adds)]
            for tok, row in zip(toks, rows):
                acc_ref[pl.ds(tok, 1), :] = row

    defer = n_chunks % 2 == 0
    if defer:
        @pl.when((b == 0) & (e == 0))
        def _():
            y_ref[...] = jnp.zeros_like(y_ref)

    for ci in range(n_chunks):
        ffn(ci)
        if ci > 0:
            scatter(ci - 1, base)
        elif defer:
            scatter(n_chunks - 1, jnp.maximum(base - cap, 0), e > 0)
    if not defer:
        scatter(n_chunks - 1, base)

    @pl.when(e == N_EXPERTS - 1)
    def _():
        if defer:
            scatter(n_chunks - 1, base)
        n_tiles = acc_ref.shape[0] // out_rows

        def out_copy(i):
            rows = pl.ds(pl.multiple_of(i * out_rows, out_rows), out_rows)
            return rows, pltpu.make_async_copy(acc_ref.at[rows], o_hbm.at[b, rows], sem)

        def emit(i, _):
            rows, copy = out_copy(i)
            if apply_norm:
                acc_ref[rows, :] = _rms(acc_ref[rows, :], gf_ref[...])
            copy.start()
            return 0

        def drain(i, _):
            out_copy(i)[1].wait()
            return 0

        lax.fori_loop(0, n_tiles, emit, 0)
        lax.fori_loop(0, n_tiles, drain, 0)


def _experts(idx_flat, gsel_flat, xg, wg, wu, wd, x2, gf, apply_norm, seq, tc):
    batch, _, cap, half = xg.shape
    d = 2 * half
    ff = wg.shape[2]
    slab = seq // N_EXPERTS
    grid_spec = pltpu.PrefetchScalarGridSpec(
        num_scalar_prefetch=2,
        grid=(batch, N_EXPERTS),
        in_specs=[
            pl.BlockSpec((1, 1, cap, half), lambda b, e, idx, gs: (b, e, 0, 0)),
            pl.BlockSpec((1, d, ff), lambda b, e, idx, gs: (e, 0, 0)),
            pl.BlockSpec((1, d, ff), lambda b, e, idx, gs: (e, 0, 0)),
            pl.BlockSpec((1, ff, d), lambda b, e, idx, gs: (e, 0, 0)),
            pl.BlockSpec((slab, d), lambda b, e, idx, gs: (b * N_EXPERTS + e, 0)),
            pl.BlockSpec(gf.shape, lambda b, e, idx, gs: (0, 0)),
        ],
        out_specs=pl.BlockSpec(memory_space=pl.ANY),
        scratch_shapes=[pltpu.VMEM((seq, d), F32), pltpu.VMEM((2, tc, d), F32), pltpu.SemaphoreType.DMA],
    )
    return pl.pallas_call(
        functools.partial(_experts_kernel, cap, tc, apply_norm, min(seq, OUT_TILE_ROWS)),
        grid_spec=grid_spec,
        out_shape=jax.ShapeDtypeStruct((batch, seq, d), F32),
        compiler_params=pltpu.CompilerParams(dimension_semantics=("arbitrary", "arbitrary"),
                                             vmem_limit_bytes=EXPERTS_VMEM_LIMIT_BYTES),
        name="experts",
    )(idx_flat, gsel_flat, xg, wg, wu, wd, x2, gf)


def _rot_cols(w):
    half = w.shape[-1] // 2
    return jnp.concatenate([-w[..., half:], w[..., :half]], axis=-1)


def _tile(n, pref):
    return pref if n % pref == 0 else n


def kernel(x, mem, norm_mix_g, w_in, conv_w, conv_b, w_conv_out, q_norm_g, w_uq, kv_norm_g, w_ukv,
           w_mla_out, b_gate, w_mix_out, norm_mem_g, norm_memkv_g, w_mem_q, w_mem_kv, w_mem_out,
           norm_moe_g, w_router, w_exp_gate, w_exp_up, w_exp_down, norm_final_g):
    batch, seq, d = x.shape
    depth = w_in.shape[0]
    t = batch * seq
    q_lora = q_norm_g.shape[1]
    kv_lora = kv_norm_g.shape[1]
    cap = max(1, CAPACITY_FACTOR * seq // N_EXPERTS)
    assert cap % LANES == 0 and seq % LANES == 0 and d % (2 * LANES) == 0

    tm_in = _tile(seq, 512)
    tm_post = _tile(seq, 512)
    tq = _tile(seq, 1024)
    tk = _tile(seq, 512)
    tc = _tile(cap, 256)

    inv = 1.0 / (ROPE_THETA ** (np.arange(0, QK_ROPE, 2, dtype=np.float64) / QK_ROPE))
    ang = np.arange(seq, dtype=np.float64)[:, None] * inv[None, :]
    cos2 = np.concatenate([np.cos(ang), np.cos(ang)], axis=-1)
    sin2 = np.concatenate([np.sin(ang), np.sin(ang)], axis=-1)
    qf = (QK_DIM ** -0.5) * math.log2(math.e)
    kc, ks = jnp.asarray(cos2, F32), jnp.asarray(sin2, F32)
    qc, qs = jnp.asarray(qf * cos2.T, F32), jnp.asarray(qf * sin2.T, F32)

    o_cq = 3 * d
    o_ckv = o_cq + q_lora
    o_kr = o_ckv + kv_lora
    o_gl = o_kr + QK_ROPE
    cols = {"xc": 0, "gb": d, "gc": 2 * d, "cq": o_cq, "ckv": o_ckv}

    x2d = x.reshape(t, d)
    mem2d = mem.reshape(-1, d)
    row = lambda a: a.reshape(1, -1)

    for l in range(depth):
        win = w_in[l].astype(BF16)
        wgl = win[:, o_gl:]
        wkr = jnp.concatenate([win[:, o_kr:o_gl], _rot_cols(win[:, o_kr:o_gl])], axis=1)
        wq3 = w_uq[l].reshape(q_lora, MLA_HEADS, QK_DIM)
        wq = wq3.transpose(1, 2, 0).astype(BF16)
        wqr = _rot_cols(wq3[..., QK_NOPE:]).transpose(1, 2, 0).astype(BF16)
        wkv3 = w_ukv[l].reshape(kv_lora, MLA_HEADS, QK_NOPE + V_HEAD)
        wk = wkv3[..., :QK_NOPE].transpose(1, 0, 2).astype(BF16)
        wv = wkv3[..., QK_NOPE:].transpose(1, 2, 0).astype(BF16)

        wr_hi = w_router[l].T.astype(BF16)
        wr_lo = (w_router[l].T - wr_hi.astype(F32)).astype(BF16)

        v, gb, gates, qt, k, vt = _inproj(
            x2d, batch, seq, row(norm_mix_g[l]), win, wgl, wkr, cols, row(q_norm_g[l]), row(kv_norm_g[l]),
            row(b_gate[l]), wq, wqr, wk, wv, qc, qs, kc, ks, qf, tm_in)
        w_exp = [w_exp_gate[l], w_exp_up[l], w_exp_down[l]]
        w_flat = [w.reshape(-1, w.shape[-1]) for w in w_exp]
        n_attn_steps = batch * MLA_HEADS * (seq // tq)
        fused_cast = all(_side_cast_fits(w, n_attn_steps) for w in w_flat)
        att, w_bf = _attention(qt, k, vt, tq, tk, w_flat if fused_cast else [])
        att = att.reshape(t, MLA_HEADS * V_HEAD)
        wgb, wub, wdb = ([wb.reshape(w.shape) for wb, w in zip(w_bf, w_exp)] if fused_cast
                         else [w.astype(BF16) for w in w_exp])
        mk, mv = _memkv(mem2d, row(norm_memkv_g[l]), w_mem_kv[l].astype(BF16))
        x2, hp, aff = _post(
            x2d, batch, seq, v, gb, gates, att, conv_w[l], row(conv_b[l]), w_conv_out[l].astype(BF16),
            w_mla_out[l].astype(BF16), w_mix_out[l].astype(BF16), row(norm_mem_g[l]),
            w_mem_q[l].astype(BF16), mk, mv, w_mem_out[l].astype(BF16), row(norm_moe_g[l]),
            wr_hi, wr_lo, tm_post)
        idx, gsel = _route(aff.reshape(batch * N_EXPERTS, seq), cap)
        idx_flat = idx.reshape(-1)
        xg = _gather(idx_flat, hp.reshape(batch, seq, d // 2), cap)
        x2d = _experts(idx_flat, gsel.reshape(-1), xg, wgb, wub, wdb, x2, row(norm_final_g),
                       l == depth - 1, seq, tc).reshape(t, d)
    return x2d.reshape(batch, seq, d)
```

```python
import functools
import math

import jax
import jax.numpy as jnp
import numpy as np
from jax import lax
from jax.experimental import pallas as pl
from jax.experimental.pallas import tpu as pltpu

MLA_HEADS = 8
QK_NOPE = 128
QK_ROPE = 64
QK_DIM = QK_NOPE + QK_ROPE
V_HEAD = 128
V_AUG = V_HEAD + 16
ROPE_THETA = 10000.0
MEM_HEADS = 4
N_EXPERTS = 16
CAPACITY_FACTOR = 2
CONV_WIDTH = 3
EPS = 1e-6

LANES = 128
SUBLANES = 8
BF16_SUBLANES = 16
VMEM_LIMIT_BYTES = 56 * 1024 * 1024
EXPERTS_VMEM_LIMIT_BYTES = 60 * 1024 * 1024
OUT_TILE_ROWS = 256
SIDE_CAST_SLICE_BYTES = 1024 * 1024
POST_SUBTILE = 256
ATTN_SCORE_SLOTS = 3
ATTN_BODY_ROUNDS = 2

F32 = jnp.float32
BF16 = jnp.bfloat16


def _const_spec(shape):
    nd = len(shape)
    return pl.BlockSpec(shape, lambda *_: (0,) * nd, pipeline_mode=pl.Buffered(1))


def _rms(x, g):
    return x * lax.rsqrt(jnp.mean(x * x, axis=-1, keepdims=True) + EPS) * g


def _dot(a, b):
    return jnp.dot(a, b, preferred_element_type=F32)


def _dot_t(a, b):
    return lax.dot_general(a, b, (((1,), (1,)), ((), ())), preferred_element_type=F32)


def _inproj_kernel(cols, q_scale, x_ref, g_ref, win_ref, wgl_ref, wkr_ref, qg_ref, kvg_ref, bg_ref,
                   wq_ref, wqr_ref, wk_ref, wv_ref, qc_ref, qs_ref, kc_ref, ks_ref,
                   v_ref, gb_ref, gates_ref, qt_ref, k_ref, vt_ref):
    d = x_ref.shape[1]
    hb = _rms(x_ref[...], g_ref[...]).astype(BF16)

    def proj(name, width):
        lo = cols[name]
        return _dot(hb, win_ref[:, lo:lo + width])

    v_ref[...] = (proj("gc", d) * proj("xc", d)).astype(BF16)
    gb_ref[...] = proj("gb", d).astype(BF16)
    gates_ref[...] = jax.nn.sigmoid(_dot(hb, wgl_ref[...]) + bg_ref[...]).astype(BF16)

    q_lora = qg_ref.shape[1]
    kv_lora = kvg_ref.shape[1]
    qn = _rms(proj("cq", q_lora), qg_ref[...]).astype(BF16)
    kvn = _rms(proj("ckv", kv_lora), kvg_ref[...]).astype(BF16)
    kr2 = _dot(hb, wkr_ref[...])
    k_rope = kr2[:, :QK_ROPE] * kc_ref[...] + kr2[:, QK_ROPE:] * ks_ref[...]
    qc = qc_ref[...]
    qs = qs_ref[...]
    pad_row = lax.broadcasted_iota(jnp.int32, (V_AUG - V_HEAD, x_ref.shape[0]), 0)
    ones_rows = jnp.where(pad_row == 0, 1.0, 0.0).astype(BF16)
    for h in range(MLA_HEADS):
        qt_h = _dot_t(wq_ref[h], qn)
        rope = qt_h[QK_NOPE:] * qc + _dot_t(wqr_ref[h], qn) * qs
        qt_ref[0, h] = jnp.concatenate([qt_h[:QK_NOPE] * q_scale, rope], axis=0).astype(BF16)
        k_ref[0, h] = jnp.concatenate([_dot(kvn, wk_ref[h]), k_rope], axis=-1).astype(BF16)
        vt_ref[0, h, :V_HEAD, :] = _dot_t(wv_ref[h], kvn).astype(BF16)
        vt_ref[0, h, V_HEAD:, :] = ones_rows


def _inproj(x2d, batch, seq, g, win, wgl, wkr, cols, qg, kvg, bg, wq, wqr, wk, wv, qc, qs, kc, ks, q_scale, tm):
    t, d = x2d.shape
    nb = seq // tm
    tok = lambda i: (i, 0)
    pos = lambda i: (i % nb, 0)
    pos_t = lambda i: (0, i % nb)
    head_blk = lambda i: (i // nb, 0, i % nb, 0)
    head_blk_t = lambda i: (i // nb, 0, 0, i % nb)
    in_specs = [
        pl.BlockSpec((tm, d), tok),
        _const_spec(g.shape), _const_spec(win.shape), _const_spec(wgl.shape), _const_spec(wkr.shape),
        _const_spec(qg.shape), _const_spec(kvg.shape),
        _const_spec(bg.shape), _const_spec(wq.shape), _const_spec(wqr.shape), _const_spec(wk.shape),
        _const_spec(wv.shape),
        pl.BlockSpec((QK_ROPE, tm), pos_t), pl.BlockSpec((QK_ROPE, tm), pos_t),
        pl.BlockSpec((tm, QK_ROPE), pos), pl.BlockSpec((tm, QK_ROPE), pos),
    ]
    out_shape = [
        jax.ShapeDtypeStruct((t, d), BF16),
        jax.ShapeDtypeStruct((t, d), BF16),
        jax.ShapeDtypeStruct((t, 2 * d), BF16),
        jax.ShapeDtypeStruct((batch, MLA_HEADS, QK_DIM, seq), BF16),
        jax.ShapeDtypeStruct((batch, MLA_HEADS, seq, QK_DIM), BF16),
        jax.ShapeDtypeStruct((batch, MLA_HEADS, V_AUG, seq), BF16),
    ]
    out_specs = [
        pl.BlockSpec((tm, d), tok), pl.BlockSpec((tm, d), tok), pl.BlockSpec((tm, 2 * d), tok),
        pl.BlockSpec((1, MLA_HEADS, QK_DIM, tm), head_blk_t),
        pl.BlockSpec((1, MLA_HEADS, tm, QK_DIM), head_blk),
        pl.BlockSpec((1, MLA_HEADS, V_AUG, tm), head_blk_t),
    ]
    return pl.pallas_call(
        functools.partial(_inproj_kernel, cols, q_scale),
        grid=(t // tm,), in_specs=in_specs, out_specs=out_specs, out_shape=out_shape,
        compiler_params=pltpu.CompilerParams(dimension_semantics=("arbitrary",),
                                             vmem_limit_bytes=VMEM_LIMIT_BYTES),
        name="inproj",
    )(x2d, g, win, wgl, wkr, qg, kvg, bg, wq, wqr, wk, wv, qc, qs, kc, ks)


def _attn_kernel(tk, n_slots, n_cast, qt_ref, k_ref, vt_ref, *refs):
    cast_in, o_ref, cast_out, s_ref = refs[:n_cast], refs[n_cast], refs[n_cast + 1:2 * n_cast + 1], refs[-1]
    for src, dst in zip(cast_in, cast_out):
        dst[...] = src[...].astype(BF16)
    qt = qt_ref[0, 0]
    tq = qt.shape[1]
    nk = k_ref.shape[2] // tk

    def scores(j, slot):
        start = pl.multiple_of(j * tk, tk)
        s = _dot(k_ref[0, 0, pl.ds(start, tk), :], qt)
        s_ref[slot] = s
        return jnp.max(s, axis=0, keepdims=True)

    def update(j, slot, mx, m, acc):
        start = pl.multiple_of(j * tk, tk)
        m_new = jnp.maximum(m, mx)
        p = jnp.exp2(s_ref[slot] - m_new)
        acc = jnp.exp2(m - m_new) * acc + _dot(vt_ref[0, 0, :, pl.ds(start, tk)], p.astype(BF16))
        return m_new, acc

    ahead = n_slots - 1

    def chunks(j0, count, pending, m, acc):
        pending = list(pending)
        for u in range(count):
            if isinstance(j0, int) and j0 + u + ahead >= nk:
                pass
            else:
                pending.append(scores(j0 + u + ahead, (u + ahead) % n_slots))
            m, acc = update(j0 + u, u % n_slots, pending.pop(0), m, acc)
        return tuple(pending), m, acc

    body = n_slots * ATTN_BODY_ROUNDS
    n_iter = max(nk - ahead, 0) // body
    pending = tuple(scores(j, j) for j in range(min(ahead, nk)))
    m0 = jnp.full((1, tq), -1e30, F32)
    acc0 = jnp.zeros((vt_ref.shape[2], tq), F32)
    pending, m, acc = lax.fori_loop(0, n_iter, lambda i, c: chunks(body * i, body, *c),
                                    (pending, m0, acc0))
    _, _, acc = chunks(body * n_iter, nk - body * n_iter, pending, m, acc)
    o_ref[0] = (acc[:V_HEAD] / acc[V_HEAD:V_HEAD + 1]).T.astype(BF16)


def _side_cast_fits(w, n_steps):
    rows, cols = w.shape
    slice_rows = rows // n_steps
    return (rows % n_steps == 0 and slice_rows % BF16_SUBLANES == 0
            and slice_rows * cols * 4 <= SIDE_CAST_SLICE_BYTES)


def _attention(qt, k, vt, tq, tk, side_casts):
    batch, heads, seq, _ = k.shape
    nk = seq // tk
    nq = seq // tq
    n_steps = batch * heads * nq
    step = lambda b, h, i: ((b * heads + h) * nq + i, 0)
    cast_specs = [pl.BlockSpec((w.shape[0] // n_steps, w.shape[1]), step) for w in side_casts]
    outs = pl.pallas_call(
        functools.partial(_attn_kernel, tk, ATTN_SCORE_SLOTS, len(side_casts)),
        grid=(batch, heads, nq),
        in_specs=[
            pl.BlockSpec((1, 1, QK_DIM, tq), lambda b, h, i: (b, h, 0, i)),
            pl.BlockSpec((1, 1, seq, QK_DIM), lambda b, h, i: (b, h, 0, 0)),
            pl.BlockSpec((1, 1, V_AUG, seq), lambda b, h, i: (b, h, 0, 0)),
        ] + cast_specs,
        out_specs=[pl.BlockSpec((1, tq, V_HEAD), lambda b, h, i: (b, i, h))] + cast_specs,
        out_shape=[jax.ShapeDtypeStruct((batch, seq, heads * V_HEAD), BF16)]
                  + [jax.ShapeDtypeStruct(w.shape, BF16) for w in side_casts],
        scratch_shapes=[pltpu.VMEM((ATTN_SCORE_SLOTS, tk, tq), F32)],
        compiler_params=pltpu.CompilerParams(
            dimension_semantics=("arbitrary", "arbitrary", "arbitrary"),
            vmem_limit_bytes=VMEM_LIMIT_BYTES),
        name="mla_attention",
    )(qt, k, vt, *side_casts)
    return outs[0], outs[1:]


def _memkv_kernel(m_ref, g_ref, w_ref, k_ref, v_ref):
    d = m_ref.shape[1]
    kv = _dot(_rms(m_ref[...], g_ref[...]).astype(BF16), w_ref[...])
    k_ref[...] = kv[:, :d].astype(BF16)
    v_ref[...] = kv[:, d:].astype(BF16)


def _memkv(mem2d, g, w):
    n, d = mem2d.shape
    return pl.pallas_call(
        _memkv_kernel,
        out_shape=[jax.ShapeDtypeStruct((n, d), BF16), jax.ShapeDtypeStruct((n, d), BF16)],
        compiler_params=pltpu.CompilerParams(vmem_limit_bytes=VMEM_LIMIT_BYTES),
        name="memkv",
    )(mem2d, g, w)


def _pack_bf16_pairs(lo, hi):
    return pltpu.bitcast(pltpu.pack_elementwise([lo, hi], packed_dtype=BF16), jnp.uint32)


def _unpack_bf16_pair(words, index):
    return pltpu.unpack_elementwise(words, index=index, packed_dtype=BF16, unpacked_dtype=F32).astype(BF16)


def _post_kernel(nb, sub, x_ref, v_ref, vprev_ref, vnext_ref, gb_ref, gates_ref, att_ref,
                 cw_ref, cb_ref, wco_ref, wmo_ref, wmix_ref, gmem_ref, wmq_ref, mk_ref, mv_ref, wmout_ref,
                 gmoe_ref, wrh_ref, wrl_ref,
                 x2_ref, hp_ref, aff_ref):
    i = pl.program_id(0)
    tm, d = x_ref.shape
    v = v_ref[...].astype(F32)
    first = (i % nb) == 0
    last = (i % nb) == nb - 1
    halo_prev = jnp.where(first, 0.0, vprev_ref[...].astype(F32)[BF16_SUBLANES - 1:BF16_SUBLANES, :])
    halo_next = jnp.where(last, 0.0, vnext_ref[...].astype(F32)[0:1, :])
    row = lax.broadcasted_iota(jnp.int32, (tm, d), 0)
    v_prev = jnp.where(row == 0, halo_prev, pltpu.roll(v, 1, 0))
    v_next = jnp.where(row == tm - 1, halo_next, pltpu.roll(v, tm - 1, 0))
    conv = cw_ref[0:1, :] * v_prev + cw_ref[1:2, :] * v + cw_ref[2:3, :] * v_next + cb_ref[...]
    conv_in = (gb_ref[...].astype(F32) * conv).astype(BF16)
    hd = d // MEM_HEADS
    tiles = [pl.ds(r0, sub) for r0 in range(0, tm, sub)]
    y_conv = [_dot(conv_in[r0:r0 + sub], wco_ref[...]) for r0 in range(0, tm, sub)]
    y_mla = [_dot(att_ref[rows, :], wmo_ref[...]) for rows in tiles]
    mixed = [(gates_ref[rows, :d].astype(F32) * yc + gates_ref[rows, d:].astype(F32) * ym).astype(BF16)
             for rows, yc, ym in zip(tiles, y_conv, y_mla)]
    x1 = [x_ref[rows, :] + _dot(mx, wmix_ref[...]) for rows, mx in zip(tiles, mixed)]
    qm = [(_dot((x * gmem_ref[...]).astype(BF16), wmq_ref[...])
           * (lax.rsqrt(jnp.mean(x * x, axis=-1, keepdims=True) + EPS) * (hd ** -0.5))).astype(BF16) for x in x1]
    att_m = []
    for q in qm:
        outs = []
        for h in range(MEM_HEADS):
            sl = slice(h * hd, (h + 1) * hd)
            s = _dot_t(q[:, sl], mk_ref[:, sl])
            p = jnp.exp(s - jnp.max(s, axis=-1, keepdims=True))
            inv_l = 1.0 / jnp.sum(p, axis=-1, keepdims=True)
            outs.append((_dot(p.astype(BF16), mv_ref[:, sl]) * inv_l).astype(BF16))
        att_m.append(jnp.concatenate(outs, axis=-1))
    x2 = [x + _dot(o, wmout_ref[...]) for x, o in zip(x1, att_m)]
    for rows, x in zip(tiles, x2):
        x2_ref[rows, :] = x
        h3 = _rms(x, gmoe_ref[...])
        hp_ref[rows, :] = _pack_bf16_pairs(h3[:, :d // 2], h3[:, d // 2:])
        h_hi = h3.astype(BF16)
        h_lo = (h3 - h_hi.astype(F32)).astype(BF16)
        logits = (_dot_t(wrh_ref[...], h_hi) + _dot_t(wrh_ref[...], h_lo)
                  + _dot_t(wrl_ref[...], h_hi) + _dot_t(wrl_ref[...], h_lo))
        e = jnp.exp(logits - jnp.max(logits, axis=0, keepdims=True))
        aff_ref[0, :, rows] = e / jnp.sum(e, axis=0, keepdims=True)


def _post(x2d, batch, seq, v, gb, gates, att, cw, cb, wco, wmo, wmix, gmem, wmq, mk, mv, wmout, gmoe, wrh, wrl, tm):
    t, d = x2d.shape
    nb = seq // tm
    hb = tm // BF16_SUBLANES
    n_halo = t // BF16_SUBLANES
    tok = lambda i: (i, 0)
    mem_len = mk.shape[0] // batch
    in_specs = [
        pl.BlockSpec((tm, d), tok), pl.BlockSpec((tm, d), tok),
        pl.BlockSpec((BF16_SUBLANES, d), lambda i: (jnp.maximum(i * hb - 1, 0), 0)),
        pl.BlockSpec((BF16_SUBLANES, d), lambda i: (jnp.minimum((i + 1) * hb, n_halo - 1), 0)),
        pl.BlockSpec((tm, d), tok), pl.BlockSpec((tm, 2 * d), tok), pl.BlockSpec((tm, d), tok),
        _const_spec(cw.shape), _const_spec(cb.shape), _const_spec(wco.shape), _const_spec(wmo.shape),
        _const_spec(wmix.shape), _const_spec(gmem.shape), _const_spec(wmq.shape),
        pl.BlockSpec((mem_len, d), lambda i: (i // nb, 0)),
        pl.BlockSpec((mem_len, d), lambda i: (i // nb, 0)),
        _const_spec(wmout.shape), _const_spec(gmoe.shape), _const_spec(wrh.shape), _const_spec(wrl.shape),
    ]
    out_shape = [
        jax.ShapeDtypeStruct((t, d), F32),
        jax.ShapeDtypeStruct((t, d // 2), jnp.uint32),
        jax.ShapeDtypeStruct((batch, N_EXPERTS, seq), F32),
    ]
    out_specs = [
        pl.BlockSpec((tm, d), tok), pl.BlockSpec((tm, d // 2), tok),
        pl.BlockSpec((1, N_EXPERTS, tm), lambda i: (i // nb, 0, i % nb)),
    ]
    return pl.pallas_call(
        functools.partial(_post_kernel, nb, min(tm, POST_SUBTILE)),
        grid=(t // tm,), in_specs=in_specs, out_specs=out_specs, out_shape=out_shape,
        compiler_params=pltpu.CompilerParams(dimension_semantics=("arbitrary",),
                                             vmem_limit_bytes=VMEM_LIMIT_BYTES),
        name="post",
    )(x2d, v, v, v, gb, gates, att, cw, cb, wco, wmo, wmix, gmem, wmq, mk, mv, wmout, gmoe, wrh, wrl)


def _select_kernel(cap, aff_ref, key_ref, offs_ref):
    rows, seq = aff_ref.shape
    n_chunks = seq // LANES
    aff = aff_ref[...]

    def count_ge(x):
        return jnp.sum((aff >= x).astype(F32), axis=-1, keepdims=True)

    def search(b, t):
        cand = t | (jnp.int32(1) << (30 - b))
        return jnp.where(count_ge(pltpu.bitcast(cand, F32)) >= cap, cand, t)

    thr = lax.fori_loop(0, 31, search, jnp.zeros((rows, 1), jnp.int32))

    def refine(_, lohi):
        lo, hi = lohi
        mid = 0.5 * (lo + hi)
        take = count_ge(mid) >= cap
        return jnp.where(take, mid, lo), jnp.where(take, hi, mid)

    lo, hi = lax.fori_loop(0, 30, refine, (pltpu.bitcast(thr, F32), pltpu.bitcast(thr + 1, F32)))
    gt = aff >= hi
    eq = (aff >= lo) & (aff < hi)
    need = cap - jnp.sum(gt.astype(F32), axis=-1, keepdims=True)

    tri = (lax.broadcasted_iota(jnp.int32, (LANES, LANES), 0)
           <= lax.broadcasted_iota(jnp.int32, (LANES, LANES), 1)).astype(BF16)
    ones = jnp.ones((LANES, LANES), BF16)
    lane = lax.broadcasted_iota(jnp.int32, (rows, LANES), 1)
    run_eq = jnp.zeros((rows, LANES), F32)
    run_sel = jnp.zeros((rows, LANES), F32)
    offs = jnp.zeros((rows, LANES), F32)
    for j in range(n_chunks):
        sl = slice(j * LANES, (j + 1) * LANES)
        eq_j = eq[:, sl]
        eq_b = eq_j.astype(F32).astype(BF16)
        eq_rank = _dot(eq_b, tri) + run_eq
        run_eq = run_eq + _dot(eq_b, ones)
        sel_j = gt[:, sl] | (eq_j & (eq_rank <= need))
        sel_b = sel_j.astype(F32).astype(BF16)
        pos = _dot(sel_b, tri) + run_sel
        run_sel = run_sel + _dot(sel_b, ones)
        key_ref[:, sl] = jnp.where(sel_j, pos, 0.0)
        offs = jnp.where(lane == j, run_sel, offs)
    offs_ref[...] = offs.astype(jnp.int32)


def _compact_kernel(cap, n_chunks, offs_ref, key_ref, aff_ref, idx_ref, gsel_ref):
    r = pl.program_id(0)
    n_cblk = cap // LANES
    slot = lax.broadcasted_iota(jnp.int32, (LANES, LANES), 0) + 1
    lane = lax.broadcasted_iota(jnp.int32, (LANES, LANES), 1)
    obase = r * LANES
    j0 = jnp.int32(0)
    for cb in range(n_cblk):
        want = (slot + cb * LANES).astype(F32)
        j0 = lax.while_loop(lambda j: (j < n_chunks - 1) & (offs_ref[obase + j] <= cb * LANES),
                            lambda j: j + 1, j0)
        j1 = lax.while_loop(lambda j: (j < n_chunks - 1) & (offs_ref[obase + j] < (cb + 1) * LANES),
                            lambda j: j + 1, j0)

        def per_chunk(j, carry):
            tok_sel, g_sel = carry
            start = pl.multiple_of(j * LANES, LANES)
            hit = key_ref[0, :, pl.ds(start, LANES)] == want
            tok_sel = jnp.where(hit, (lane + j * LANES).astype(F32), tok_sel)
            g_sel = jnp.where(hit, aff_ref[0, :, pl.ds(start, LANES)], g_sel)
            return tok_sel, g_sel

        zero = jnp.zeros((LANES, LANES), F32)
        tok_sel, g_sel = lax.fori_loop(j0, j1 + 1, per_chunk, (zero, zero))
        idx_ref[0, cb:cb + 1, :] = jnp.sum(tok_sel.T, axis=0, keepdims=True).astype(jnp.int32)
        gsel_ref[0, cb:cb + 1, :] = jnp.sum(g_sel.T, axis=0, keepdims=True)
        j0 = j1


def _route(aff2d, cap):
    rows, seq = aff2d.shape
    n_chunks = seq // LANES
    n_cblk = cap // LANES
    assert n_chunks <= LANES
    key, offs = pl.pallas_call(
        functools.partial(_select_kernel, cap),
        out_shape=[jax.ShapeDtypeStruct((rows, seq), F32), jax.ShapeDtypeStruct((rows, LANES), jnp.int32)],
        compiler_params=pltpu.CompilerParams(vmem_limit_bytes=VMEM_LIMIT_BYTES),
        name="route_select",
    )(aff2d)
    row_blk = lambda r, offs: (r, 0, 0)
    grid_spec = pltpu.PrefetchScalarGridSpec(
        num_scalar_prefetch=1, grid=(rows,),
        in_specs=[pl.BlockSpec((1, 1, seq), row_blk), pl.BlockSpec((1, 1, seq), row_blk)],
        out_specs=[pl.BlockSpec((1, n_cblk, LANES), row_blk), pl.BlockSpec((1, n_cblk, LANES), row_blk)],
    )
    return pl.pallas_call(
        functools.partial(_compact_kernel, cap, n_chunks),
        grid_spec=grid_spec,
        out_shape=[jax.ShapeDtypeStruct((rows, n_cblk, LANES), jnp.int32),
                   jax.ShapeDtypeStruct((rows, n_cblk, LANES), F32)],
        compiler_params=pltpu.CompilerParams(dimension_semantics=("arbitrary",),
                                             vmem_limit_bytes=VMEM_LIMIT_BYTES),
        name="route_compact",
    )(offs.reshape(-1), key.reshape(rows, 1, seq), aff2d.reshape(rows, 1, seq))


def _gather_kernel(cap, idx_ref, hp_ref, xg_ref):
    b = pl.program_id(0)
    e = pl.program_id(1)
    base = (b * N_EXPERTS + e) * cap
    for c in range(cap):
        g, k = divmod(c, SUBLANES)
        xg_ref[0, 0, g, k:k + 1, :] = hp_ref[0, pl.ds(idx_ref[base + c], 1), :]


def _gather(idx_flat, hp, cap):
    batch, seq, half = hp.shape
    grid_spec = pltpu.PrefetchScalarGridSpec(
        num_scalar_prefetch=1,
        grid=(batch, N_EXPERTS),
        in_specs=[pl.BlockSpec((1, seq, half), lambda b, e, idx: (b, 0, 0), pipeline_mode=pl.Buffered(1))],
        out_specs=pl.BlockSpec((1, 1, cap // SUBLANES, SUBLANES, half), lambda b, e, idx: (b, e, 0, 0, 0)),
    )
    xg = pl.pallas_call(
        functools.partial(_gather_kernel, cap),
        grid_spec=grid_spec,
        out_shape=jax.ShapeDtypeStruct((batch, N_EXPERTS, cap // SUBLANES, SUBLANES, half), hp.dtype),
        compiler_params=pltpu.CompilerParams(dimension_semantics=("arbitrary", "arbitrary"),
                                             vmem_limit_bytes=VMEM_LIMIT_BYTES),
        name="gather",
    )(idx_flat, hp)
    return xg.reshape(batch, N_EXPERTS, cap, half)


def _experts_kernel(cap, tc, apply_norm, out_rows, idx_ref, gsel_ref, xg_ref, wg_ref, wu_ref, wd_ref, x2_ref, gf_ref,
                    o_hbm, acc_ref, y_ref, sem):
    b = pl.program_id(0)
    e = pl.program_id(1)
    base = (b * N_EXPERTS + e) * cap
    n_chunks = cap // tc

    @pl.when(e == 0)
    def _():
        acc_ref[...] = jnp.zeros_like(acc_ref)

    slab = x2_ref.shape[0]
    slab_rows = pl.ds(pl.multiple_of(e * slab, slab), slab)
    acc_ref[slab_rows, :] = acc_ref[slab_rows, :] + x2_ref[...]

    def ffn(ci):
        words = xg_ref[0, 0, ci * tc:(ci + 1) * tc, :]
        x = jnp.concatenate([_unpack_bf16_pair(words, 0), _unpack_bf16_pair(words, 1)], axis=-1)
        gate = _dot(x, wg_ref[0])
        up = _dot(x, wu_ref[0])
        act = (gate * jax.nn.sigmoid(gate) * up).astype(BF16)
        y_ref[ci % 2] = _dot(act, wd_ref[0])

    def scatter(ci):
        for r0 in range(0, tc, SUBLANES):
            cs = [base + ci * tc + r0 + k for k in range(SUBLANES)]
            toks = [idx_ref[c] for c in cs]
            rows = [acc_ref[pl.ds(tok, 1), :] + gsel_ref[c] * y_ref[ci % 2, r0 + k:r0 + k + 1, :]
                    for k, (c, tok) in enumerate(zip(cs, toks))]
            for tok, row in zip(toks, rows):
                acc_ref[pl.ds(tok, 1), :] = row

    for ci in range(n_chunks):
        ffn(ci)
        if ci > 0:
            scatter(ci - 1)
    scatter(n_chunks - 1)

    @pl.when(e == N_EXPERTS - 1)
    def _():
        n_tiles = acc_ref.shape[0] // out_rows

        def out_copy(i):
            rows = pl.ds(pl.multiple_of(i * out_rows, out_rows), out_rows)
            return rows, pltpu.make_async_copy(acc_ref.at[rows], o_hbm.at[b, rows], sem)

        def emit(i, _):
            rows, copy = out_copy(i)
            if apply_norm:
                acc_ref[rows, :] = _rms(acc_ref[rows, :], gf_ref[...])
            copy.start()
            return 0

        def drain(i, _):
            out_copy(i)[1].wait()
            return 0

        lax.fori_loop(0, n_tiles, emit, 0)
        lax.fori_loop(0, n_tiles, drain, 0)


def _experts(idx_flat, gsel_flat, xg, wg, wu, wd, x2, gf, apply_norm, seq, tc):
    batch, _, cap, half = xg.shape
    d = 2 * half
    ff = wg.shape[2]
    slab = seq // N_EXPERTS
    grid_spec = pltpu.PrefetchScalarGridSpec(
        num_scalar_prefetch=2,
        grid=(batch, N_EXPERTS),
        in_specs=[
            pl.BlockSpec((1, 1, cap, half), lambda b, e, idx, gs: (b, e, 0, 0)),
            pl.BlockSpec((1, d, ff), lambda b, e, idx, gs: (e, 0, 0)),
            pl.BlockSpec((1, d, ff), lambda b, e, idx, gs: (e, 0, 0)),
            pl.BlockSpec((1, ff, d), lambda b, e, idx, gs: (e, 0, 0)),
            pl.BlockSpec((slab, d), lambda b, e, idx, gs: (b * N_EXPERTS + e, 0)),
            pl.BlockSpec(gf.shape, lambda b, e, idx, gs: (0, 0)),
        ],
        out_specs=pl.BlockSpec(memory_space=pl.ANY),
        scratch_shapes=[pltpu.VMEM((seq, d), F32), pltpu.VMEM((2, tc, d), F32), pltpu.SemaphoreType.DMA],
    )
    return pl.pallas_call(
        functools.partial(_experts_kernel, cap, tc, apply_norm, min(seq, OUT_TILE_ROWS)),
        grid_spec=grid_spec,
        out_shape=jax.ShapeDtypeStruct((batch, seq, d), F32),
        compiler_params=pltpu.CompilerParams(dimension_semantics=("arbitrary", "arbitrary"),
                                             vmem_limit_bytes=EXPERTS_VMEM_LIMIT_BYTES),
        name="experts",
    )(idx_flat, gsel_flat, xg, wg, wu, wd, x2, gf)


def _rot_cols(w):
    half = w.shape[-1] // 2
    return jnp.concatenate([-w[..., half:], w[..., :half]], axis=-1)


def _tile(n, pref):
    return pref if n % pref == 0 else n


def kernel(x, mem, norm_mix_g, w_in, conv_w, conv_b, w_conv_out, q_norm_g, w_uq, kv_norm_g, w_ukv,
           w_mla_out, b_gate, w_mix_out, norm_mem_g, norm_memkv_g, w_mem_q, w_mem_kv, w_mem_out,
           norm_moe_g, w_router, w_exp_gate, w_exp_up, w_exp_down, norm_final_g):
    batch, seq, d = x.shape
    depth = w_in.shape[0]
    t = batch * seq
    q_lora = q_norm_g.shape[1]
    kv_lora = kv_norm_g.shape[1]
    cap = max(1, CAPACITY_FACTOR * seq // N_EXPERTS)
    assert cap % LANES == 0 and seq % LANES == 0 and d % (2 * LANES) == 0

    tm_in = _tile(seq, 512)
    tm_post = _tile(seq, 512)
    tq = _tile(seq, 1024)
    tk = _tile(seq, 512)
    tc = _tile(cap, 256)

    inv = 1.0 / (ROPE_THETA ** (np.arange(0, QK_ROPE, 2, dtype=np.float64) / QK_ROPE))
    ang = np.arange(seq, dtype=np.float64)[:, None] * inv[None, :]
    cos2 = np.concatenate([np.cos(ang), np.cos(ang)], axis=-1)
    sin2 = np.concatenate([np.sin(ang), np.sin(ang)], axis=-1)
    qf = (QK_DIM ** -0.5) * math.log2(math.e)
    kc, ks = jnp.asarray(cos2, F32), jnp.asarray(sin2, F32)
    qc, qs = jnp.asarray(qf * cos2.T, F32), jnp.asarray(qf * sin2.T, F32)

    o_cq = 3 * d
    o_ckv = o_cq + q_lora
    o_kr = o_ckv + kv_lora
    o_gl = o_kr + QK_ROPE
    cols = {"xc": 0, "gb": d, "gc": 2 * d, "cq": o_cq, "ckv": o_ckv}

    x2d = x.reshape(t, d)
    mem2d = mem.reshape(-1, d)
    row = lambda a: a.reshape(1, -1)

    for l in range(depth):
        win = w_in[l].astype(BF16)
        wgl = win[:, o_gl:]
        wkr = jnp.concatenate([win[:, o_kr:o_gl], _rot_cols(win[:, o_kr:o_gl])], axis=1)
        wq3 = w_uq[l].reshape(q_lora, MLA_HEADS, QK_DIM)
        wq = wq3.transpose(1, 2, 0).astype(BF16)
        wqr = _rot_cols(wq3[..., QK_NOPE:]).transpose(1, 2, 0).astype(BF16)
        wkv3 = w_ukv[l].reshape(kv_lora, MLA_HEADS, QK_NOPE + V_HEAD)
        wk = wkv3[..., :QK_NOPE].transpose(1, 0, 2).astype(BF16)
        wv = wkv3[..., QK_NOPE:].transpose(1, 2, 0).astype(BF16)

        wr_hi = w_router[l].T.astype(BF16)
        wr_lo = (w_router[l].T - wr_hi.astype(F32)).astype(BF16)

        v, gb, gates, qt, k, vt = _inproj(
            x2d, batch, seq, row(norm_mix_g[l]), win, wgl, wkr, cols, row(q_norm_g[l]), row(kv_norm_g[l]),
            row(b_gate[l]), wq, wqr, wk, wv, qc, qs, kc, ks, qf, tm_in)
        w_exp = [w_exp_gate[l], w_exp_up[l], w_exp_down[l]]
        w_flat = [w.reshape(-1, w.shape[-1]) for w in w_exp]
        n_attn_steps = batch * MLA_HEADS * (seq // tq)
        fused_cast = all(_side_cast_fits(w, n_attn_steps) for w in w_flat)
        att, w_bf = _attention(qt, k, vt, tq, tk, w_flat if fused_cast else [])
        att = att.reshape(t, MLA_HEADS * V_HEAD)
        wgb, wub, wdb = ([wb.reshape(w.shape) for wb, w in zip(w_bf, w_exp)] if fused_cast
                         else [w.astype(BF16) for w in w_exp])
        mk, mv = _memkv(mem2d, row(norm_memkv_g[l]), w_mem_kv[l].astype(BF16))
        x2, hp, aff = _post(
            x2d, batch, seq, v, gb, gates, att, conv_w[l], row(conv_b[l]), w_conv_out[l].astype(BF16),
            w_mla_out[l].astype(BF16), w_mix_out[l].astype(BF16), row(norm_mem_g[l]),
            w_mem_q[l].astype(BF16), mk, mv, w_mem_out[l].astype(BF16), row(norm_moe_g[l]),
            wr_hi, wr_lo, tm_post)
        idx, gsel = _route(aff.reshape(batch * N_EXPERTS, seq), cap)
        idx_flat = idx.reshape(-1)
        xg = _gather(idx_flat, hp.reshape(batch, seq, d // 2), cap)
        x2d = _experts(idx_flat, gsel.reshape(-1), xg, wgb, wub, wdb, x2, row(norm_final_g),
                       l == depth - 1, seq, tc).reshape(t, d)
    return x2d.reshape(batch, seq, d)
```

```python
import functools
import math

import jax
import jax.numpy as jnp
import numpy as np
from jax import lax
from jax.experimental import pallas as pl
from jax.experimental.pallas import tpu as pltpu

MLA_HEADS = 8
QK_NOPE = 128
QK_ROPE = 64
QK_DIM = QK_NOPE + QK_ROPE
V_HEAD = 128
V_AUG = V_HEAD + 16
ROPE_THETA = 10000.0
MEM_HEADS = 4
N_EXPERTS = 16
CAPACITY_FACTOR = 2
EPS = 1e-6

LANES = 128
SUBLANES = 8
BF16_SUBLANES = 16
VMEM_LIMIT_BYTES = 56 * 1024 * 1024
EXPERTS_VMEM_LIMIT_BYTES = 60 * 1024 * 1024

INPROJ_TILE = 512
POST_TILE = 512
POST_SUBTILE = 256
ATTN_Q_TILE = 1024
ATTN_K_TILE = 512
ATTN_SCORE_SLOTS = 3
ATTN_BODY_ROUNDS = 2
EXPERT_CHUNK = 256
OUT_TILE_ROWS = 256
SIDE_CAST_SLICE_BYTES = 1024 * 1024

F32 = jnp.float32
BF16 = jnp.bfloat16


def _const_spec(shape):
    nd = len(shape)
    return pl.BlockSpec(shape, lambda *_: (0,) * nd, pipeline_mode=pl.Buffered(1))


def _rms(x, g):
    return x * lax.rsqrt(jnp.mean(x * x, axis=-1, keepdims=True) + EPS) * g


def _dot(a, b):
    return jnp.dot(a, b, preferred_element_type=F32)


def _dot_t(a, b):
    return lax.dot_general(a, b, (((1,), (1,)), ((), ())), preferred_element_type=F32)


def _inproj_kernel(cols, q_scale, x_ref, g_ref, win_ref, wgl_ref, wkr_ref, qg_ref, kvg_ref, bg_ref,
                   wq_ref, wqr_ref, wk_ref, wv_ref, qc_ref, qs_ref, kc_ref, ks_ref,
                   v_ref, gb_ref, gates_ref, qt_ref, k_ref, vt_ref):
    d = x_ref.shape[1]
    hb = _rms(x_ref[...], g_ref[...]).astype(BF16)

    def proj(name, width):
        lo = cols[name]
        return _dot(hb, win_ref[:, lo:lo + width])

    v_ref[...] = (proj("gc", d) * proj("xc", d)).astype(BF16)
    gb_ref[...] = proj("gb", d).astype(BF16)
    gates_ref[...] = jax.nn.sigmoid(_dot(hb, wgl_ref[...]) + bg_ref[...]).astype(BF16)

    q_lora = qg_ref.shape[1]
    kv_lora = kvg_ref.shape[1]
    qn = _rms(proj("cq", q_lora), qg_ref[...]).astype(BF16)
    kvn = _rms(proj("ckv", kv_lora), kvg_ref[...]).astype(BF16)
    kr2 = _dot(hb, wkr_ref[...])
    k_rope = kr2[:, :QK_ROPE] * kc_ref[...] + kr2[:, QK_ROPE:] * ks_ref[...]
    qc = qc_ref[...]
    qs = qs_ref[...]
    pad_row = lax.broadcasted_iota(jnp.int32, (V_AUG - V_HEAD, x_ref.shape[0]), 0)
    ones_rows = jnp.where(pad_row == 0, 1.0, 0.0).astype(BF16)
    for h in range(MLA_HEADS):
        qt_h = _dot_t(wq_ref[h], qn)
        rope = qt_h[QK_NOPE:] * qc + _dot_t(wqr_ref[h], qn) * qs
        qt_ref[0, h] = jnp.concatenate([qt_h[:QK_NOPE] * q_scale, rope], axis=0).astype(BF16)
        k_ref[0, h] = jnp.concatenate([_dot(kvn, wk_ref[h]), k_rope], axis=-1).astype(BF16)
        vt_ref[0, h, :V_HEAD, :] = _dot_t(wv_ref[h], kvn).astype(BF16)
        vt_ref[0, h, V_HEAD:, :] = ones_rows


def _inproj(x2d, batch, seq, g, win, wgl, wkr, cols, qg, kvg, bg, wq, wqr, wk, wv, qc, qs, kc, ks, q_scale, tm):
    t, d = x2d.shape
    nb = seq // tm
    tok = lambda i: (i, 0)
    pos = lambda i: (i % nb, 0)
    pos_t = lambda i: (0, i % nb)
    head_blk = lambda i: (i // nb, 0, i % nb, 0)
    head_blk_t = lambda i: (i // nb, 0, 0, i % nb)
    in_specs = [
        pl.BlockSpec((tm, d), tok),
        _const_spec(g.shape), _const_spec(win.shape), _const_spec(wgl.shape), _const_spec(wkr.shape),
        _const_spec(qg.shape), _const_spec(kvg.shape),
        _const_spec(bg.shape), _const_spec(wq.shape), _const_spec(wqr.shape), _const_spec(wk.shape),
        _const_spec(wv.shape),
        pl.BlockSpec((QK_ROPE, tm), pos_t), pl.BlockSpec((QK_ROPE, tm), pos_t),
        pl.BlockSpec((tm, QK_ROPE), pos), pl.BlockSpec((tm, QK_ROPE), pos),
    ]
    out_shape = [
        jax.ShapeDtypeStruct((t, d), BF16),
        jax.ShapeDtypeStruct((t, d), BF16),
        jax.ShapeDtypeStruct((t, 2 * d), BF16),
        jax.ShapeDtypeStruct((batch, MLA_HEADS, QK_DIM, seq), BF16),
        jax.ShapeDtypeStruct((batch, MLA_HEADS, seq, QK_DIM), BF16),
        jax.ShapeDtypeStruct((batch, MLA_HEADS, V_AUG, seq), BF16),
    ]
    out_specs = [
        pl.BlockSpec((tm, d), tok), pl.BlockSpec((tm, d), tok), pl.BlockSpec((tm, 2 * d), tok),
        pl.BlockSpec((1, MLA_HEADS, QK_DIM, tm), head_blk_t),
        pl.BlockSpec((1, MLA_HEADS, tm, QK_DIM), head_blk),
        pl.BlockSpec((1, MLA_HEADS, V_AUG, tm), head_blk_t),
    ]
    return pl.pallas_call(
        functools.partial(_inproj_kernel, cols, q_scale),
        grid=(t // tm,), in_specs=in_specs, out_specs=out_specs, out_shape=out_shape,
        compiler_params=pltpu.CompilerParams(dimension_semantics=("arbitrary",),
                                             vmem_limit_bytes=VMEM_LIMIT_BYTES),
        name="inproj",
    )(x2d, g, win, wgl, wkr, qg, kvg, bg, wq, wqr, wk, wv, qc, qs, kc, ks)


def _attn_kernel(tk, n_slots, n_cast, qt_ref, k_ref, vt_ref, *refs):
    cast_in, o_ref, cast_out, s_ref = refs[:n_cast], refs[n_cast], refs[n_cast + 1:2 * n_cast + 1], refs[-1]
    for src, dst in zip(cast_in, cast_out):
        dst[...] = src[...].astype(BF16)
    qt = qt_ref[0, 0]
    tq = qt.shape[1]
    nk = k_ref.shape[2] // tk

    def scores(j, slot):
        start = pl.multiple_of(j * tk, tk)
        s = _dot(k_ref[0, 0, pl.ds(start, tk), :], qt)
        s_ref[slot] = s
        return jnp.max(s, axis=0, keepdims=True)

    def update(j, slot, mx, m, acc):
        start = pl.multiple_of(j * tk, tk)
        m_new = jnp.maximum(m, mx)
        p = jnp.exp2(s_ref[slot] - m_new)
        acc = jnp.exp2(m - m_new) * acc + _dot(vt_ref[0, 0, :, pl.ds(start, tk)], p.astype(BF16))
        return m_new, acc

    ahead = n_slots - 1

    def chunks(j0, count, pending, m, acc):
        pending = list(pending)
        for u in range(count):
            if isinstance(j0, int) and j0 + u + ahead >= nk:
                pass
            else:
                pending.append(scores(j0 + u + ahead, (u + ahead) % n_slots))
            m, acc = update(j0 + u, u % n_slots, pending.pop(0), m, acc)
        return tuple(pending), m, acc

    body = n_slots * ATTN_BODY_ROUNDS
    n_iter = max(nk - ahead, 0) // body
    pending = tuple(scores(j, j) for j in range(min(ahead, nk)))
    m0 = jnp.full((1, tq), -1e30, F32)
    acc0 = jnp.zeros((vt_ref.shape[2], tq), F32)
    pending, m, acc = lax.fori_loop(0, n_iter, lambda i, c: chunks(body * i, body, *c),
                                    (pending, m0, acc0))
    _, _, acc = chunks(body * n_iter, nk - body * n_iter, pending, m, acc)
    o_ref[0] = (acc[:V_HEAD] / acc[V_HEAD:V_HEAD + 1]).T.astype(BF16)


def _side_cast_fits(w, n_steps):
    rows, cols = w.shape
    slice_rows = rows // n_steps
    return (rows % n_steps == 0 and slice_rows % BF16_SUBLANES == 0
            and slice_rows * cols * 4 <= SIDE_CAST_SLICE_BYTES)


def _attention(qt, k, vt, tq, tk, side_casts):
    batch, heads, seq, _ = k.shape
    nk = seq // tk
    nq = seq // tq
    n_steps = batch * heads * nq
    step = lambda b, h, i: ((b * heads + h) * nq + i, 0)
    cast_specs = [pl.BlockSpec((w.shape[0] // n_steps, w.shape[1]), step) for w in side_casts]
    outs = pl.pallas_call(
        functools.partial(_attn_kernel, tk, ATTN_SCORE_SLOTS, len(side_casts)),
        grid=(batch, heads, nq),
        in_specs=[
            pl.BlockSpec((1, 1, QK_DIM, tq), lambda b, h, i: (b, h, 0, i)),
            pl.BlockSpec((1, 1, seq, QK_DIM), lambda b, h, i: (b, h, 0, 0)),
            pl.BlockSpec((1, 1, V_AUG, seq), lambda b, h, i: (b, h, 0, 0)),
        ] + cast_specs,
        out_specs=[pl.BlockSpec((1, tq, V_HEAD), lambda b, h, i: (b, i, h))] + cast_specs,
        out_shape=[jax.ShapeDtypeStruct((batch, seq, heads * V_HEAD), BF16)]
                  + [jax.ShapeDtypeStruct(w.shape, BF16) for w in side_casts],
        scratch_shapes=[pltpu.VMEM((ATTN_SCORE_SLOTS, tk, tq), F32)],
        compiler_params=pltpu.CompilerParams(
            dimension_semantics=("arbitrary", "arbitrary", "arbitrary"),
            vmem_limit_bytes=VMEM_LIMIT_BYTES),
        name="mla_attention",
    )(qt, k, vt, *side_casts)
    return outs[0], outs[1:]


def _memkv_kernel(m_ref, g_ref, w_ref, k_ref, v_ref):
    d = m_ref.shape[1]
    kv = _dot(_rms(m_ref[...], g_ref[...]).astype(BF16), w_ref[...])
    k_ref[...] = kv[:, :d].astype(BF16)
    v_ref[...] = kv[:, d:].astype(BF16)


def _memkv(mem2d, g, w):
    n, d = mem2d.shape
    return pl.pallas_call(
        _memkv_kernel,
        out_shape=[jax.ShapeDtypeStruct((n, d), BF16), jax.ShapeDtypeStruct((n, d), BF16)],
        compiler_params=pltpu.CompilerParams(vmem_limit_bytes=VMEM_LIMIT_BYTES),
        name="memkv",
    )(mem2d, g, w)


def _pack_bf16_pairs(lo, hi):
    return pltpu.bitcast(pltpu.pack_elementwise([lo, hi], packed_dtype=BF16), jnp.uint32)


def _unpack_bf16_pair(words, index):
    return pltpu.unpack_elementwise(words, index=index, packed_dtype=BF16, unpacked_dtype=F32).astype(BF16)


def _post_kernel(nb, sub, x_ref, v_ref, vprev_ref, vnext_ref, gb_ref, gates_ref, att_ref,
                 cw_ref, cb_ref, wco_ref, wmo_ref, wmix_ref, gmem_ref, wmq_ref, mk_ref, mv_ref, wmout_ref,
                 gmoe_ref, wrh_ref, wrl_ref,
                 x2_ref, hp_ref, aff_ref):
    i = pl.program_id(0)
    tm, d = x_ref.shape
    v = v_ref[...].astype(F32)
    first = (i % nb) == 0
    last = (i % nb) == nb - 1
    halo_prev = jnp.where(first, 0.0, vprev_ref[...].astype(F32)[BF16_SUBLANES - 1:BF16_SUBLANES, :])
    halo_next = jnp.where(last, 0.0, vnext_ref[...].astype(F32)[0:1, :])
    row = lax.broadcasted_iota(jnp.int32, (tm, d), 0)
    v_prev = jnp.where(row == 0, halo_prev, pltpu.roll(v, 1, 0))
    v_next = jnp.where(row == tm - 1, halo_next, pltpu.roll(v, tm - 1, 0))
    conv = cw_ref[0:1, :] * v_prev + cw_ref[1:2, :] * v + cw_ref[2:3, :] * v_next + cb_ref[...]
    conv_in = (gb_ref[...].astype(F32) * conv).astype(BF16)
    hd = d // MEM_HEADS
    tiles = [pl.ds(r0, sub) for r0 in range(0, tm, sub)]
    y_conv = [_dot(conv_in[r0:r0 + sub], wco_ref[...]) for r0 in range(0, tm, sub)]
    y_mla = [_dot(att_ref[rows, :], wmo_ref[...]) for rows in tiles]
    mixed = [(gates_ref[rows, :d].astype(F32) * yc + gates_ref[rows, d:].astype(F32) * ym).astype(BF16)
             for rows, yc, ym in zip(tiles, y_conv, y_mla)]
    x1 = [x_ref[rows, :] + _dot(mx, wmix_ref[...]) for rows, mx in zip(tiles, mixed)]
    qm = [(_dot((x * gmem_ref[...]).astype(BF16), wmq_ref[...])
           * (lax.rsqrt(jnp.mean(x * x, axis=-1, keepdims=True) + EPS) * (hd ** -0.5))).astype(BF16) for x in x1]
    att_m = []
    for q in qm:
        outs = []
        for h in range(MEM_HEADS):
            sl = slice(h * hd, (h + 1) * hd)
            s = _dot_t(q[:, sl], mk_ref[:, sl])
            p = jnp.exp(s - jnp.max(s, axis=-1, keepdims=True))
            inv_l = 1.0 / jnp.sum(p, axis=-1, keepdims=True)
            outs.append((_dot(p.astype(BF16), mv_ref[:, sl]) * inv_l).astype(BF16))
        att_m.append(jnp.concatenate(outs, axis=-1))
    x2 = [x + _dot(o, wmout_ref[...]) for x, o in zip(x1, att_m)]
    for rows, x in zip(tiles, x2):
        x2_ref[rows, :] = x
        h3 = _rms(x, gmoe_ref[...])
        hp_ref[rows, :] = _pack_bf16_pairs(h3[:, :d // 2], h3[:, d // 2:])
        h_hi = h3.astype(BF16)
        h_lo = (h3 - h_hi.astype(F32)).astype(BF16)
        logits = (_dot_t(wrh_ref[...], h_hi) + _dot_t(wrh_ref[...], h_lo)
                  + _dot_t(wrl_ref[...], h_hi) + _dot_t(wrl_ref[...], h_lo))
        e = jnp.exp(logits - jnp.max(logits, axis=0, keepdims=True))
        aff_ref[0, :, rows] = e / jnp.sum(e, axis=0, keepdims=True)


def _post(x2d, batch, seq, v, gb, gates, att, cw, cb, wco, wmo, wmix, gmem, wmq, mk, mv, wmout, gmoe, wrh, wrl, tm):
    t, d = x2d.shape
    nb = seq // tm
    hb = tm // BF16_SUBLANES
    n_halo = t // BF16_SUBLANES
    tok = lambda i: (i, 0)
    mem_len = mk.shape[0] // batch
    in_specs = [
        pl.BlockSpec((tm, d), tok), pl.BlockSpec((tm, d), tok),
        pl.BlockSpec((BF16_SUBLANES, d), lambda i: (jnp.maximum(i * hb - 1, 0), 0)),
        pl.BlockSpec((BF16_SUBLANES, d), lambda i: (jnp.minimum((i + 1) * hb, n_halo - 1), 0)),
        pl.BlockSpec((tm, d), tok), pl.BlockSpec((tm, 2 * d), tok), pl.BlockSpec((tm, d), tok),
        _const_spec(cw.shape), _const_spec(cb.shape), _const_spec(wco.shape), _const_spec(wmo.shape),
        _const_spec(wmix.shape), _const_spec(gmem.shape), _const_spec(wmq.shape),
        pl.BlockSpec((mem_len, d), lambda i: (i // nb, 0)),
        pl.BlockSpec((mem_len, d), lambda i: (i // nb, 0)),
        _const_spec(wmout.shape), _const_spec(gmoe.shape), _const_spec(wrh.shape), _const_spec(wrl.shape),
    ]
    out_shape = [
        jax.ShapeDtypeStruct((t, d), F32),
        jax.ShapeDtypeStruct((t, d // 2), jnp.uint32),
        jax.ShapeDtypeStruct((batch, N_EXPERTS, seq), F32),
    ]
    out_specs = [
        pl.BlockSpec((tm, d), tok), pl.BlockSpec((tm, d // 2), tok),
        pl.BlockSpec((1, N_EXPERTS, tm), lambda i: (i // nb, 0, i % nb)),
    ]
    return pl.pallas_call(
        functools.partial(_post_kernel, nb, min(tm, POST_SUBTILE)),
        grid=(t // tm,), in_specs=in_specs, out_specs=out_specs, out_shape=out_shape,
        compiler_params=pltpu.CompilerParams(dimension_semantics=("arbitrary",),
                                             vmem_limit_bytes=VMEM_LIMIT_BYTES),
        name="post",
    )(x2d, v, v, v, gb, gates, att, cw, cb, wco, wmo, wmix, gmem, wmq, mk, mv, wmout, gmoe, wrh, wrl)


def _select_kernel(cap, aff_ref, key_ref, offs_ref):
    rows, seq = aff_ref.shape
    n_chunks = seq // LANES
    aff = aff_ref[...]

    def count_ge(x):
        return jnp.sum((aff >= x).astype(F32), axis=-1, keepdims=True)

    def search(b, t):
        cand = t | (jnp.int32(1) << (30 - b))
        return jnp.where(count_ge(pltpu.bitcast(cand, F32)) >= cap, cand, t)

    thr = lax.fori_loop(0, 31, search, jnp.zeros((rows, 1), jnp.int32))

    def refine(_, lohi):
        lo, hi = lohi
        mid = 0.5 * (lo + hi)
        take = count_ge(mid) >= cap
        return jnp.where(take, mid, lo), jnp.where(take, hi, mid)

    lo, hi = lax.fori_loop(0, 30, refine, (pltpu.bitcast(thr, F32), pltpu.bitcast(thr + 1, F32)))
    gt = aff >= hi
    eq = (aff >= lo) & (aff < hi)
    need = cap - jnp.sum(gt.astype(F32), axis=-1, keepdims=True)

    tri = (lax.broadcasted_iota(jnp.int32, (LANES, LANES), 0)
           <= lax.broadcasted_iota(jnp.int32, (LANES, LANES), 1)).astype(BF16)
    ones = jnp.ones((LANES, LANES), BF16)
    lane = lax.broadcasted_iota(jnp.int32, (rows, LANES), 1)
    run_eq = jnp.zeros((rows, LANES), F32)
    run_sel = jnp.zeros((rows, LANES), F32)
    offs = jnp.zeros((rows, LANES), F32)
    for j in range(n_chunks):
        sl = slice(j * LANES, (j + 1) * LANES)
        eq_j = eq[:, sl]
        eq_b = eq_j.astype(F32).astype(BF16)
        eq_rank = _dot(eq_b, tri) + run_eq
        run_eq = run_eq + _dot(eq_b, ones)
        sel_j = gt[:, sl] | (eq_j & (eq_rank <= need))
        sel_b = sel_j.astype(F32).astype(BF16)
        pos = _dot(sel_b, tri) + run_sel
        run_sel = run_sel + _dot(sel_b, ones)
        key_ref[:, sl] = jnp.where(sel_j, pos, 0.0)
        offs = jnp.where(lane == j, run_sel, offs)
    offs_ref[...] = offs.astype(jnp.int32)


def _compact_kernel(cap, n_chunks, offs_ref, key_ref, aff_ref, idx_ref, gsel_ref):
    r = pl.program_id(0)
    n_cblk = cap // LANES
    slot = lax.broadcasted_iota(jnp.int32, (LANES, LANES), 0) + 1
    lane = lax.broadcasted_iota(jnp.int32, (LANES, LANES), 1)
    obase = r * LANES
    j0 = jnp.int32(0)
    for cb in range(n_cblk):
        want = (slot + cb * LANES).astype(F32)
        j0 = lax.while_loop(lambda j: (j < n_chunks - 1) & (offs_ref[obase + j] <= cb * LANES),
                            lambda j: j + 1, j0)
        j1 = lax.while_loop(lambda j: (j < n_chunks - 1) & (offs_ref[obase + j] < (cb + 1) * LANES),
                            lambda j: j + 1, j0)

        def per_chunk(j, carry):
            tok_sel, g_sel = carry
            start = pl.multiple_of(j * LANES, LANES)
            hit = key_ref[0, :, pl.ds(start, LANES)] == want
            tok_sel = jnp.where(hit, lane + j * LANES, tok_sel)
            g_sel = jnp.where(hit, aff_ref[0, :, pl.ds(start, LANES)], g_sel)
            return tok_sel, g_sel

        init = (jnp.zeros((LANES, LANES), jnp.int32), jnp.zeros((LANES, LANES), F32))
        tok_sel, g_sel = lax.fori_loop(j0, j1 + 1, per_chunk, init)
        idx_ref[0, cb:cb + 1, :] = jnp.sum(tok_sel.T, axis=0, keepdims=True)
        gsel_ref[0, cb:cb + 1, :] = jnp.sum(g_sel.T, axis=0, keepdims=True)
        j0 = j1


def _route(aff2d, cap):
    rows, seq = aff2d.shape
    n_chunks = seq // LANES
    n_cblk = cap // LANES
    assert n_chunks <= LANES
    key, offs = pl.pallas_call(
        functools.partial(_select_kernel, cap),
        out_shape=[jax.ShapeDtypeStruct((rows, seq), F32), jax.ShapeDtypeStruct((rows, LANES), jnp.int32)],
        compiler_params=pltpu.CompilerParams(vmem_limit_bytes=VMEM_LIMIT_BYTES),
        name="route_select",
    )(aff2d)
    row_blk = lambda r, offs: (r, 0, 0)
    grid_spec = pltpu.PrefetchScalarGridSpec(
        num_scalar_prefetch=1, grid=(rows,),
        in_specs=[pl.BlockSpec((1, 1, seq), row_blk), pl.BlockSpec((1, 1, seq), row_blk)],
        out_specs=[pl.BlockSpec((1, n_cblk, LANES), row_blk), pl.BlockSpec((1, n_cblk, LANES), row_blk)],
    )
    return pl.pallas_call(
        functools.partial(_compact_kernel, cap, n_chunks),
        grid_spec=grid_spec,
        out_shape=[jax.ShapeDtypeStruct((rows, n_cblk, LANES), jnp.int32),
                   jax.ShapeDtypeStruct((rows, n_cblk, LANES), F32)],
        compiler_params=pltpu.CompilerParams(dimension_semantics=("arbitrary",),
                                             vmem_limit_bytes=VMEM_LIMIT_BYTES),
        name="route_compact",
    )(offs.reshape(-1), key.reshape(rows, 1, seq), aff2d.reshape(rows, 1, seq))


def _gather_kernel(cap, idx_ref, hp_ref, xg_ref):
    b = pl.program_id(0)
    e = pl.program_id(1)
    base = (b * N_EXPERTS + e) * cap
    for c in range(cap):
        g, k = divmod(c, SUBLANES)
        xg_ref[0, 0, g, k:k + 1, :] = hp_ref[0, pl.ds(idx_ref[base + c], 1), :]


def _gather(idx_flat, hp, cap):
    batch, seq, half = hp.shape
    grid_spec = pltpu.PrefetchScalarGridSpec(
        num_scalar_prefetch=1,
        grid=(batch, N_EXPERTS),
        in_specs=[pl.BlockSpec((1, seq, half), lambda b, e, idx: (b, 0, 0), pipeline_mode=pl.Buffered(1))],
        out_specs=pl.BlockSpec((1, 1, cap // SUBLANES, SUBLANES, half), lambda b, e, idx: (b, e, 0, 0, 0)),
    )
    xg = pl.pallas_call(
        functools.partial(_gather_kernel, cap),
        grid_spec=grid_spec,
        out_shape=jax.ShapeDtypeStruct((batch, N_EXPERTS, cap // SUBLANES, SUBLANES, half), hp.dtype),
        compiler_params=pltpu.CompilerParams(dimension_semantics=("arbitrary", "arbitrary"),
                                             vmem_limit_bytes=VMEM_LIMIT_BYTES),
        name="gather",
    )(idx_flat, hp)
    return xg.reshape(batch, N_EXPERTS, cap, half)


def _experts_kernel(cap, tc, apply_norm, out_rows, idx_ref, gsel_ref, xg_ref, wg_ref, wu_ref, wd_ref, x2_ref, gf_ref,
                    o_hbm, acc_ref, y_ref, sem):
    b = pl.program_id(0)
    e = pl.program_id(1)
    base = (b * N_EXPERTS + e) * cap
    n_chunks = cap // tc

    @pl.when(e == 0)
    def _():
        acc_ref[...] = jnp.zeros_like(acc_ref)

    slab = x2_ref.shape[0]
    slab_rows = pl.ds(pl.multiple_of(e * slab, slab), slab)
    acc_ref[slab_rows, :] = acc_ref[slab_rows, :] + x2_ref[...]

    def ffn(ci):
        words = xg_ref[0, 0, ci * tc:(ci + 1) * tc, :]
        x = jnp.concatenate([_unpack_bf16_pair(words, 0), _unpack_bf16_pair(words, 1)], axis=-1)
        gate = _dot(x, wg_ref[0])
        up = _dot(x, wu_ref[0])
        act = (gate * jax.nn.sigmoid(gate) * up).astype(BF16)
        y_ref[ci % 2] = _dot(act, wd_ref[0])

    def scatter(ci):
        for r0 in range(0, tc, SUBLANES):
            cs = [base + ci * tc + r0 + k for k in range(SUBLANES)]
            toks = [idx_ref[c] for c in cs]
            rows = [acc_ref[pl.ds(tok, 1), :] + gsel_ref[c] * y_ref[ci % 2, r0 + k:r0 + k + 1, :]
                    for k, (c, tok) in enumerate(zip(cs, toks))]
            for tok, row in zip(toks, rows):
                acc_ref[pl.ds(tok, 1), :] = row

    for ci in range(n_chunks):
        ffn(ci)
        if ci > 0:
            scatter(ci - 1)
    scatter(n_chunks - 1)

    @pl.when(e == N_EXPERTS - 1)
    def _():
        n_tiles = acc_ref.shape[0] // out_rows

        def out_copy(i):
            rows = pl.ds(pl.multiple_of(i * out_rows, out_rows), out_rows)
            return rows, pltpu.make_async_copy(acc_ref.at[rows], o_hbm.at[b, rows], sem)

        def emit(i, _):
            rows, copy = out_copy(i)
            if apply_norm:
                acc_ref[rows, :] = _rms(acc_ref[rows, :], gf_ref[...])
            copy.start()
            return 0

        def drain(i, _):
            out_copy(i)[1].wait()
            return 0

        lax.fori_loop(0, n_tiles, emit, 0)
        lax.fori_loop(0, n_tiles, drain, 0)


def _experts(idx_flat, gsel_flat, xg, wg, wu, wd, x2, gf, apply_norm, seq, tc):
    batch, _, cap, half = xg.shape
    d = 2 * half
    ff = wg.shape[2]
    slab = seq // N_EXPERTS
    grid_spec = pltpu.PrefetchScalarGridSpec(
        num_scalar_prefetch=2,
        grid=(batch, N_EXPERTS),
        in_specs=[
            pl.BlockSpec((1, 1, cap, half), lambda b, e, idx, gs: (b, e, 0, 0)),
            pl.BlockSpec((1, d, ff), lambda b, e, idx, gs: (e, 0, 0)),
            pl.BlockSpec((1, d, ff), lambda b, e, idx, gs: (e, 0, 0)),
            pl.BlockSpec((1, ff, d), lambda b, e, idx, gs: (e, 0, 0)),
            pl.BlockSpec((slab, d), lambda b, e, idx, gs: (b * N_EXPERTS + e, 0)),
            pl.BlockSpec(gf.shape, lambda b, e, idx, gs: (0, 0)),
        ],
        out_specs=pl.BlockSpec(memory_space=pl.ANY),
        scratch_shapes=[pltpu.VMEM((seq, d), F32), pltpu.VMEM((2, tc, d), F32), pltpu.SemaphoreType.DMA],
    )
    return pl.pallas_call(
        functools.partial(_experts_kernel, cap, tc, apply_norm, min(seq, OUT_TILE_ROWS)),
        grid_spec=grid_spec,
        out_shape=jax.ShapeDtypeStruct((batch, seq, d), F32),
        compiler_params=pltpu.CompilerParams(dimension_semantics=("arbitrary", "arbitrary"),
                                             vmem_limit_bytes=EXPERTS_VMEM_LIMIT_BYTES),
        name="experts",
    )(idx_flat, gsel_flat, xg, wg, wu, wd, x2, gf)


def _rot_cols(w):
    half = w.shape[-1] // 2
    return jnp.concatenate([-w[..., half:], w[..., :half]], axis=-1)


def _tile(n, pref):
    return pref if n % pref == 0 else n


def kernel(x, mem, norm_mix_g, w_in, conv_w, conv_b, w_conv_out, q_norm_g, w_uq, kv_norm_g, w_ukv,
           w_mla_out, b_gate, w_mix_out, norm_mem_g, norm_memkv_g, w_mem_q, w_mem_kv, w_mem_out,
           norm_moe_g, w_router, w_exp_gate, w_exp_up, w_exp_down, norm_final_g):
    batch, seq, d = x.shape
    depth = w_in.shape[0]
    t = batch * seq
    q_lora = q_norm_g.shape[1]
    kv_lora = kv_norm_g.shape[1]
    cap = max(1, CAPACITY_FACTOR * seq // N_EXPERTS)
    assert cap % LANES == 0 and seq % LANES == 0 and d % (2 * LANES) == 0

    tm_in = _tile(seq, INPROJ_TILE)
    tm_post = _tile(seq, POST_TILE)
    tq = _tile(seq, ATTN_Q_TILE)
    tk = _tile(seq, ATTN_K_TILE)
    tc = _tile(cap, EXPERT_CHUNK)

    inv = 1.0 / (ROPE_THETA ** (np.arange(0, QK_ROPE, 2, dtype=np.float64) / QK_ROPE))
    ang = np.arange(seq, dtype=np.float64)[:, None] * inv[None, :]
    cos2 = np.concatenate([np.cos(ang), np.cos(ang)], axis=-1)
    sin2 = np.concatenate([np.sin(ang), np.sin(ang)], axis=-1)
    qf = (QK_DIM ** -0.5) * math.log2(math.e)
    kc, ks = jnp.asarray(cos2, F32), jnp.asarray(sin2, F32)
    qc, qs = jnp.asarray(qf * cos2.T, F32), jnp.asarray(qf * sin2.T, F32)

    o_cq = 3 * d
    o_ckv = o_cq + q_lora
    o_kr = o_ckv + kv_lora
    o_gl = o_kr + QK_ROPE
    cols = {"xc": 0, "gb": d, "gc": 2 * d, "cq": o_cq, "ckv": o_ckv}

    x2d = x.reshape(t, d)
    mem2d = mem.reshape(-1, d)
    row = lambda a: a.reshape(1, -1)

    for l in range(depth):
        win = w_in[l].astype(BF16)
        wgl = win[:, o_gl:]
        wkr = jnp.concatenate([win[:, o_kr:o_gl], _rot_cols(win[:, o_kr:o_gl])], axis=1)
        wq3 = w_uq[l].reshape(q_lora, MLA_HEADS, QK_DIM)
        wq = wq3.transpose(1, 2, 0).astype(BF16)
        wqr = _rot_cols(wq3[..., QK_NOPE:]).transpose(1, 2, 0).astype(BF16)
        wkv3 = w_ukv[l].reshape(kv_lora, MLA_HEADS, QK_NOPE + V_HEAD)
        wk = wkv3[..., :QK_NOPE].transpose(1, 0, 2).astype(BF16)
        wv = wkv3[..., QK_NOPE:].transpose(1, 2, 0).astype(BF16)

        wr_hi = w_router[l].T.astype(BF16)
        wr_lo = (w_router[l].T - wr_hi.astype(F32)).astype(BF16)

        v, gb, gates, qt, k, vt = _inproj(
            x2d, batch, seq, row(norm_mix_g[l]), win, wgl, wkr, cols, row(q_norm_g[l]), row(kv_norm_g[l]),
            row(b_gate[l]), wq, wqr, wk, wv, qc, qs, kc, ks, qf, tm_in)
        w_exp = [w_exp_gate[l], w_exp_up[l], w_exp_down[l]]
        w_flat = [w.reshape(-1, w.shape[-1]) for w in w_exp]
        n_attn_steps = batch * MLA_HEADS * (seq // tq)
        fused_cast = all(_side_cast_fits(w, n_attn_steps) for w in w_flat)
        att, w_bf = _attention(qt, k, vt, tq, tk, w_flat if fused_cast else [])
        att = att.reshape(t, MLA_HEADS * V_HEAD)
        wgb, wub, wdb = ([wb.reshape(w.shape) for wb, w in zip(w_bf, w_exp)] if fused_cast
                         else [w.astype(BF16) for w in w_exp])
        mk, mv = _memkv(mem2d, row(norm_memkv_g[l]), w_mem_kv[l].astype(BF16))
        x2, hp, aff = _post(
            x2d, batch, seq, v, gb, gates, att, conv_w[l], row(conv_b[l]), w_conv_out[l].astype(BF16),
            w_mla_out[l].astype(BF16), w_mix_out[l].astype(BF16), row(norm_mem_g[l]),
            w_mem_q[l].astype(BF16), mk, mv, w_mem_out[l].astype(BF16), row(norm_moe_g[l]),
            wr_hi, wr_lo, tm_post)
        idx, gsel = _route(aff.reshape(batch * N_EXPERTS, seq), cap)
        idx_flat = idx.reshape(-1)
        xg = _gather(idx_flat, hp.reshape(batch, seq, d // 2), cap)
        x2d = _experts(idx_flat, gsel.reshape(-1), xg, wgb, wub, wdb, x2, row(norm_final_g),
                       l == depth - 1, seq, tc).reshape(t, d)
    return x2d.reshape(batch, seq, d)
```

```python
import functools
import math

import jax
import jax.numpy as jnp
import numpy as np
from jax import lax
from jax.experimental import pallas as pl
from jax.experimental.pallas import tpu as pltpu

MLA_HEADS = 8
QK_NOPE = 128
QK_ROPE = 64
QK_DIM = QK_NOPE + QK_ROPE
V_HEAD = 128
V_AUG = V_HEAD + 16
ROPE_THETA = 10000.0
MEM_HEADS = 4
N_EXPERTS = 16
CAPACITY_FACTOR = 2
EPS = 1e-6

LANES = 128
SUBLANES = 8
BF16_SUBLANES = 16
VMEM_LIMIT_BYTES = 56 * 1024 * 1024
EXPERTS_VMEM_LIMIT_BYTES = 60 * 1024 * 1024

INPROJ_TILE = 512
POST_TILE = 512
POST_SUBTILE = 256
ATTN_Q_TILE = 1024
ATTN_K_TILE = 512
ATTN_SCORE_SLOTS = 3
ATTN_BODY_ROUNDS = 2
EXPERT_CHUNK = 256
OUT_TILE_ROWS = 256
SIDE_CAST_SLICE_BYTES = 1024 * 1024

F32 = jnp.float32
BF16 = jnp.bfloat16


def _const_spec(shape):
    nd = len(shape)
    return pl.BlockSpec(shape, lambda *_: (0,) * nd, pipeline_mode=pl.Buffered(1))


def _rms(x, g):
    return x * lax.rsqrt(jnp.mean(x * x, axis=-1, keepdims=True) + EPS) * g


def _dot(a, b):
    return jnp.dot(a, b, preferred_element_type=F32)


def _dot_t(a, b):
    return lax.dot_general(a, b, (((1,), (1,)), ((), ())), preferred_element_type=F32)


def _inproj_kernel(cols, q_scale, x_ref, g_ref, win_ref, wgl_ref, wkr_ref, qg_ref, kvg_ref, bg_ref,
                   wq_ref, wqr_ref, wk_ref, wv_ref, qc_ref, qs_ref, kc_ref, ks_ref,
                   v_ref, gb_ref, gates_ref, qt_ref, k_ref, vt_ref):
    d = x_ref.shape[1]
    hb = _rms(x_ref[...], g_ref[...]).astype(BF16)

    def proj(name, width):
        lo = cols[name]
        return _dot(hb, win_ref[:, lo:lo + width])

    v_ref[...] = (proj("gc", d) * proj("xc", d)).astype(BF16)
    gb_ref[...] = proj("gb", d).astype(BF16)
    gates_ref[...] = jax.nn.sigmoid(_dot(hb, wgl_ref[...]) + bg_ref[...]).astype(BF16)

    q_lora = qg_ref.shape[1]
    kv_lora = kvg_ref.shape[1]
    qn = _rms(proj("cq", q_lora), qg_ref[...]).astype(BF16)
    kvn = _rms(proj("ckv", kv_lora), kvg_ref[...]).astype(BF16)
    kr2 = _dot(hb, wkr_ref[...])
    k_rope = kr2[:, :QK_ROPE] * kc_ref[...] + kr2[:, QK_ROPE:] * ks_ref[...]
    qc = qc_ref[...]
    qs = qs_ref[...]
    pad_row = lax.broadcasted_iota(jnp.int32, (V_AUG - V_HEAD, x_ref.shape[0]), 0)
    ones_rows = jnp.where(pad_row == 0, 1.0, 0.0).astype(BF16)
    for h in range(MLA_HEADS):
        qt_h = _dot_t(wq_ref[h], qn)
        rope = qt_h[QK_NOPE:] * qc + _dot_t(wqr_ref[h], qn) * qs
        qt_ref[0, h] = jnp.concatenate([qt_h[:QK_NOPE] * q_scale, rope], axis=0).astype(BF16)
        k_ref[0, h] = jnp.concatenate([_dot(kvn, wk_ref[h]), k_rope], axis=-1).astype(BF16)
        vt_ref[0, h, :V_HEAD, :] = _dot_t(wv_ref[h], kvn).astype(BF16)
        vt_ref[0, h, V_HEAD:, :] = ones_rows


def _inproj(x2d, batch, seq, g, win, wgl, wkr, cols, qg, kvg, bg, wq, wqr, wk, wv, qc, qs, kc, ks, q_scale, tm):
    t, d = x2d.shape
    nb = seq // tm
    tok = lambda i: (i, 0)
    pos = lambda i: (i % nb, 0)
    pos_t = lambda i: (0, i % nb)
    head_blk = lambda i: (i // nb, 0, i % nb, 0)
    head_blk_t = lambda i: (i // nb, 0, 0, i % nb)
    in_specs = [
        pl.BlockSpec((tm, d), tok),
        _const_spec(g.shape), _const_spec(win.shape), _const_spec(wgl.shape), _const_spec(wkr.shape),
        _const_spec(qg.shape), _const_spec(kvg.shape),
        _const_spec(bg.shape), _const_spec(wq.shape), _const_spec(wqr.shape), _const_spec(wk.shape),
        _const_spec(wv.shape),
        pl.BlockSpec((QK_ROPE, tm), pos_t), pl.BlockSpec((QK_ROPE, tm), pos_t),
        pl.BlockSpec((tm, QK_ROPE), pos), pl.BlockSpec((tm, QK_ROPE), pos),
    ]
    out_shape = [
        jax.ShapeDtypeStruct((t, d), BF16),
        jax.ShapeDtypeStruct((t, d), BF16),
        jax.ShapeDtypeStruct((t, 2 * d), BF16),
        jax.ShapeDtypeStruct((batch, MLA_HEADS, QK_DIM, seq), BF16),
        jax.ShapeDtypeStruct((batch, MLA_HEADS, seq, QK_DIM), BF16),
        jax.ShapeDtypeStruct((batch, MLA_HEADS, V_AUG, seq), BF16),
    ]
    out_specs = [
        pl.BlockSpec((tm, d), tok), pl.BlockSpec((tm, d), tok), pl.BlockSpec((tm, 2 * d), tok),
        pl.BlockSpec((1, MLA_HEADS, QK_DIM, tm), head_blk_t),
        pl.BlockSpec((1, MLA_HEADS, tm, QK_DIM), head_blk),
        pl.BlockSpec((1, MLA_HEADS, V_AUG, tm), head_blk_t),
    ]
    return pl.pallas_call(
        functools.partial(_inproj_kernel, cols, q_scale),
        grid=(t // tm,), in_specs=in_specs, out_specs=out_specs, out_shape=out_shape,
        compiler_params=pltpu.CompilerParams(dimension_semantics=("arbitrary",),
                                             vmem_limit_bytes=VMEM_LIMIT_BYTES),
        name="inproj",
    )(x2d, g, win, wgl, wkr, qg, kvg, bg, wq, wqr, wk, wv, qc, qs, kc, ks)


def _attn_kernel(tk, n_slots, n_cast, qt_ref, k_ref, vt_ref, *refs):
    cast_in, o_ref, cast_out, s_ref = refs[:n_cast], refs[n_cast], refs[n_cast + 1:2 * n_cast + 1], refs[-1]
    for src, dst in zip(cast_in, cast_out):
        dst[...] = src[...].astype(BF16)
    qt = qt_ref[0, 0]
    tq = qt.shape[1]
    nk = k_ref.shape[2] // tk

    def scores(j, slot):
        start = pl.multiple_of(j * tk, tk)
        s = _dot(k_ref[0, 0, pl.ds(start, tk), :], qt)
        s_ref[slot] = s
        return jnp.max(s, axis=0, keepdims=True)

    def update(j, slot, mx, m, acc):
        start = pl.multiple_of(j * tk, tk)
        m_new = jnp.maximum(m, mx)
        p = jnp.exp2(s_ref[slot] - m_new)
        acc = jnp.exp2(m - m_new) * acc + _dot(vt_ref[0, 0, :, pl.ds(start, tk)], p.astype(BF16))
        return m_new, acc

    ahead = n_slots - 1

    def chunks(j0, count, pending, m, acc):
        pending = list(pending)
        for u in range(count):
            if isinstance(j0, int) and j0 + u + ahead >= nk:
                pass
            else:
                pending.append(scores(j0 + u + ahead, (u + ahead) % n_slots))
            m, acc = update(j0 + u, u % n_slots, pending.pop(0), m, acc)
        return tuple(pending), m, acc

    body = n_slots * ATTN_BODY_ROUNDS
    n_iter = max(nk - ahead, 0) // body
    pending = tuple(scores(j, j) for j in range(min(ahead, nk)))
    m0 = jnp.full((1, tq), -jnp.inf, F32)
    acc0 = jnp.zeros((vt_ref.shape[2], tq), F32)
    pending, m, acc = lax.fori_loop(0, n_iter, lambda i, c: chunks(body * i, body, *c),
                                    (pending, m0, acc0))
    _, _, acc = chunks(body * n_iter, nk - body * n_iter, pending, m, acc)
    o_ref[0] = (acc[:V_HEAD] / acc[V_HEAD:V_HEAD + 1]).T.astype(BF16)


def _side_cast_fits(w, n_steps):
    rows, cols = w.shape
    slice_rows = rows // n_steps
    return (rows % n_steps == 0 and slice_rows % BF16_SUBLANES == 0
            and slice_rows * cols * 4 <= SIDE_CAST_SLICE_BYTES)


def _attention(qt, k, vt, tq, tk, side_casts):
    batch, heads, seq, _ = k.shape
    nk = seq // tk
    nq = seq // tq
    n_steps = batch * heads * nq
    step = lambda b, h, i: ((b * heads + h) * nq + i, 0)
    cast_specs = [pl.BlockSpec((w.shape[0] // n_steps, w.shape[1]), step) for w in side_casts]
    outs = pl.pallas_call(
        functools.partial(_attn_kernel, tk, ATTN_SCORE_SLOTS, len(side_casts)),
        grid=(batch, heads, nq),
        in_specs=[
            pl.BlockSpec((1, 1, QK_DIM, tq), lambda b, h, i: (b, h, 0, i)),
            pl.BlockSpec((1, 1, seq, QK_DIM), lambda b, h, i: (b, h, 0, 0)),
            pl.BlockSpec((1, 1, V_AUG, seq), lambda b, h, i: (b, h, 0, 0)),
        ] + cast_specs,
        out_specs=[pl.BlockSpec((1, tq, V_HEAD), lambda b, h, i: (b, i, h))] + cast_specs,
        out_shape=[jax.ShapeDtypeStruct((batch, seq, heads * V_HEAD), BF16)]
                  + [jax.ShapeDtypeStruct(w.shape, BF16) for w in side_casts],
        scratch_shapes=[pltpu.VMEM((ATTN_SCORE_SLOTS, tk, tq), F32)],
        compiler_params=pltpu.CompilerParams(
            dimension_semantics=("arbitrary", "arbitrary", "arbitrary"),
            vmem_limit_bytes=VMEM_LIMIT_BYTES),
        name="mla_attention",
    )(qt, k, vt, *side_casts)
    return outs[0], outs[1:]


def _memkv_kernel(m_ref, g_ref, w_ref, k_ref, v_ref):
    d = m_ref.shape[1]
    kv = _dot(_rms(m_ref[...], g_ref[...]).astype(BF16), w_ref[...])
    k_ref[...] = kv[:, :d].astype(BF16)
    v_ref[...] = kv[:, d:].astype(BF16)


def _memkv(mem2d, g, w):
    n, d = mem2d.shape
    return pl.pallas_call(
        _memkv_kernel,
        out_shape=[jax.ShapeDtypeStruct((n, d), BF16), jax.ShapeDtypeStruct((n, d), BF16)],
        compiler_params=pltpu.CompilerParams(vmem_limit_bytes=VMEM_LIMIT_BYTES),
        name="memkv",
    )(mem2d, g, w)


def _pack_bf16_pairs(lo, hi):
    return pltpu.bitcast(pltpu.pack_elementwise([lo, hi], packed_dtype=BF16), jnp.uint32)


def _unpack_bf16_pair(words, index):
    return pltpu.unpack_elementwise(words, index=index, packed_dtype=BF16, unpacked_dtype=F32).astype(BF16)


def _post_kernel(nb, sub, x_ref, v_ref, vprev_ref, vnext_ref, gb_ref, gates_ref, att_ref,
                 cw_ref, cb_ref, wco_ref, wmo_ref, wmix_ref, gmem_ref, wmq_ref, mk_ref, mv_ref, wmout_ref,
                 gmoe_ref, wrh_ref, wrl_ref,
                 x2_ref, hp_ref, aff_ref):
    i = pl.program_id(0)
    tm, d = x_ref.shape
    v = v_ref[...].astype(F32)
    first = (i % nb) == 0
    last = (i % nb) == nb - 1
    halo_prev = jnp.where(first, 0.0, vprev_ref[...].astype(F32)[BF16_SUBLANES - 1:BF16_SUBLANES, :])
    halo_next = jnp.where(last, 0.0, vnext_ref[...].astype(F32)[0:1, :])
    row = lax.broadcasted_iota(jnp.int32, (tm, d), 0)
    v_prev = jnp.where(row == 0, halo_prev, pltpu.roll(v, 1, 0))
    v_next = jnp.where(row == tm - 1, halo_next, pltpu.roll(v, tm - 1, 0))
    conv = cw_ref[0:1, :] * v_prev + cw_ref[1:2, :] * v + cw_ref[2:3, :] * v_next + cb_ref[...]
    conv_in = (gb_ref[...].astype(F32) * conv).astype(BF16)
    hd = d // MEM_HEADS
    tiles = [pl.ds(r0, sub) for r0 in range(0, tm, sub)]
    y_conv = [_dot(conv_in[r0:r0 + sub], wco_ref[...]) for r0 in range(0, tm, sub)]
    y_mla = [_dot(att_ref[rows, :], wmo_ref[...]) for rows in tiles]
    mixed = [(gates_ref[rows, :d].astype(F32) * yc + gates_ref[rows, d:].astype(F32) * ym).astype(BF16)
             for rows, yc, ym in zip(tiles, y_conv, y_mla)]
    x1 = [x_ref[rows, :] + _dot(mx, wmix_ref[...]) for rows, mx in zip(tiles, mixed)]
    qm = [(_dot((x * gmem_ref[...]).astype(BF16), wmq_ref[...])
           * (lax.rsqrt(jnp.mean(x * x, axis=-1, keepdims=True) + EPS) * (hd ** -0.5))).astype(BF16) for x in x1]
    att_m = []
    for q in qm:
        outs = []
        for h in range(MEM_HEADS):
            sl = slice(h * hd, (h + 1) * hd)
            s = _dot_t(q[:, sl], mk_ref[:, sl])
            p = jnp.exp(s - jnp.max(s, axis=-1, keepdims=True))
            inv_l = 1.0 / jnp.sum(p, axis=-1, keepdims=True)
            outs.append((_dot(p.astype(BF16), mv_ref[:, sl]) * inv_l).astype(BF16))
        att_m.append(jnp.concatenate(outs, axis=-1))
    x2 = [x + _dot(o, wmout_ref[...]) for x, o in zip(x1, att_m)]
    for rows, x in zip(tiles, x2):
        x2_ref[rows, :] = x
        h3 = _rms(x, gmoe_ref[...])
        hp_ref[rows, :] = _pack_bf16_pairs(h3[:, :d // 2], h3[:, d // 2:])
        h_hi = h3.astype(BF16)
        h_lo = (h3 - h_hi.astype(F32)).astype(BF16)
        logits = (_dot_t(wrh_ref[...], h_hi) + _dot_t(wrh_ref[...], h_lo)
                  + _dot_t(wrl_ref[...], h_hi) + _dot_t(wrl_ref[...], h_lo))
        e = jnp.exp(logits - jnp.max(logits, axis=0, keepdims=True))
        aff_ref[0, :, rows] = e / jnp.sum(e, axis=0, keepdims=True)


def _post(x2d, batch, seq, v, gb, gates, att, cw, cb, wco, wmo, wmix, gmem, wmq, mk, mv, wmout, gmoe, wrh, wrl, tm):
    t, d = x2d.shape
    nb = seq // tm
    hb = tm // BF16_SUBLANES
    n_halo = t // BF16_SUBLANES
    tok = lambda i: (i, 0)
    mem_len = mk.shape[0] // batch
    in_specs = [
        pl.BlockSpec((tm, d), tok), pl.BlockSpec((tm, d), tok),
        pl.BlockSpec((BF16_SUBLANES, d), lambda i: (jnp.maximum(i * hb - 1, 0), 0)),
        pl.BlockSpec((BF16_SUBLANES, d), lambda i: (jnp.minimum((i + 1) * hb, n_halo - 1), 0)),
        pl.BlockSpec((tm, d), tok), pl.BlockSpec((tm, 2 * d), tok), pl.BlockSpec((tm, d), tok),
        _const_spec(cw.shape), _const_spec(cb.shape), _const_spec(wco.shape), _const_spec(wmo.shape),
        _const_spec(wmix.shape), _const_spec(gmem.shape), _const_spec(wmq.shape),
        pl.BlockSpec((mem_len, d), lambda i: (i // nb, 0)),
        pl.BlockSpec((mem_len, d), lambda i: (i // nb, 0)),
        _const_spec(wmout.shape), _const_spec(gmoe.shape), _const_spec(wrh.shape), _const_spec(wrl.shape),
    ]
    out_shape = [
        jax.ShapeDtypeStruct((t, d), F32),
        jax.ShapeDtypeStruct((t, d // 2), jnp.uint32),
        jax.ShapeDtypeStruct((batch, N_EXPERTS, seq), F32),
    ]
    out_specs = [
        pl.BlockSpec((tm, d), tok), pl.BlockSpec((tm, d // 2), tok),
        pl.BlockSpec((1, N_EXPERTS, tm), lambda i: (i // nb, 0, i % nb)),
    ]
    return pl.pallas_call(
        functools.partial(_post_kernel, nb, min(tm, POST_SUBTILE)),
        grid=(t // tm,), in_specs=in_specs, out_specs=out_specs, out_shape=out_shape,
        compiler_params=pltpu.CompilerParams(dimension_semantics=("arbitrary",),
                                             vmem_limit_bytes=VMEM_LIMIT_BYTES),
        name="post",
    )(x2d, v, v, v, gb, gates, att, cw, cb, wco, wmo, wmix, gmem, wmq, mk, mv, wmout, gmoe, wrh, wrl)


def _select_kernel(cap, aff_ref, key_ref, offs_ref):
    rows, seq = aff_ref.shape
    n_chunks = seq // LANES
    aff = aff_ref[...]

    def count_ge(x):
        return jnp.sum((aff >= x).astype(F32), axis=-1, keepdims=True)

    def search(b, t):
        cand = t | (jnp.int32(1) << (30 - b))
        return jnp.where(count_ge(pltpu.bitcast(cand, F32)) >= cap, cand, t)

    thr = lax.fori_loop(0, 31, search, jnp.zeros((rows, 1), jnp.int32))

    def refine(_, lohi):
        lo, hi = lohi
        mid = 0.5 * (lo + hi)
        take = count_ge(mid) >= cap
        return jnp.where(take, mid, lo), jnp.where(take, hi, mid)

    lo, hi = lax.fori_loop(0, 30, refine, (pltpu.bitcast(thr, F32), pltpu.bitcast(thr + 1, F32)))
    gt = aff >= hi
    eq = (aff >= lo) & (aff < hi)
    need = cap - jnp.sum(gt.astype(F32), axis=-1, keepdims=True)

    tri = (lax.broadcasted_iota(jnp.int32, (LANES, LANES), 0)
           <= lax.broadcasted_iota(jnp.int32, (LANES, LANES), 1)).astype(BF16)
    ones = jnp.ones((LANES, LANES), BF16)
    lane = lax.broadcasted_iota(jnp.int32, (rows, LANES), 1)
    run_eq = jnp.zeros((rows, LANES), F32)
    run_sel = jnp.zeros((rows, LANES), F32)
    offs = jnp.zeros((rows, LANES), F32)
    for j in range(n_chunks):
        sl = slice(j * LANES, (j + 1) * LANES)
        eq_j = eq[:, sl]
        eq_b = eq_j.astype(F32).astype(BF16)
        eq_rank = _dot(eq_b, tri) + run_eq
        run_eq = run_eq + _dot(eq_b, ones)
        sel_j = gt[:, sl] | (eq_j & (eq_rank <= need))
        sel_b = sel_j.astype(F32).astype(BF16)
        pos = _dot(sel_b, tri) + run_sel
        run_sel = run_sel + _dot(sel_b, ones)
        key_ref[:, sl] = jnp.where(sel_j, pos, 0.0)
        offs = jnp.where(lane == j, run_sel, offs)
    offs_ref[...] = offs.astype(jnp.int32)


def _compact_kernel(cap, n_chunks, offs_ref, key_ref, aff_ref, idx_ref, gsel_ref):
    r = pl.program_id(0)
    n_cblk = cap // LANES
    slot = lax.broadcasted_iota(jnp.int32, (LANES, LANES), 0) + 1
    lane = lax.broadcasted_iota(jnp.int32, (LANES, LANES), 1)
    obase = r * LANES
    j0 = jnp.int32(0)
    for cb in range(n_cblk):
        want = (slot + cb * LANES).astype(F32)
        j0 = lax.while_loop(lambda j: (j < n_chunks - 1) & (offs_ref[obase + j] <= cb * LANES),
                            lambda j: j + 1, j0)
        j1 = lax.while_loop(lambda j: (j < n_chunks - 1) & (offs_ref[obase + j] < (cb + 1) * LANES),
                            lambda j: j + 1, j0)

        def per_chunk(j, carry):
            tok_sel, g_sel = carry
            start = pl.multiple_of(j * LANES, LANES)
            hit = key_ref[0, :, pl.ds(start, LANES)] == want
            tok_sel = jnp.where(hit, lane + j * LANES, tok_sel)
            g_sel = jnp.where(hit, aff_ref[0, :, pl.ds(start, LANES)], g_sel)
            return tok_sel, g_sel

        init = (jnp.zeros((LANES, LANES), jnp.int32), jnp.zeros((LANES, LANES), F32))
        tok_sel, g_sel = lax.fori_loop(j0, j1 + 1, per_chunk, init)
        idx_ref[0, cb:cb + 1, :] = jnp.sum(tok_sel.T, axis=0, keepdims=True)
        gsel_ref[0, cb:cb + 1, :] = jnp.sum(g_sel.T, axis=0, keepdims=True)
        j0 = j1


def _route(aff2d, cap):
    rows, seq = aff2d.shape
    n_chunks = seq // LANES
    n_cblk = cap // LANES
    assert n_chunks <= LANES
    key, offs = pl.pallas_call(
        functools.partial(_select_kernel, cap),
        out_shape=[jax.ShapeDtypeStruct((rows, seq), F32), jax.ShapeDtypeStruct((rows, LANES), jnp.int32)],
        compiler_params=pltpu.CompilerParams(vmem_limit_bytes=VMEM_LIMIT_BYTES),
        name="route_select",
    )(aff2d)
    row_blk = lambda r, offs: (r, 0, 0)
    grid_spec = pltpu.PrefetchScalarGridSpec(
        num_scalar_prefetch=1, grid=(rows,),
        in_specs=[pl.BlockSpec((1, 1, seq), row_blk), pl.BlockSpec((1, 1, seq), row_blk)],
        out_specs=[pl.BlockSpec((1, n_cblk, LANES), row_blk), pl.BlockSpec((1, n_cblk, LANES), row_blk)],
    )
    return pl.pallas_call(
        functools.partial(_compact_kernel, cap, n_chunks),
        grid_spec=grid_spec,
        out_shape=[jax.ShapeDtypeStruct((rows, n_cblk, LANES), jnp.int32),
                   jax.ShapeDtypeStruct((rows, n_cblk, LANES), F32)],
        compiler_params=pltpu.CompilerParams(dimension_semantics=("arbitrary",),
                                             vmem_limit_bytes=VMEM_LIMIT_BYTES),
        name="route_compact",
    )(offs.reshape(-1), key.reshape(rows, 1, seq), aff2d.reshape(rows, 1, seq))


def _gather_kernel(cap, idx_ref, hp_ref, xg_ref):
    b = pl.program_id(0)
    e = pl.program_id(1)
    base = (b * N_EXPERTS + e) * cap
    for c in range(cap):
        g, k = divmod(c, SUBLANES)
        xg_ref[0, 0, g, k:k + 1, :] = hp_ref[0, pl.ds(idx_ref[base + c], 1), :]


def _gather(idx_flat, hp, cap):
    batch, seq, half = hp.shape
    grid_spec = pltpu.PrefetchScalarGridSpec(
        num_scalar_prefetch=1,
        grid=(batch, N_EXPERTS),
        in_specs=[pl.BlockSpec((1, seq, half), lambda b, e, idx: (b, 0, 0), pipeline_mode=pl.Buffered(1))],
        out_specs=pl.BlockSpec((1, 1, cap // SUBLANES, SUBLANES, half), lambda b, e, idx: (b, e, 0, 0, 0)),
    )
    xg = pl.pallas_call(
        functools.partial(_gather_kernel, cap),
        grid_spec=grid_spec,
        out_shape=jax.ShapeDtypeStruct((batch, N_EXPERTS, cap // SUBLANES, SUBLANES, half), hp.dtype),
        compiler_params=pltpu.CompilerParams(dimension_semantics=("arbitrary", "arbitrary"),
                                             vmem_limit_bytes=VMEM_LIMIT_BYTES),
        name="gather",
    )(idx_flat, hp)
    return xg.reshape(batch, N_EXPERTS, cap, half)


def _experts_kernel(cap, tc, apply_norm, out_rows, idx_ref, gsel_ref, xg_ref, wg_ref, wu_ref, wd_ref, x2_ref, gf_ref,
                    o_hbm, acc_ref, y_ref, sem):
    b = pl.program_id(0)
    e = pl.program_id(1)
    base = (b * N_EXPERTS + e) * cap
    n_chunks = cap // tc

    @pl.when(e == 0)
    def _():
        acc_ref[...] = jnp.zeros_like(acc_ref)

    slab = x2_ref.shape[0]
    slab_rows = pl.ds(pl.multiple_of(e * slab, slab), slab)
    acc_ref[slab_rows, :] = acc_ref[slab_rows, :] + x2_ref[...]

    def ffn(ci):
        words = xg_ref[0, 0, ci * tc:(ci + 1) * tc, :]
        x = jnp.concatenate([_unpack_bf16_pair(words, 0), _unpack_bf16_pair(words, 1)], axis=-1)
        gate = _dot(x, wg_ref[0])
        up = _dot(x, wu_ref[0])
        act = (gate * jax.nn.sigmoid(gate) * up).astype(BF16)
        y_ref[ci % 2] = _dot(act, wd_ref[0])

    def scatter(ci):
        for r0 in range(0, tc, SUBLANES):
            cs = [base + ci * tc + r0 + k for k in range(SUBLANES)]
            toks = [idx_ref[c] for c in cs]
            rows = [acc_ref[pl.ds(tok, 1), :] + gsel_ref[c] * y_ref[ci % 2, r0 + k:r0 + k + 1, :]
                    for k, (c, tok) in enumerate(zip(cs, toks))]
            for tok, row in zip(toks, rows):
                acc_ref[pl.ds(tok, 1), :] = row

    for ci in range(n_chunks):
        ffn(ci)
        if ci > 0:
            scatter(ci - 1)
    scatter(n_chunks - 1)

    @pl.when(e == N_EXPERTS - 1)
    def _():
        n_tiles = acc_ref.shape[0] // out_rows

        def out_copy(i):
            rows = pl.ds(pl.multiple_of(i * out_rows, out_rows), out_rows)
            return rows, pltpu.make_async_copy(acc_ref.at[rows], o_hbm.at[b, rows], sem)

        def emit(i, _):
            rows, copy = out_copy(i)
            if apply_norm:
                acc_ref[rows, :] = _rms(acc_ref[rows, :], gf_ref[...])
            copy.start()
            return 0

        def drain(i, _):
            out_copy(i)[1].wait()
            return 0

        lax.fori_loop(0, n_tiles, emit, 0)
        lax.fori_loop(0, n_tiles, drain, 0)


def _experts(idx_flat, gsel_flat, xg, wg, wu, wd, x2, gf, apply_norm, seq, tc):
    batch, _, cap, half = xg.shape
    d = 2 * half
    ff = wg.shape[2]
    slab = seq // N_EXPERTS
    grid_spec = pltpu.PrefetchScalarGridSpec(
        num_scalar_prefetch=2,
        grid=(batch, N_EXPERTS),
        in_specs=[
            pl.BlockSpec((1, 1, cap, half), lambda b, e, idx, gs: (b, e, 0, 0)),
            pl.BlockSpec((1, d, ff), lambda b, e, idx, gs: (e, 0, 0)),
            pl.BlockSpec((1, d, ff), lambda b, e, idx, gs: (e, 0, 0)),
            pl.BlockSpec((1, ff, d), lambda b, e, idx, gs: (e, 0, 0)),
            pl.BlockSpec((slab, d), lambda b, e, idx, gs: (b * N_EXPERTS + e, 0)),
            pl.BlockSpec(gf.shape, lambda b, e, idx, gs: (0, 0)),
        ],
        out_specs=pl.BlockSpec(memory_space=pl.ANY),
        scratch_shapes=[pltpu.VMEM((seq, d), F32), pltpu.VMEM((2, tc, d), F32), pltpu.SemaphoreType.DMA],
    )
    return pl.pallas_call(
        functools.partial(_experts_kernel, cap, tc, apply_norm, min(seq, OUT_TILE_ROWS)),
        grid_spec=grid_spec,
        out_shape=jax.ShapeDtypeStruct((batch, seq, d), F32),
        compiler_params=pltpu.CompilerParams(dimension_semantics=("arbitrary", "arbitrary"),
                                             vmem_limit_bytes=EXPERTS_VMEM_LIMIT_BYTES),
        name="experts",
    )(idx_flat, gsel_flat, xg, wg, wu, wd, x2, gf)


def _rot_cols(w):
    half = w.shape[-1] // 2
    return jnp.concatenate([-w[..., half:], w[..., :half]], axis=-1)


def _tile(n, pref):
    return pref if n % pref == 0 else n


def kernel(x, mem, norm_mix_g, w_in, conv_w, conv_b, w_conv_out, q_norm_g, w_uq, kv_norm_g, w_ukv,
           w_mla_out, b_gate, w_mix_out, norm_mem_g, norm_memkv_g, w_mem_q, w_mem_kv, w_mem_out,
           norm_moe_g, w_router, w_exp_gate, w_exp_up, w_exp_down, norm_final_g):
    batch, seq, d = x.shape
    depth = w_in.shape[0]
    t = batch * seq
    q_lora = q_norm_g.shape[1]
    kv_lora = kv_norm_g.shape[1]
    cap = max(1, CAPACITY_FACTOR * seq // N_EXPERTS)
    assert cap % LANES == 0 and seq % LANES == 0 and d % (2 * LANES) == 0

    tm_in = _tile(seq, INPROJ_TILE)
    tm_post = _tile(seq, POST_TILE)
    tq = _tile(seq, ATTN_Q_TILE)
    tk = _tile(seq, ATTN_K_TILE)
    tc = _tile(cap, EXPERT_CHUNK)

    inv = 1.0 / (ROPE_THETA ** (np.arange(0, QK_ROPE, 2, dtype=np.float64) / QK_ROPE))
    ang = np.arange(seq, dtype=np.float64)[:, None] * inv[None, :]
    cos2 = np.concatenate([np.cos(ang), np.cos(ang)], axis=-1)
    sin2 = np.concatenate([np.sin(ang), np.sin(ang)], axis=-1)
    qf = (QK_DIM ** -0.5) * math.log2(math.e)
    kc, ks = jnp.asarray(cos2, F32), jnp.asarray(sin2, F32)
    qc, qs = jnp.asarray(qf * cos2.T, F32), jnp.asarray(qf * sin2.T, F32)

    o_cq = 3 * d
    o_ckv = o_cq + q_lora
    o_kr = o_ckv + kv_lora
    o_gl = o_kr + QK_ROPE
    cols = {"xc": 0, "gb": d, "gc": 2 * d, "cq": o_cq, "ckv": o_ckv}

    x2d = x.reshape(t, d)
    mem2d = mem.reshape(-1, d)
    row = lambda a: a.reshape(1, -1)

    for l in range(depth):
        win = w_in[l].astype(BF16)
        wgl = win[:, o_gl:]
        wkr = jnp.concatenate([win[:, o_kr:o_gl], _rot_cols(win[:, o_kr:o_gl])], axis=1)
        wq3 = w_uq[l].reshape(q_lora, MLA_HEADS, QK_DIM)
        wq = wq3.transpose(1, 2, 0).astype(BF16)
        wqr = _rot_cols(wq3[..., QK_NOPE:]).transpose(1, 2, 0).astype(BF16)
        wkv3 = w_ukv[l].reshape(kv_lora, MLA_HEADS, QK_NOPE + V_HEAD)
        wk = wkv3[..., :QK_NOPE].transpose(1, 0, 2).astype(BF16)
        wv = wkv3[..., QK_NOPE:].transpose(1, 2, 0).astype(BF16)

        wr_hi = w_router[l].T.astype(BF16)
        wr_lo = (w_router[l].T - wr_hi.astype(F32)).astype(BF16)

        v, gb, gates, qt, k, vt = _inproj(
            x2d, batch, seq, row(norm_mix_g[l]), win, wgl, wkr, cols, row(q_norm_g[l]), row(kv_norm_g[l]),
            row(b_gate[l]), wq, wqr, wk, wv, qc, qs, kc, ks, qf, tm_in)
        w_exp = [w_exp_gate[l], w_exp_up[l], w_exp_down[l]]
        w_flat = [w.reshape(-1, w.shape[-1]) for w in w_exp]
        n_attn_steps = batch * MLA_HEADS * (seq // tq)
        fused_cast = all(_side_cast_fits(w, n_attn_steps) for w in w_flat)
        att, w_bf = _attention(qt, k, vt, tq, tk, w_flat if fused_cast else [])
        att = att.reshape(t, MLA_HEADS * V_HEAD)
        wgb, wub, wdb = ([wb.reshape(w.shape) for wb, w in zip(w_bf, w_exp)] if fused_cast
                         else [w.astype(BF16) for w in w_exp])
        mk, mv = _memkv(mem2d, row(norm_memkv_g[l]), w_mem_kv[l].astype(BF16))
        x2, hp, aff = _post(
            x2d, batch, seq, v, gb, gates, att, conv_w[l], row(conv_b[l]), w_conv_out[l].astype(BF16),
            w_mla_out[l].astype(BF16), w_mix_out[l].astype(BF16), row(norm_mem_g[l]),
            w_mem_q[l].astype(BF16), mk, mv, w_mem_out[l].astype(BF16), row(norm_moe_g[l]),
            wr_hi, wr_lo, tm_post)
        idx, gsel = _route(aff.reshape(batch * N_EXPERTS, seq), cap)
        idx_flat = idx.reshape(-1)
        xg = _gather(idx_flat, hp.reshape(batch, seq, d // 2), cap)
        x2d = _experts(idx_flat, gsel.reshape(-1), xg, wgb, wub, wdb, x2, row(norm_final_g),
                       l == depth - 1, seq, tc).reshape(t, d)
    return x2d.reshape(batch, seq, d)
```

```python
import functools
import math

import jax
import jax.numpy as jnp
import numpy as np
from jax import lax
from jax.experimental import pallas as pl
from jax.experimental.pallas import tpu as pltpu

MLA_HEADS = 8
QK_NOPE = 128
QK_ROPE = 64
QK_DIM = QK_NOPE + QK_ROPE
V_HEAD = 128
V_AUG = V_HEAD + 16
ROPE_THETA = 10000.0
MEM_HEADS = 4
N_EXPERTS = 16
CAPACITY_FACTOR = 2
EPS = 1e-6

LANES = 128
SUBLANES = 8
BF16_SUBLANES = 16
VMEM_LIMIT_BYTES = 56 * 1024 * 1024
EXPERTS_VMEM_LIMIT_BYTES = 60 * 1024 * 1024

INPROJ_TILE = 512
POST_TILE = 512
POST_SUBTILE = 256
ATTN_Q_TILE = 1024
ATTN_K_TILE = 512
ATTN_SCORE_SLOTS = 3
ATTN_BODY_ROUNDS = 2
EXPERT_CHUNK = 256
OUT_TILE_ROWS = 256
SIDE_CAST_SLICE_BYTES = 1024 * 1024

F32 = jnp.float32
BF16 = jnp.bfloat16


def _const_spec(shape):
    nd = len(shape)
    return pl.BlockSpec(shape, lambda *_: (0,) * nd, pipeline_mode=pl.Buffered(1))


def _rms(x, g):
    return x * lax.rsqrt(jnp.mean(x * x, axis=-1, keepdims=True) + EPS) * g


def _dot(a, b):
    return jnp.dot(a, b, preferred_element_type=F32)


def _dot_t(a, b):
    return lax.dot_general(a, b, (((1,), (1,)), ((), ())), preferred_element_type=F32)


def _inproj_kernel(cols, q_scale, x_ref, g_ref, win_ref, wgl_ref, wkr_ref, qg_ref, kvg_ref, bg_ref,
                   wq_ref, wqr_ref, wk_ref, wv_ref, qc_ref, qs_ref, kc_ref, ks_ref,
                   v_ref, gb_ref, gates_ref, qt_ref, k_ref, vt_ref):
    d = x_ref.shape[1]
    hb = _rms(x_ref[...], g_ref[...]).astype(BF16)

    def proj(name, width):
        lo = cols[name]
        return _dot(hb, win_ref[:, lo:lo + width])

    v_ref[...] = (proj("gc", d) * proj("xc", d)).astype(BF16)
    gb_ref[...] = proj("gb", d).astype(BF16)
    gates_ref[...] = jax.nn.sigmoid(_dot(hb, wgl_ref[...]) + bg_ref[...]).astype(BF16)

    q_lora = qg_ref.shape[1]
    kv_lora = kvg_ref.shape[1]
    qn = _rms(proj("cq", q_lora), qg_ref[...]).astype(BF16)
    kvn = _rms(proj("ckv", kv_lora), kvg_ref[...]).astype(BF16)
    kr2 = _dot(hb, wkr_ref[...])
    k_rope = kr2[:, :QK_ROPE] * kc_ref[...] + kr2[:, QK_ROPE:] * ks_ref[...]
    qc = qc_ref[...]
    qs = qs_ref[...]
    pad_row = lax.broadcasted_iota(jnp.int32, (V_AUG - V_HEAD, x_ref.shape[0]), 0)
    ones_rows = jnp.where(pad_row == 0, 1.0, 0.0).astype(BF16)
    for h in range(MLA_HEADS):
        qt_h = _dot_t(wq_ref[h], qn)
        rope = qt_h[QK_NOPE:] * qc + _dot_t(wqr_ref[h], qn) * qs
        qt_ref[0, h] = jnp.concatenate([qt_h[:QK_NOPE] * q_scale, rope], axis=0).astype(BF16)
        k_ref[0, h] = jnp.concatenate([_dot(kvn, wk_ref[h]), k_rope], axis=-1).astype(BF16)
        vt_ref[0, h, :V_HEAD, :] = _dot_t(wv_ref[h], kvn).astype(BF16)
        vt_ref[0, h, V_HEAD:, :] = ones_rows


def _inproj(x2d, batch, seq, g, win, wgl, wkr, cols, qg, kvg, bg, wq, wqr, wk, wv, qc, qs, kc, ks, q_scale, tm):
    t, d = x2d.shape
    nb = seq // tm
    tok = lambda i: (i, 0)
    pos = lambda i: (i % nb, 0)
    pos_t = lambda i: (0, i % nb)
    head_blk = lambda i: (i // nb, 0, i % nb, 0)
    head_blk_t = lambda i: (i // nb, 0, 0, i % nb)
    in_specs = [
        pl.BlockSpec((tm, d), tok),
        _const_spec(g.shape), _const_spec(win.shape), _const_spec(wgl.shape), _const_spec(wkr.shape),
        _const_spec(qg.shape), _const_spec(kvg.shape),
        _const_spec(bg.shape), _const_spec(wq.shape), _const_spec(wqr.shape), _const_spec(wk.shape),
        _const_spec(wv.shape),
        pl.BlockSpec((QK_ROPE, tm), pos_t), pl.BlockSpec((QK_ROPE, tm), pos_t),
        pl.BlockSpec((tm, QK_ROPE), pos), pl.BlockSpec((tm, QK_ROPE), pos),
    ]
    out_shape = [
        jax.ShapeDtypeStruct((t, d), BF16),
        jax.ShapeDtypeStruct((t, d), BF16),
        jax.ShapeDtypeStruct((t, 2 * d), BF16),
        jax.ShapeDtypeStruct((batch, MLA_HEADS, QK_DIM, seq), BF16),
        jax.ShapeDtypeStruct((batch, MLA_HEADS, seq, QK_DIM), BF16),
        jax.ShapeDtypeStruct((batch, MLA_HEADS, V_AUG, seq), BF16),
    ]
    out_specs = [
        pl.BlockSpec((tm, d), tok), pl.BlockSpec((tm, d), tok), pl.BlockSpec((tm, 2 * d), tok),
        pl.BlockSpec((1, MLA_HEADS, QK_DIM, tm), head_blk_t),
        pl.BlockSpec((1, MLA_HEADS, tm, QK_DIM), head_blk),
        pl.BlockSpec((1, MLA_HEADS, V_AUG, tm), head_blk_t),
    ]
    return pl.pallas_call(
        functools.partial(_inproj_kernel, cols, q_scale),
        grid=(t // tm,), in_specs=in_specs, out_specs=out_specs, out_shape=out_shape,
        compiler_params=pltpu.CompilerParams(dimension_semantics=("arbitrary",),
                                             vmem_limit_bytes=VMEM_LIMIT_BYTES),
        name="inproj",
    )(x2d, g, win, wgl, wkr, qg, kvg, bg, wq, wqr, wk, wv, qc, qs, kc, ks)


def _attn_kernel(tk, n_slots, n_cast, qt_ref, k_ref, vt_ref, *refs):
    cast_in, o_ref, cast_out, s_ref = refs[:n_cast], refs[n_cast], refs[n_cast + 1:2 * n_cast + 1], refs[-1]
    for src, dst in zip(cast_in, cast_out):
        dst[...] = src[...].astype(BF16)
    qt = qt_ref[0, 0]
    tq = qt.shape[1]
    nk = k_ref.shape[2] // tk

    def scores(j, slot):
        start = pl.multiple_of(j * tk, tk)
        s = _dot(k_ref[0, 0, pl.ds(start, tk), :], qt)
        s_ref[slot] = s
        return jnp.max(s, axis=0, keepdims=True)

    def update(j, slot, mx, m, acc):
        start = pl.multiple_of(j * tk, tk)
        m_new = jnp.maximum(m, mx)
        p = jnp.exp2(s_ref[slot] - m_new)
        acc = jnp.exp2(m - m_new) * acc + _dot(vt_ref[0, 0, :, pl.ds(start, tk)], p.astype(BF16))
        return m_new, acc

    ahead = n_slots - 1

    def chunks(j0, count, pending, m, acc):
        pending = list(pending)
        for u in range(count):
            if isinstance(j0, int) and j0 + u + ahead >= nk:
                pass
            else:
                pending.append(scores(j0 + u + ahead, (u + ahead) % n_slots))
            m, acc = update(j0 + u, u % n_slots, pending.pop(0), m, acc)
        return tuple(pending), m, acc

    body = n_slots * ATTN_BODY_ROUNDS
    n_iter = max(nk - ahead, 0) // body
    pending = tuple(scores(j, j) for j in range(min(ahead, nk)))
    m0 = jnp.full((1, tq), -jnp.inf, F32)
    acc0 = jnp.zeros((vt_ref.shape[2], tq), F32)
    pending, m, acc = lax.fori_loop(0, n_iter, lambda i, c: chunks(body * i, body, *c),
                                    (pending, m0, acc0))
    _, _, acc = chunks(body * n_iter, nk - body * n_iter, pending, m, acc)
    o_ref[0] = (acc[:V_HEAD] / acc[V_HEAD:V_HEAD + 1]).T.astype(BF16)


def _side_cast_fits(w, n_steps):
    rows, cols = w.shape
    slice_rows = rows // n_steps
    return (rows % n_steps == 0 and slice_rows % BF16_SUBLANES == 0
            and slice_rows * cols * 4 <= SIDE_CAST_SLICE_BYTES)


def _attention(qt, k, vt, tq, tk, side_casts):
    batch, heads, seq, _ = k.shape
    nk = seq // tk
    nq = seq // tq
    n_steps = batch * heads * nq
    step = lambda b, h, i: ((b * heads + h) * nq + i, 0)
    cast_specs = [pl.BlockSpec((w.shape[0] // n_steps, w.shape[1]), step) for w in side_casts]
    outs = pl.pallas_call(
        functools.partial(_attn_kernel, tk, ATTN_SCORE_SLOTS, len(side_casts)),
        grid=(batch, heads, nq),
        in_specs=[
            pl.BlockSpec((1, 1, QK_DIM, tq), lambda b, h, i: (b, h, 0, i)),
            pl.BlockSpec((1, 1, seq, QK_DIM), lambda b, h, i: (b, h, 0, 0)),
            pl.BlockSpec((1, 1, V_AUG, seq), lambda b, h, i: (b, h, 0, 0)),
        ] + cast_specs,
        out_specs=[pl.BlockSpec((1, tq, V_HEAD), lambda b, h, i: (b, i, h))] + cast_specs,
        out_shape=[jax.ShapeDtypeStruct((batch, seq, heads * V_HEAD), BF16)]
                  + [jax.ShapeDtypeStruct(w.shape, BF16) for w in side_casts],
        scratch_shapes=[pltpu.VMEM((ATTN_SCORE_SLOTS, tk, tq), F32)],
        compiler_params=pltpu.CompilerParams(
            dimension_semantics=("arbitrary", "arbitrary", "arbitrary"),
            vmem_limit_bytes=VMEM_LIMIT_BYTES),
        name="mla_attention",
    )(qt, k, vt, *side_casts)
    return outs[0], outs[1:]


def _memkv_kernel(m_ref, g_ref, w_ref, k_ref, v_ref):
    d = m_ref.shape[1]
    kv = _dot(_rms(m_ref[...], g_ref[...]).astype(BF16), w_ref[...])
    k_ref[...] = kv[:, :d].astype(BF16)
    v_ref[...] = kv[:, d:].astype(BF16)


def _memkv(mem2d, g, w):
    n, d = mem2d.shape
    return pl.pallas_call(
        _memkv_kernel,
        out_shape=[jax.ShapeDtypeStruct((n, d), BF16), jax.ShapeDtypeStruct((n, d), BF16)],
        compiler_params=pltpu.CompilerParams(vmem_limit_bytes=VMEM_LIMIT_BYTES),
        name="memkv",
    )(mem2d, g, w)


def _pack_bf16_pairs(lo, hi):
    return pltpu.bitcast(pltpu.pack_elementwise([lo, hi], packed_dtype=BF16), jnp.uint32)


def _unpack_bf16_pair(words, index):
    return pltpu.unpack_elementwise(words, index=index, packed_dtype=BF16, unpacked_dtype=F32).astype(BF16)


def _post_kernel(nb, sub, x_ref, v_ref, vprev_ref, vnext_ref, gb_ref, gates_ref, att_ref,
                 cw_ref, cb_ref, wco_ref, wmo_ref, wmix_ref, gmem_ref, wmq_ref, mk_ref, mv_ref, wmout_ref,
                 gmoe_ref, wrh_ref, wrl_ref,
                 x2_ref, hp_ref, aff_ref):
    i = pl.program_id(0)
    tm, d = x_ref.shape
    v = v_ref[...].astype(F32)
    first = (i % nb) == 0
    last = (i % nb) == nb - 1
    halo_prev = jnp.where(first, 0.0, vprev_ref[...].astype(F32)[BF16_SUBLANES - 1:BF16_SUBLANES, :])
    halo_next = jnp.where(last, 0.0, vnext_ref[...].astype(F32)[0:1, :])
    row = lax.broadcasted_iota(jnp.int32, (tm, d), 0)
    v_prev = jnp.where(row == 0, halo_prev, pltpu.roll(v, 1, 0))
    v_next = jnp.where(row == tm - 1, halo_next, pltpu.roll(v, tm - 1, 0))
    conv = cw_ref[0:1, :] * v_prev + cw_ref[1:2, :] * v + cw_ref[2:3, :] * v_next + cb_ref[...]
    conv_in = (gb_ref[...].astype(F32) * conv).astype(BF16)
    hd = d // MEM_HEADS
    tiles = [pl.ds(r0, sub) for r0 in range(0, tm, sub)]
    y_conv = [_dot(conv_in[r0:r0 + sub], wco_ref[...]) for r0 in range(0, tm, sub)]
    y_mla = [_dot(att_ref[rows, :], wmo_ref[...]) for rows in tiles]
    mixed = [(gates_ref[rows, :d].astype(F32) * yc + gates_ref[rows, d:].astype(F32) * ym).astype(BF16)
             for rows, yc, ym in zip(tiles, y_conv, y_mla)]
    x1 = [x_ref[rows, :] + _dot(mx, wmix_ref[...]) for rows, mx in zip(tiles, mixed)]
    qm = [(_dot((x * gmem_ref[...]).astype(BF16), wmq_ref[...])
           * (lax.rsqrt(jnp.mean(x * x, axis=-1, keepdims=True) + EPS) * (hd ** -0.5))).astype(BF16) for x in x1]
    att_m = []
    for q in qm:
        outs = []
        for h in range(MEM_HEADS):
            sl = slice(h * hd, (h + 1) * hd)
            s = _dot_t(q[:, sl], mk_ref[:, sl])
            p = jnp.exp(s - jnp.max(s, axis=-1, keepdims=True))
            inv_l = 1.0 / jnp.sum(p, axis=-1, keepdims=True)
            outs.append((_dot(p.astype(BF16), mv_ref[:, sl]) * inv_l).astype(BF16))
        att_m.append(jnp.concatenate(outs, axis=-1))
    x2 = [x + _dot(o, wmout_ref[...]) for x, o in zip(x1, att_m)]
    for rows, x in zip(tiles, x2):
        x2_ref[rows, :] = x
        h3 = _rms(x, gmoe_ref[...])
        hp_ref[rows, :] = _pack_bf16_pairs(h3[:, :d // 2], h3[:, d // 2:])
        h_hi = h3.astype(BF16)
        h_lo = (h3 - h_hi.astype(F32)).astype(BF16)
        logits = (_dot_t(wrh_ref[...], h_hi) + _dot_t(wrh_ref[...], h_lo)
                  + _dot_t(wrl_ref[...], h_hi) + _dot_t(wrl_ref[...], h_lo))
        e = jnp.exp(logits - jnp.max(logits, axis=0, keepdims=True))
        aff_ref[0, :, rows] = e / jnp.sum(e, axis=0, keepdims=True)


def _post(x2d, batch, seq, v, gb, gates, att, cw, cb, wco, wmo, wmix, gmem, wmq, mk, mv, wmout, gmoe, wrh, wrl, tm):
    t, d = x2d.shape
    nb = seq // tm
    hb = tm // BF16_SUBLANES
    n_halo = t // BF16_SUBLANES
    tok = lambda i: (i, 0)
    mem_len = mk.shape[0] // batch
    in_specs = [
        pl.BlockSpec((tm, d), tok), pl.BlockSpec((tm, d), tok),
        pl.BlockSpec((BF16_SUBLANES, d), lambda i: (jnp.maximum(i * hb - 1, 0), 0)),
        pl.BlockSpec((BF16_SUBLANES, d), lambda i: (jnp.minimum((i + 1) * hb, n_halo - 1), 0)),
        pl.BlockSpec((tm, d), tok), pl.BlockSpec((tm, 2 * d), tok), pl.BlockSpec((tm, d), tok),
        _const_spec(cw.shape), _const_spec(cb.shape), _const_spec(wco.shape), _const_spec(wmo.shape),
        _const_spec(wmix.shape), _const_spec(gmem.shape), _const_spec(wmq.shape),
        pl.BlockSpec((mem_len, d), lambda i: (i // nb, 0)),
        pl.BlockSpec((mem_len, d), lambda i: (i // nb, 0)),
        _const_spec(wmout.shape), _const_spec(gmoe.shape), _const_spec(wrh.shape), _const_spec(wrl.shape),
    ]
    out_shape = [
        jax.ShapeDtypeStruct((t, d), F32),
        jax.ShapeDtypeStruct((t, d // 2), jnp.uint32),
        jax.ShapeDtypeStruct((batch, N_EXPERTS, seq), F32),
    ]
    out_specs = [
        pl.BlockSpec((tm, d), tok), pl.BlockSpec((tm, d // 2), tok),
        pl.BlockSpec((1, N_EXPERTS, tm), lambda i: (i // nb, 0, i % nb)),
    ]
    return pl.pallas_call(
        functools.partial(_post_kernel, nb, min(tm, POST_SUBTILE)),
        grid=(t // tm,), in_specs=in_specs, out_specs=out_specs, out_shape=out_shape,
        compiler_params=pltpu.CompilerParams(dimension_semantics=("arbitrary",),
                                             vmem_limit_bytes=VMEM_LIMIT_BYTES),
        name="post",
    )(x2d, v, v, v, gb, gates, att, cw, cb, wco, wmo, wmix, gmem, wmq, mk, mv, wmout, gmoe, wrh, wrl)


def _select_kernel(cap, aff_ref, key_ref, offs_ref):
    rows, seq = aff_ref.shape
    n_chunks = seq // LANES
    aff = aff_ref[...]

    def count_ge(x):
        return jnp.sum((aff >= x).astype(F32), axis=-1, keepdims=True)

    def search(b, t):
        cand = t | (jnp.int32(1) << (30 - b))
        return jnp.where(count_ge(pltpu.bitcast(cand, F32)) >= cap, cand, t)

    thr = lax.fori_loop(0, 31, search, jnp.zeros((rows, 1), jnp.int32))

    def refine(_, lohi):
        lo, hi = lohi
        mid = 0.5 * (lo + hi)
        take = count_ge(mid) >= cap
        return jnp.where(take, mid, lo), jnp.where(take, hi, mid)

    lo, hi = lax.fori_loop(0, 30, refine, (pltpu.bitcast(thr, F32), pltpu.bitcast(thr + 1, F32)))
    gt = aff >= hi
    eq = (aff >= lo) & (aff < hi)
    need = cap - jnp.sum(gt.astype(F32), axis=-1, keepdims=True)

    tri = (lax.broadcasted_iota(jnp.int32, (LANES, LANES), 0)
           <= lax.broadcasted_iota(jnp.int32, (LANES, LANES), 1)).astype(BF16)
    ones = jnp.ones((LANES, LANES), BF16)
    lane = lax.broadcasted_iota(jnp.int32, (rows, LANES), 1)
    run_eq = jnp.zeros((rows, LANES), F32)
    run_sel = jnp.zeros((rows, LANES), F32)
    offs = jnp.zeros((rows, LANES), F32)
    for j in range(n_chunks):
        sl = slice(j * LANES, (j + 1) * LANES)
        eq_j = eq[:, sl]
        eq_b = eq_j.astype(F32).astype(BF16)
        eq_rank = _dot(eq_b, tri) + run_eq
        run_eq = run_eq + _dot(eq_b, ones)
        sel_j = gt[:, sl] | (eq_j & (eq_rank <= need))
        sel_b = sel_j.astype(F32).astype(BF16)
        pos = _dot(sel_b, tri) + run_sel
        run_sel = run_sel + _dot(sel_b, ones)
        key_ref[:, sl] = jnp.where(sel_j, pos, 0.0)
        offs = jnp.where(lane == j, run_sel, offs)
    offs_ref[...] = offs.astype(jnp.int32)


def _compact_kernel(cap, n_chunks, offs_ref, key_ref, aff_ref, idx_ref, gsel_ref):
    r = pl.program_id(0)
    n_cblk = cap // LANES
    slot = lax.broadcasted_iota(jnp.int32, (LANES, LANES), 0) + 1
    lane = lax.broadcasted_iota(jnp.int32, (LANES, LANES), 1)
    obase = r * LANES
    j0 = jnp.int32(0)
    for cb in range(n_cblk):
        want = (slot + cb * LANES).astype(F32)
        j0 = lax.while_loop(lambda j: (j < n_chunks - 1) & (offs_ref[obase + j] <= cb * LANES),
                            lambda j: j + 1, j0)
        j1 = lax.while_loop(lambda j: (j < n_chunks - 1) & (offs_ref[obase + j] < (cb + 1) * LANES),
                            lambda j: j + 1, j0)

        def per_chunk(j, carry):
            tok_sel, g_sel = carry
            start = pl.multiple_of(j * LANES, LANES)
            hit = key_ref[0, :, pl.ds(start, LANES)] == want
            tok_sel = jnp.where(hit, lane + j * LANES, tok_sel)
            g_sel = jnp.where(hit, aff_ref[0, :, pl.ds(start, LANES)], g_sel)
            return tok_sel, g_sel

        init = (jnp.zeros((LANES, LANES), jnp.int32), jnp.zeros((LANES, LANES), F32))
        tok_sel, g_sel = lax.fori_loop(j0, j1 + 1, per_chunk, init)
        idx_ref[0, cb:cb + 1, :] = jnp.sum(tok_sel.T, axis=0, keepdims=True)
        gsel_ref[0, cb:cb + 1, :] = jnp.sum(g_sel.T, axis=0, keepdims=True)
        j0 = j1


def _route(aff2d, cap):
    rows, seq = aff2d.shape
    n_chunks = seq // LANES
    n_cblk = cap // LANES
    assert n_chunks <= LANES
    key, offs = pl.pallas_call(
        functools.partial(_select_kernel, cap),
        out_shape=[jax.ShapeDtypeStruct((rows, seq), F32), jax.ShapeDtypeStruct((rows, LANES), jnp.int32)],
        compiler_params=pltpu.CompilerParams(vmem_limit_bytes=VMEM_LIMIT_BYTES),
        name="route_select",
    )(aff2d)
    row_blk = lambda r, offs: (r, 0, 0)
    grid_spec = pltpu.PrefetchScalarGridSpec(
        num_scalar_prefetch=1, grid=(rows,),
        in_specs=[pl.BlockSpec((1, 1, seq), row_blk), pl.BlockSpec((1, 1, seq), row_blk)],
        out_specs=[pl.BlockSpec((1, n_cblk, LANES), row_blk), pl.BlockSpec((1, n_cblk, LANES), row_blk)],
    )
    return pl.pallas_call(
        functools.partial(_compact_kernel, cap, n_chunks),
        grid_spec=grid_spec,
        out_shape=[jax.ShapeDtypeStruct((rows, n_cblk, LANES), jnp.int32),
                   jax.ShapeDtypeStruct((rows, n_cblk, LANES), F32)],
        compiler_params=pltpu.CompilerParams(dimension_semantics=("arbitrary",),
                                             vmem_limit_bytes=VMEM_LIMIT_BYTES),
        name="route_compact",
    )(offs.reshape(-1), key.reshape(rows, 1, seq), aff2d.reshape(rows, 1, seq))


def _gather_kernel(cap, idx_ref, hp_ref, xg_ref):
    b = pl.program_id(0)
    e = pl.program_id(1)
    base = (b * N_EXPERTS + e) * cap
    for c in range(cap):
        g, k = divmod(c, SUBLANES)
        xg_ref[0, 0, g, k:k + 1, :] = hp_ref[0, pl.ds(idx_ref[base + c], 1), :]


def _gather(idx_flat, hp, cap):
    batch, seq, half = hp.shape
    grid_spec = pltpu.PrefetchScalarGridSpec(
        num_scalar_prefetch=1,
        grid=(batch, N_EXPERTS),
        in_specs=[pl.BlockSpec((1, seq, half), lambda b, e, idx: (b, 0, 0), pipeline_mode=pl.Buffered(1))],
        out_specs=pl.BlockSpec((1, 1, cap // SUBLANES, SUBLANES, half), lambda b, e, idx: (b, e, 0, 0, 0)),
    )
    xg = pl.pallas_call(
        functools.partial(_gather_kernel, cap),
        grid_spec=grid_spec,
        out_shape=jax.ShapeDtypeStruct((batch, N_EXPERTS, cap // SUBLANES, SUBLANES, half), hp.dtype),
        compiler_params=pltpu.CompilerParams(dimension_semantics=("arbitrary", "arbitrary"),
                                             vmem_limit_bytes=VMEM_LIMIT_BYTES),
        name="gather",
    )(idx_flat, hp)
    return xg.reshape(batch, N_EXPERTS, cap, half)


def _experts_kernel(cap, tc, apply_norm, out_rows, idx_ref, xg_ref, gsel_ref, wg_ref, wu_ref, wd_ref, x2_ref, gf_ref,
                    o_hbm, acc_ref, y_ref, sem):
    b = pl.program_id(0)
    e = pl.program_id(1)
    base = (b * N_EXPERTS + e) * cap
    n_chunks = cap // tc

    @pl.when(e == 0)
    def _():
        acc_ref[...] = jnp.zeros_like(acc_ref)

    slab = x2_ref.shape[0]
    slab_rows = pl.ds(pl.multiple_of(e * slab, slab), slab)
    acc_ref[slab_rows, :] = acc_ref[slab_rows, :] + x2_ref[...]

    diag = (lax.broadcasted_iota(jnp.int32, (LANES, LANES), 0)
            == lax.broadcasted_iota(jnp.int32, (LANES, LANES), 1))

    def gate_column(ci):
        cols = []
        for blk in range(ci * tc // LANES, (ci + 1) * tc // LANES):
            g_row = gsel_ref[0, blk:blk + 1, :]
            cols.append(jnp.sum(jnp.where(diag, g_row, 0.0), axis=1, keepdims=True))
        return jnp.concatenate(cols, axis=0)

    def ffn(ci):
        words = xg_ref[0, 0, ci * tc:(ci + 1) * tc, :]
        x = jnp.concatenate([_unpack_bf16_pair(words, 0), _unpack_bf16_pair(words, 1)], axis=-1)
        gate = _dot(x, wg_ref[0])
        up = _dot(x, wu_ref[0])
        act = (gate * jax.nn.sigmoid(gate) * up * gate_column(ci)).astype(BF16)
        y_ref[ci % 2] = _dot(act, wd_ref[0])

    def scatter(ci):
        for r0 in range(0, tc, SUBLANES):
            toks = [idx_ref[base + ci * tc + r0 + k] for k in range(SUBLANES)]
            rows = [acc_ref[pl.ds(tok, 1), :] + y_ref[ci % 2, r0 + k:r0 + k + 1, :]
                    for k, tok in enumerate(toks)]
            for tok, row in zip(toks, rows):
                acc_ref[pl.ds(tok, 1), :] = row

    for ci in range(n_chunks):
        ffn(ci)
        if ci > 0:
            scatter(ci - 1)
    scatter(n_chunks - 1)

    @pl.when(e == N_EXPERTS - 1)
    def _():
        n_tiles = acc_ref.shape[0] // out_rows

        def out_copy(i):
            rows = pl.ds(pl.multiple_of(i * out_rows, out_rows), out_rows)
            return rows, pltpu.make_async_copy(acc_ref.at[rows], o_hbm.at[b, rows], sem)

        def emit(i, _):
            rows, copy = out_copy(i)
            if apply_norm:
                acc_ref[rows, :] = _rms(acc_ref[rows, :], gf_ref[...])
            copy.start()
            return 0

        def drain(i, _):
            out_copy(i)[1].wait()
            return 0

        lax.fori_loop(0, n_tiles, emit, 0)
        lax.fori_loop(0, n_tiles, drain, 0)


def _experts(idx_flat, gsel, xg, wg, wu, wd, x2, gf, apply_norm, seq, tc):
    batch, _, cap, half = xg.shape
    d = 2 * half
    ff = wg.shape[2]
    slab = seq // N_EXPERTS
    assert tc % LANES == 0 and gsel.shape == (batch * N_EXPERTS, cap // LANES, LANES)
    grid_spec = pltpu.PrefetchScalarGridSpec(
        num_scalar_prefetch=1,
        grid=(batch, N_EXPERTS),
        in_specs=[
            pl.BlockSpec((1, 1, cap, half), lambda b, e, idx: (b, e, 0, 0)),
            pl.BlockSpec((1, cap // LANES, LANES), lambda b, e, idx: (b * N_EXPERTS + e, 0, 0)),
            pl.BlockSpec((1, d, ff), lambda b, e, idx: (e, 0, 0)),
            pl.BlockSpec((1, d, ff), lambda b, e, idx: (e, 0, 0)),
            pl.BlockSpec((1, ff, d), lambda b, e, idx: (e, 0, 0)),
            pl.BlockSpec((slab, d), lambda b, e, idx: (b * N_EXPERTS + e, 0)),
            pl.BlockSpec(gf.shape, lambda b, e, idx: (0, 0)),
        ],
        out_specs=pl.BlockSpec(memory_space=pl.ANY),
        scratch_shapes=[pltpu.VMEM((seq, d), F32), pltpu.VMEM((2, tc, d), F32), pltpu.SemaphoreType.DMA],
    )
    return pl.pallas_call(
        functools.partial(_experts_kernel, cap, tc, apply_norm, min(seq, OUT_TILE_ROWS)),
        grid_spec=grid_spec,
        out_shape=jax.ShapeDtypeStruct((batch, seq, d), F32),
        compiler_params=pltpu.CompilerParams(dimension_semantics=("arbitrary", "arbitrary"),
                                             vmem_limit_bytes=EXPERTS_VMEM_LIMIT_BYTES),
        name="experts",
    )(idx_flat, xg, gsel, wg, wu, wd, x2, gf)


def _rot_cols(w):
    half = w.shape[-1] // 2
    return jnp.concatenate([-w[..., half:], w[..., :half]], axis=-1)


def _tile(n, pref):
    return pref if n % pref == 0 else n


def kernel(x, mem, norm_mix_g, w_in, conv_w, conv_b, w_conv_out, q_norm_g, w_uq, kv_norm_g, w_ukv,
           w_mla_out, b_gate, w_mix_out, norm_mem_g, norm_memkv_g, w_mem_q, w_mem_kv, w_mem_out,
           norm_moe_g, w_router, w_exp_gate, w_exp_up, w_exp_down, norm_final_g):
    batch, seq, d = x.shape
    depth = w_in.shape[0]
    t = batch * seq
    q_lora = q_norm_g.shape[1]
    kv_lora = kv_norm_g.shape[1]
    cap = max(1, CAPACITY_FACTOR * seq // N_EXPERTS)
    assert cap % LANES == 0 and seq % LANES == 0 and d % (2 * LANES) == 0

    tm_in = _tile(seq, INPROJ_TILE)
    tm_post = _tile(seq, POST_TILE)
    tq = _tile(seq, ATTN_Q_TILE)
    tk = _tile(seq, ATTN_K_TILE)
    tc = _tile(cap, EXPERT_CHUNK)

    inv = 1.0 / (ROPE_THETA ** (np.arange(0, QK_ROPE, 2, dtype=np.float64) / QK_ROPE))
    ang = np.arange(seq, dtype=np.float64)[:, None] * inv[None, :]
    cos2 = np.concatenate([np.cos(ang), np.cos(ang)], axis=-1)
    sin2 = np.concatenate([np.sin(ang), np.sin(ang)], axis=-1)
    qf = (QK_DIM ** -0.5) * math.log2(math.e)
    kc, ks = jnp.asarray(cos2, F32), jnp.asarray(sin2, F32)
    qc, qs = jnp.asarray(qf * cos2.T, F32), jnp.asarray(qf * sin2.T, F32)

    o_cq = 3 * d
    o_ckv = o_cq + q_lora
    o_kr = o_ckv + kv_lora
    o_gl = o_kr + QK_ROPE
    cols = {"xc": 0, "gb": d, "gc": 2 * d, "cq": o_cq, "ckv": o_ckv}

    x2d = x.reshape(t, d)
    mem2d = mem.reshape(-1, d)
    row = lambda a: a.reshape(1, -1)

    for l in range(depth):
        win = w_in[l].astype(BF16)
        wgl = win[:, o_gl:]
        wkr = jnp.concatenate([win[:, o_kr:o_gl], _rot_cols(win[:, o_kr:o_gl])], axis=1)
        wq3 = w_uq[l].reshape(q_lora, MLA_HEADS, QK_DIM)
        wq = wq3.transpose(1, 2, 0).astype(BF16)
        wqr = _rot_cols(wq3[..., QK_NOPE:]).transpose(1, 2, 0).astype(BF16)
        wkv3 = w_ukv[l].reshape(kv_lora, MLA_HEADS, QK_NOPE + V_HEAD)
        wk = wkv3[..., :QK_NOPE].transpose(1, 0, 2).astype(BF16)
        wv = wkv3[..., QK_NOPE:].transpose(1, 2, 0).astype(BF16)

        wr_hi = w_router[l].T.astype(BF16)
        wr_lo = (w_router[l].T - wr_hi.astype(F32)).astype(BF16)

        v, gb, gates, qt, k, vt = _inproj(
            x2d, batch, seq, row(norm_mix_g[l]), win, wgl, wkr, cols, row(q_norm_g[l]), row(kv_norm_g[l]),
            row(b_gate[l]), wq, wqr, wk, wv, qc, qs, kc, ks, qf, tm_in)
        w_exp = [w_exp_gate[l], w_exp_up[l], w_exp_down[l]]
        w_flat = [w.reshape(-1, w.shape[-1]) for w in w_exp]
        n_attn_steps = batch * MLA_HEADS * (seq // tq)
        fused_cast = all(_side_cast_fits(w, n_attn_steps) for w in w_flat)
        att, w_bf = _attention(qt, k, vt, tq, tk, w_flat if fused_cast else [])
        att = att.reshape(t, MLA_HEADS * V_HEAD)
        wgb, wub, wdb = ([wb.reshape(w.shape) for wb, w in zip(w_bf, w_exp)] if fused_cast
                         else [w.astype(BF16) for w in w_exp])
        mk, mv = _memkv(mem2d, row(norm_memkv_g[l]), w_mem_kv[l].astype(BF16))
        x2, hp, aff = _post(
            x2d, batch, seq, v, gb, gates, att, conv_w[l], row(conv_b[l]), w_conv_out[l].astype(BF16),
            w_mla_out[l].astype(BF16), w_mix_out[l].astype(BF16), row(norm_mem_g[l]),
            w_mem_q[l].astype(BF16), mk, mv, w_mem_out[l].astype(BF16), row(norm_moe_g[l]),
            wr_hi, wr_lo, tm_post)
        idx, gsel = _route(aff.reshape(batch * N_EXPERTS, seq), cap)
        idx_flat = idx.reshape(-1)
        xg = _gather(idx_flat, hp.reshape(batch, seq, d // 2), cap)
        x2d = _experts(idx_flat, gsel, xg, wgb, wub, wdb, x2, row(norm_final_g),
                       l == depth - 1, seq, tc).reshape(t, d)
    return x2d.reshape(batch, seq, d)
```

```python
import functools
import math

import jax
import jax.numpy as jnp
import numpy as np
from jax import lax
from jax.experimental import pallas as pl
from jax.experimental.pallas import tpu as pltpu

MLA_HEADS = 8
QK_NOPE = 128
QK_ROPE = 64
QK_DIM = QK_NOPE + QK_ROPE
V_HEAD = 128
V_AUG = V_HEAD + 16
ROPE_THETA = 10000.0
MEM_HEADS = 4
N_EXPERTS = 16
CAPACITY_FACTOR = 2
EPS = 1e-6

LANES = 128
SUBLANES = 8
BF16_SUBLANES = 16
VMEM_LIMIT_BYTES = 56 * 1024 * 1024
EXPERTS_VMEM_LIMIT_BYTES = 60 * 1024 * 1024

INPROJ_TILE = 512
POST_TILE = 512
POST_SUBTILE = 256
ATTN_Q_TILE = 1024
ATTN_K_TILE = 512
ATTN_SCORE_SLOTS = 3
ATTN_BODY_ROUNDS = 2
EXPERT_CHUNK = 256
SCATTER_GROUP_OVERLAPPED = 4
SCATTER_GROUP_TAIL = 8
OUT_TILE_ROWS = 256
SIDE_CAST_SLICE_BYTES = 1024 * 1024

F32 = jnp.float32
BF16 = jnp.bfloat16


def _const_spec(shape):
    nd = len(shape)
    return pl.BlockSpec(shape, lambda *_: (0,) * nd, pipeline_mode=pl.Buffered(1))


def _rms(x, g):
    return x * lax.rsqrt(jnp.mean(x * x, axis=-1, keepdims=True) + EPS) * g


def _dot(a, b):
    return jnp.dot(a, b, preferred_element_type=F32)


def _dot_t(a, b):
    return lax.dot_general(a, b, (((1,), (1,)), ((), ())), preferred_element_type=F32)


def _inproj_kernel(cols, q_scale, x_ref, g_ref, win_ref, wgl_ref, wkr_ref, qg_ref, kvg_ref, bg_ref,
                   wq_ref, wqr_ref, wk_ref, wv_ref, qc_ref, qs_ref, kc_ref, ks_ref,
                   v_ref, gb_ref, gates_ref, qt_ref, k_ref, vt_ref):
    d = x_ref.shape[1]
    hb = _rms(x_ref[...], g_ref[...]).astype(BF16)

    def proj(name, width):
        lo = cols[name]
        return _dot(hb, win_ref[:, lo:lo + width])

    v_ref[...] = (proj("gc", d) * proj("xc", d)).astype(BF16)
    gb_ref[...] = proj("gb", d).astype(BF16)
    gates_ref[...] = jax.nn.sigmoid(_dot(hb, wgl_ref[...]) + bg_ref[...]).astype(BF16)

    q_lora = qg_ref.shape[1]
    kv_lora = kvg_ref.shape[1]
    qn = _rms(proj("cq", q_lora), qg_ref[...]).astype(BF16)
    kvn = _rms(proj("ckv", kv_lora), kvg_ref[...]).astype(BF16)
    kr2 = _dot(hb, wkr_ref[...])
    k_rope = kr2[:, :QK_ROPE] * kc_ref[...] + kr2[:, QK_ROPE:] * ks_ref[...]
    qc = qc_ref[...]
    qs = qs_ref[...]
    pad_row = lax.broadcasted_iota(jnp.int32, (V_AUG - V_HEAD, x_ref.shape[0]), 0)
    ones_rows = jnp.where(pad_row == 0, 1.0, 0.0).astype(BF16)
    for h in range(MLA_HEADS):
        qt_h = _dot_t(wq_ref[h], qn)
        rope = qt_h[QK_NOPE:] * qc + _dot_t(wqr_ref[h], qn) * qs
        qt_ref[0, h] = jnp.concatenate([qt_h[:QK_NOPE] * q_scale, rope], axis=0).astype(BF16)
        k_ref[0, h] = jnp.concatenate([_dot(kvn, wk_ref[h]), k_rope], axis=-1).astype(BF16)
        vt_ref[0, h, :V_HEAD, :] = _dot_t(wv_ref[h], kvn).astype(BF16)
        vt_ref[0, h, V_HEAD:, :] = ones_rows


def _inproj(x2d, batch, seq, g, win, wgl, wkr, cols, qg, kvg, bg, wq, wqr, wk, wv, qc, qs, kc, ks, q_scale, tm):
    t, d = x2d.shape
    nb = seq // tm
    tok = lambda i: (i, 0)
    pos = lambda i: (i % nb, 0)
    pos_t = lambda i: (0, i % nb)
    head_blk = lambda i: (i // nb, 0, i % nb, 0)
    head_blk_t = lambda i: (i // nb, 0, 0, i % nb)
    in_specs = [
        pl.BlockSpec((tm, d), tok),
        _const_spec(g.shape), _const_spec(win.shape), _const_spec(wgl.shape), _const_spec(wkr.shape),
        _const_spec(qg.shape), _const_spec(kvg.shape),
        _const_spec(bg.shape), _const_spec(wq.shape), _const_spec(wqr.shape), _const_spec(wk.shape),
        _const_spec(wv.shape),
        pl.BlockSpec((QK_ROPE, tm), pos_t), pl.BlockSpec((QK_ROPE, tm), pos_t),
        pl.BlockSpec((tm, QK_ROPE), pos), pl.BlockSpec((tm, QK_ROPE), pos),
    ]
    out_shape = [
        jax.ShapeDtypeStruct((t, d), BF16),
        jax.ShapeDtypeStruct((t, d), BF16),
        jax.ShapeDtypeStruct((t, 2 * d), BF16),
        jax.ShapeDtypeStruct((batch, MLA_HEADS, QK_DIM, seq), BF16),
        jax.ShapeDtypeStruct((batch, MLA_HEADS, seq, QK_DIM), BF16),
        jax.ShapeDtypeStruct((batch, MLA_HEADS, V_AUG, seq), BF16),
    ]
    out_specs = [
        pl.BlockSpec((tm, d), tok), pl.BlockSpec((tm, d), tok), pl.BlockSpec((tm, 2 * d), tok),
        pl.BlockSpec((1, MLA_HEADS, QK_DIM, tm), head_blk_t),
        pl.BlockSpec((1, MLA_HEADS, tm, QK_DIM), head_blk),
        pl.BlockSpec((1, MLA_HEADS, V_AUG, tm), head_blk_t),
    ]
    return pl.pallas_call(
        functools.partial(_inproj_kernel, cols, q_scale),
        grid=(t // tm,), in_specs=in_specs, out_specs=out_specs, out_shape=out_shape,
        compiler_params=pltpu.CompilerParams(dimension_semantics=("arbitrary",),
                                             vmem_limit_bytes=VMEM_LIMIT_BYTES),
        name="inproj",
    )(x2d, g, win, wgl, wkr, qg, kvg, bg, wq, wqr, wk, wv, qc, qs, kc, ks)


def _attn_kernel(tk, n_slots, n_cast, qt_ref, k_ref, vt_ref, *refs):
    cast_in, o_ref, cast_out, s_ref = refs[:n_cast], refs[n_cast], refs[n_cast + 1:2 * n_cast + 1], refs[-1]
    for src, dst in zip(cast_in, cast_out):
        dst[...] = src[...].astype(BF16)
    qt = qt_ref[0, 0]
    tq = qt.shape[1]
    nk = k_ref.shape[2] // tk

    def scores(j, slot):
        start = pl.multiple_of(j * tk, tk)
        s = _dot(k_ref[0, 0, pl.ds(start, tk), :], qt)
        s_ref[slot] = s
        return jnp.max(s, axis=0, keepdims=True)

    def update(j, slot, mx, m, acc):
        start = pl.multiple_of(j * tk, tk)
        m_new = jnp.maximum(m, mx)
        p = jnp.exp2(s_ref[slot] - m_new)
        acc = jnp.exp2(m - m_new) * acc + _dot(vt_ref[0, 0, :, pl.ds(start, tk)], p.astype(BF16))
        return m_new, acc

    ahead = n_slots - 1

    def chunks(j0, count, pending, m, acc):
        pending = list(pending)
        for u in range(count):
            if isinstance(j0, int) and j0 + u + ahead >= nk:
                pass
            else:
                pending.append(scores(j0 + u + ahead, (u + ahead) % n_slots))
            m, acc = update(j0 + u, u % n_slots, pending.pop(0), m, acc)
        return tuple(pending), m, acc

    body = n_slots * ATTN_BODY_ROUNDS
    n_iter = max(nk - ahead, 0) // body
    pending = tuple(scores(j, j) for j in range(min(ahead, nk)))
    m0 = jnp.full((1, tq), -jnp.inf, F32)
    acc0 = jnp.zeros((vt_ref.shape[2], tq), F32)
    pending, m, acc = lax.fori_loop(0, n_iter, lambda i, c: chunks(body * i, body, *c),
                                    (pending, m0, acc0))
    _, _, acc = chunks(body * n_iter, nk - body * n_iter, pending, m, acc)
    o_ref[0] = (acc[:V_HEAD] / acc[V_HEAD:V_HEAD + 1]).T.astype(BF16)


def _side_cast_fits(w, n_steps):
    rows, cols = w.shape
    slice_rows = rows // n_steps
    return (rows % n_steps == 0 and slice_rows % BF16_SUBLANES == 0
            and slice_rows * cols * 4 <= SIDE_CAST_SLICE_BYTES)


def _attention(qt, k, vt, tq, tk, side_casts):
    batch, heads, seq, _ = k.shape
    nk = seq // tk
    nq = seq // tq
    n_steps = batch * heads * nq
    step = lambda b, h, i: ((b * heads + h) * nq + i, 0)
    cast_specs = [pl.BlockSpec((w.shape[0] // n_steps, w.shape[1]), step) for w in side_casts]
    outs = pl.pallas_call(
        functools.partial(_attn_kernel, tk, ATTN_SCORE_SLOTS, len(side_casts)),
        grid=(batch, heads, nq),
        in_specs=[
            pl.BlockSpec((1, 1, QK_DIM, tq), lambda b, h, i: (b, h, 0, i)),
            pl.BlockSpec((1, 1, seq, QK_DIM), lambda b, h, i: (b, h, 0, 0)),
            pl.BlockSpec((1, 1, V_AUG, seq), lambda b, h, i: (b, h, 0, 0)),
        ] + cast_specs,
        out_specs=[pl.BlockSpec((1, tq, V_HEAD), lambda b, h, i: (b, i, h))] + cast_specs,
        out_shape=[jax.ShapeDtypeStruct((batch, seq, heads * V_HEAD), BF16)]
                  + [jax.ShapeDtypeStruct(w.shape, BF16) for w in side_casts],
        scratch_shapes=[pltpu.VMEM((ATTN_SCORE_SLOTS, tk, tq), F32)],
        compiler_params=pltpu.CompilerParams(
            dimension_semantics=("arbitrary", "arbitrary", "arbitrary"),
            vmem_limit_bytes=VMEM_LIMIT_BYTES),
        name="mla_attention",
    )(qt, k, vt, *side_casts)
    return outs[0], outs[1:]


def _memkv_kernel(m_ref, g_ref, w_ref, k_ref, v_ref):
    d = m_ref.shape[1]
    kv = _dot(_rms(m_ref[...], g_ref[...]).astype(BF16), w_ref[...])
    k_ref[...] = kv[:, :d].astype(BF16)
    v_ref[...] = kv[:, d:].astype(BF16)


def _memkv(mem2d, g, w):
    n, d = mem2d.shape
    return pl.pallas_call(
        _memkv_kernel,
        out_shape=[jax.ShapeDtypeStruct((n, d), BF16), jax.ShapeDtypeStruct((n, d), BF16)],
        compiler_params=pltpu.CompilerParams(vmem_limit_bytes=VMEM_LIMIT_BYTES),
        name="memkv",
    )(mem2d, g, w)


def _pack_bf16_pairs(lo, hi):
    return pltpu.bitcast(pltpu.pack_elementwise([lo, hi], packed_dtype=BF16), jnp.uint32)


def _unpack_bf16_pair(words, index):
    return pltpu.unpack_elementwise(words, index=index, packed_dtype=BF16, unpacked_dtype=F32).astype(BF16)


def _post_kernel(nb, sub, x_ref, v_ref, vprev_ref, vnext_ref, gb_ref, gates_ref, att_ref,
                 cw_ref, cb_ref, wco_ref, wmo_ref, wmix_ref, gmem_ref, wmq_ref, mk_ref, mv_ref, wmout_ref,
                 gmoe_ref, wrh_ref, wrl_ref,
                 x2_ref, hp_ref, aff_ref):
    i = pl.program_id(0)
    tm, d = x_ref.shape
    v = v_ref[...].astype(F32)
    first = (i % nb) == 0
    last = (i % nb) == nb - 1
    halo_prev = jnp.where(first, 0.0, vprev_ref[...].astype(F32)[BF16_SUBLANES - 1:BF16_SUBLANES, :])
    halo_next = jnp.where(last, 0.0, vnext_ref[...].astype(F32)[0:1, :])
    row = lax.broadcasted_iota(jnp.int32, (tm, d), 0)
    v_prev = jnp.where(row == 0, halo_prev, pltpu.roll(v, 1, 0))
    v_next = jnp.where(row == tm - 1, halo_next, pltpu.roll(v, tm - 1, 0))
    conv = cw_ref[0:1, :] * v_prev + cw_ref[1:2, :] * v + cw_ref[2:3, :] * v_next + cb_ref[...]
    conv_in = (gb_ref[...].astype(F32) * conv).astype(BF16)
    hd = d // MEM_HEADS
    tiles = [pl.ds(r0, sub) for r0 in range(0, tm, sub)]
    y_conv = [_dot(conv_in[r0:r0 + sub], wco_ref[...]) for r0 in range(0, tm, sub)]
    y_mla = [_dot(att_ref[rows, :], wmo_ref[...]) for rows in tiles]
    mixed = [(gates_ref[rows, :d].astype(F32) * yc + gates_ref[rows, d:].astype(F32) * ym).astype(BF16)
             for rows, yc, ym in zip(tiles, y_conv, y_mla)]
    x1 = [x_ref[rows, :] + _dot(mx, wmix_ref[...]) for rows, mx in zip(tiles, mixed)]
    qm = [(_dot((x * gmem_ref[...]).astype(BF16), wmq_ref[...])
           * (lax.rsqrt(jnp.mean(x * x, axis=-1, keepdims=True) + EPS) * (hd ** -0.5))).astype(BF16) for x in x1]
    att_m = []
    for q in qm:
        outs = []
        for h in range(MEM_HEADS):
            sl = slice(h * hd, (h + 1) * hd)
            s = _dot_t(q[:, sl], mk_ref[:, sl])
            p = jnp.exp(s - jnp.max(s, axis=-1, keepdims=True))
            inv_l = 1.0 / jnp.sum(p, axis=-1, keepdims=True)
            outs.append((_dot(p.astype(BF16), mv_ref[:, sl]) * inv_l).astype(BF16))
        att_m.append(jnp.concatenate(outs, axis=-1))
    x2 = [x + _dot(o, wmout_ref[...]) for x, o in zip(x1, att_m)]
    for rows, x in zip(tiles, x2):
        x2_ref[rows, :] = x
        h3 = _rms(x, gmoe_ref[...])
        hp_ref[rows, :] = _pack_bf16_pairs(h3[:, :d // 2], h3[:, d // 2:])
        h_hi = h3.astype(BF16)
        h_lo = (h3 - h_hi.astype(F32)).astype(BF16)
        logits = (_dot_t(wrh_ref[...], h_hi) + _dot_t(wrh_ref[...], h_lo)
                  + _dot_t(wrl_ref[...], h_hi) + _dot_t(wrl_ref[...], h_lo))
        e = jnp.exp(logits - jnp.max(logits, axis=0, keepdims=True))
        aff_ref[0, :, rows] = e / jnp.sum(e, axis=0, keepdims=True)


def _post(x2d, batch, seq, v, gb, gates, att, cw, cb, wco, wmo, wmix, gmem, wmq, mk, mv, wmout, gmoe, wrh, wrl, tm):
    t, d = x2d.shape
    nb = seq // tm
    hb = tm // BF16_SUBLANES
    n_halo = t // BF16_SUBLANES
    tok = lambda i: (i, 0)
    mem_len = mk.shape[0] // batch
    in_specs = [
        pl.BlockSpec((tm, d), tok), pl.BlockSpec((tm, d), tok),
        pl.BlockSpec((BF16_SUBLANES, d), lambda i: (jnp.maximum(i * hb - 1, 0), 0)),
        pl.BlockSpec((BF16_SUBLANES, d), lambda i: (jnp.minimum((i + 1) * hb, n_halo - 1), 0)),
        pl.BlockSpec((tm, d), tok), pl.BlockSpec((tm, 2 * d), tok), pl.BlockSpec((tm, d), tok),
        _const_spec(cw.shape), _const_spec(cb.shape), _const_spec(wco.shape), _const_spec(wmo.shape),
        _const_spec(wmix.shape), _const_spec(gmem.shape), _const_spec(wmq.shape),
        pl.BlockSpec((mem_len, d), lambda i: (i // nb, 0)),
        pl.BlockSpec((mem_len, d), lambda i: (i // nb, 0)),
        _const_spec(wmout.shape), _const_spec(gmoe.shape), _const_spec(wrh.shape), _const_spec(wrl.shape),
    ]
    out_shape = [
        jax.ShapeDtypeStruct((t, d), F32),
        jax.ShapeDtypeStruct((t, d // 2), jnp.uint32),
        jax.ShapeDtypeStruct((batch, N_EXPERTS, seq), F32),
    ]
    out_specs = [
        pl.BlockSpec((tm, d), tok), pl.BlockSpec((tm, d // 2), tok),
        pl.BlockSpec((1, N_EXPERTS, tm), lambda i: (i // nb, 0, i % nb)),
    ]
    return pl.pallas_call(
        functools.partial(_post_kernel, nb, min(tm, POST_SUBTILE)),
        grid=(t // tm,), in_specs=in_specs, out_specs=out_specs, out_shape=out_shape,
        compiler_params=pltpu.CompilerParams(dimension_semantics=("arbitrary",),
                                             vmem_limit_bytes=VMEM_LIMIT_BYTES),
        name="post",
    )(x2d, v, v, v, gb, gates, att, cw, cb, wco, wmo, wmix, gmem, wmq, mk, mv, wmout, gmoe, wrh, wrl)


def _select_kernel(cap, aff_ref, key_ref, offs_ref):
    rows, seq = aff_ref.shape
    n_chunks = seq // LANES
    aff = aff_ref[...]

    def count_ge(x):
        return jnp.sum((aff >= x).astype(F32), axis=-1, keepdims=True)

    def search(b, t):
        cand = t | (jnp.int32(1) << (30 - b))
        return jnp.where(count_ge(pltpu.bitcast(cand, F32)) >= cap, cand, t)

    thr = lax.fori_loop(0, 31, search, jnp.zeros((rows, 1), jnp.int32))

    def refine(_, lohi):
        lo, hi = lohi
        mid = 0.5 * (lo + hi)
        take = count_ge(mid) >= cap
        return jnp.where(take, mid, lo), jnp.where(take, hi, mid)

    lo, hi = lax.fori_loop(0, 30, refine, (pltpu.bitcast(thr, F32), pltpu.bitcast(thr + 1, F32)))
    gt = aff >= hi
    eq = (aff >= lo) & (aff < hi)
    need = cap - jnp.sum(gt.astype(F32), axis=-1, keepdims=True)

    tri = (lax.broadcasted_iota(jnp.int32, (LANES, LANES), 0)
           <= lax.broadcasted_iota(jnp.int32, (LANES, LANES), 1)).astype(BF16)
    ones = jnp.ones((LANES, LANES), BF16)
    lane = lax.broadcasted_iota(jnp.int32, (rows, LANES), 1)
    run_eq = jnp.zeros((rows, LANES), F32)
    run_sel = jnp.zeros((rows, LANES), F32)
    offs = jnp.zeros((rows, LANES), F32)
    for j in range(n_chunks):
        sl = slice(j * LANES, (j + 1) * LANES)
        eq_j = eq[:, sl]
        eq_b = eq_j.astype(F32).astype(BF16)
        eq_rank = _dot(eq_b, tri) + run_eq
        run_eq = run_eq + _dot(eq_b, ones)
        sel_j = gt[:, sl] | (eq_j & (eq_rank <= need))
        sel_b = sel_j.astype(F32).astype(BF16)
        pos = _dot(sel_b, tri) + run_sel
        run_sel = run_sel + _dot(sel_b, ones)
        key_ref[:, sl] = jnp.where(sel_j, pos, 0.0)
        offs = jnp.where(lane == j, run_sel, offs)
    offs_ref[...] = offs.astype(jnp.int32)


def _compact_kernel(cap, n_chunks, offs_ref, key_ref, aff_ref, idx_ref, gsel_ref):
    r = pl.program_id(0)
    n_cblk = cap // LANES
    slot = lax.broadcasted_iota(jnp.int32, (LANES, LANES), 0) + 1
    lane = lax.broadcasted_iota(jnp.int32, (LANES, LANES), 1)
    obase = r * LANES
    j0 = jnp.int32(0)
    for cb in range(n_cblk):
        want = (slot + cb * LANES).astype(F32)
        j0 = lax.while_loop(lambda j: (j < n_chunks - 1) & (offs_ref[obase + j] <= cb * LANES),
                            lambda j: j + 1, j0)
        j1 = lax.while_loop(lambda j: (j < n_chunks - 1) & (offs_ref[obase + j] < (cb + 1) * LANES),
                            lambda j: j + 1, j0)

        def per_chunk(j, carry):
            tok_sel, g_sel = carry
            start = pl.multiple_of(j * LANES, LANES)
            hit = key_ref[0, :, pl.ds(start, LANES)] == want
            tok_sel = jnp.where(hit, lane + j * LANES, tok_sel)
            g_sel = jnp.where(hit, aff_ref[0, :, pl.ds(start, LANES)], g_sel)
            return tok_sel, g_sel

        init = (jnp.zeros((LANES, LANES), jnp.int32), jnp.zeros((LANES, LANES), F32))
        tok_sel, g_sel = lax.fori_loop(j0, j1 + 1, per_chunk, init)
        idx_ref[0, cb:cb + 1, :] = jnp.sum(tok_sel.T, axis=0, keepdims=True)
        gsel_ref[0, cb:cb + 1, :] = jnp.sum(g_sel.T, axis=0, keepdims=True)
        j0 = j1


def _route(aff2d, cap):
    rows, seq = aff2d.shape
    n_chunks = seq // LANES
    n_cblk = cap // LANES
    assert n_chunks <= LANES
    key, offs = pl.pallas_call(
        functools.partial(_select_kernel, cap),
        out_shape=[jax.ShapeDtypeStruct((rows, seq), F32), jax.ShapeDtypeStruct((rows, LANES), jnp.int32)],
        compiler_params=pltpu.CompilerParams(vmem_limit_bytes=VMEM_LIMIT_BYTES),
        name="route_select",
    )(aff2d)
    row_blk = lambda r, offs: (r, 0, 0)
    grid_spec = pltpu.PrefetchScalarGridSpec(
        num_scalar_prefetch=1, grid=(rows,),
        in_specs=[pl.BlockSpec((1, 1, seq), row_blk), pl.BlockSpec((1, 1, seq), row_blk)],
        out_specs=[pl.BlockSpec((1, n_cblk, LANES), row_blk), pl.BlockSpec((1, n_cblk, LANES), row_blk)],
    )
    return pl.pallas_call(
        functools.partial(_compact_kernel, cap, n_chunks),
        grid_spec=grid_spec,
        out_shape=[jax.ShapeDtypeStruct((rows, n_cblk, LANES), jnp.int32),
                   jax.ShapeDtypeStruct((rows, n_cblk, LANES), F32)],
        compiler_params=pltpu.CompilerParams(dimension_semantics=("arbitrary",),
                                             vmem_limit_bytes=VMEM_LIMIT_BYTES),
        name="route_compact",
    )(offs.reshape(-1), key.reshape(rows, 1, seq), aff2d.reshape(rows, 1, seq))


def _gather_kernel(cap, idx_ref, hp_ref, xg_ref):
    b = pl.program_id(0)
    e = pl.program_id(1)
    base = (b * N_EXPERTS + e) * cap
    for c in range(cap):
        g, k = divmod(c, SUBLANES)
        xg_ref[0, 0, g, k:k + 1, :] = hp_ref[0, pl.ds(idx_ref[base + c], 1), :]


def _gather(idx_flat, hp, cap):
    batch, seq, half = hp.shape
    grid_spec = pltpu.PrefetchScalarGridSpec(
        num_scalar_prefetch=1,
        grid=(batch, N_EXPERTS),
        in_specs=[pl.BlockSpec((1, seq, half), lambda b, e, idx: (b, 0, 0))],
        out_specs=pl.BlockSpec((1, 1, cap // SUBLANES, SUBLANES, half), lambda b, e, idx: (b, e, 0, 0, 0)),
    )
    xg = pl.pallas_call(
        functools.partial(_gather_kernel, cap),
        grid_spec=grid_spec,
        out_shape=jax.ShapeDtypeStruct((batch, N_EXPERTS, cap // SUBLANES, SUBLANES, half), hp.dtype),
        compiler_params=pltpu.CompilerParams(dimension_semantics=("arbitrary", "arbitrary"),
                                             vmem_limit_bytes=VMEM_LIMIT_BYTES),
        name="gather",
    )(idx_flat, hp)
    return xg.reshape(batch, N_EXPERTS, cap, half)


def _experts_kernel(cap, tc, apply_norm, out_rows, idx_ref, xg_ref, gsel_ref, wg_ref, wu_ref, wd_ref, x2_ref, gf_ref,
                    o_hbm, acc_ref, y_ref, sem):
    b = pl.program_id(0)
    e = pl.program_id(1)
    base = (b * N_EXPERTS + e) * cap
    n_chunks = cap // tc

    @pl.when(e == 0)
    def _():
        acc_ref[...] = jnp.zeros_like(acc_ref)

    slab = x2_ref.shape[0]
    slab_rows = pl.ds(pl.multiple_of(e * slab, slab), slab)
    acc_ref[slab_rows, :] = acc_ref[slab_rows, :] + x2_ref[...]

    diag = (lax.broadcasted_iota(jnp.int32, (LANES, LANES), 0)
            == lax.broadcasted_iota(jnp.int32, (LANES, LANES), 1))

    def gate_column(ci):
        cols = []
        for blk in range(ci * tc // LANES, (ci + 1) * tc // LANES):
            g_row = gsel_ref[0, blk:blk + 1, :]
            cols.append(jnp.sum(jnp.where(diag, g_row, 0.0), axis=1, keepdims=True))
        return jnp.concatenate(cols, axis=0)

    def ffn(ci):
        words = xg_ref[0, 0, ci * tc:(ci + 1) * tc, :]
        x = jnp.concatenate([_unpack_bf16_pair(words, 0), _unpack_bf16_pair(words, 1)], axis=-1)
        gate = _dot(x, wg_ref[0])
        up = _dot(x, wu_ref[0])
        act = (gate * jax.nn.sigmoid(gate) * up * gate_column(ci)).astype(BF16)
        y_ref[ci % 2] = _dot(act, wd_ref[0])

    def scatter(ci, group):
        for r0 in range(0, tc, group):
            toks = [idx_ref[base + ci * tc + r0 + k] for k in range(group)]
            rows = [acc_ref[pl.ds(tok, 1), :] + y_ref[ci % 2, r0 + k:r0 + k + 1, :]
                    for k, tok in enumerate(toks)]
            for tok, row in zip(toks, rows):
                acc_ref[pl.ds(tok, 1), :] = row

    for ci in range(n_chunks):
        ffn(ci)
        if ci > 0:
            scatter(ci - 1, SCATTER_GROUP_OVERLAPPED)
    scatter(n_chunks - 1, SCATTER_GROUP_TAIL)

    @pl.when(e == N_EXPERTS - 1)
    def _():
        n_tiles = acc_ref.shape[0] // out_rows

        def out_copy(i):
            rows = pl.ds(pl.multiple_of(i * out_rows, out_rows), out_rows)
            return rows, pltpu.make_async_copy(acc_ref.at[rows], o_hbm.at[b, rows], sem)

        def emit(i, _):
            rows, copy = out_copy(i)
            if apply_norm:
                acc_ref[rows, :] = _rms(acc_ref[rows, :], gf_ref[...])
            copy.start()
            return 0

        def drain(i, _):
            out_copy(i)[1].wait()
            return 0

        lax.fori_loop(0, n_tiles, emit, 0)
        lax.fori_loop(0, n_tiles, drain, 0)


def _experts(idx_flat, gsel, xg, wg, wu, wd, x2, gf, apply_norm, seq, tc):
    batch, _, cap, half = xg.shape
    d = 2 * half
    ff = wg.shape[2]
    slab = seq // N_EXPERTS
    assert tc % LANES == 0 and gsel.shape == (batch * N_EXPERTS, cap // LANES, LANES)
    grid_spec = pltpu.PrefetchScalarGridSpec(
        num_scalar_prefetch=1,
        grid=(batch, N_EXPERTS),
        in_specs=[
            pl.BlockSpec((1, 1, cap, half), lambda b, e, idx: (b, e, 0, 0)),
            pl.BlockSpec((1, cap // LANES, LANES), lambda b, e, idx: (b * N_EXPERTS + e, 0, 0)),
            pl.BlockSpec((1, d, ff), lambda b, e, idx: (e, 0, 0)),
            pl.BlockSpec((1, d, ff), lambda b, e, idx: (e, 0, 0)),
            pl.BlockSpec((1, ff, d), lambda b, e, idx: (e, 0, 0)),
            pl.BlockSpec((slab, d), lambda b, e, idx: (b * N_EXPERTS + e, 0)),
            pl.BlockSpec(gf.shape, lambda b, e, idx: (0, 0)),
        ],
        out_specs=pl.BlockSpec(memory_space=pl.ANY),
        scratch_shapes=[pltpu.VMEM((seq, d), F32), pltpu.VMEM((2, tc, d), F32), pltpu.SemaphoreType.DMA],
    )
    return pl.pallas_call(
        functools.partial(_experts_kernel, cap, tc, apply_norm, min(seq, OUT_TILE_ROWS)),
        grid_spec=grid_spec,
        out_shape=jax.ShapeDtypeStruct((batch, seq, d), F32),
        compiler_params=pltpu.CompilerParams(dimension_semantics=("arbitrary", "arbitrary"),
                                             vmem_limit_bytes=EXPERTS_VMEM_LIMIT_BYTES),
        name="experts",
    )(idx_flat, xg, gsel, wg, wu, wd, x2, gf)


def _rot_cols(w):
    half = w.shape[-1] // 2
    return jnp.concatenate([-w[..., half:], w[..., :half]], axis=-1)


def _tile(n, pref):
    return pref if n % pref == 0 else n


def kernel(x, mem, norm_mix_g, w_in, conv_w, conv_b, w_conv_out, q_norm_g, w_uq, kv_norm_g, w_ukv,
           w_mla_out, b_gate, w_mix_out, norm_mem_g, norm_memkv_g, w_mem_q, w_mem_kv, w_mem_out,
           norm_moe_g, w_router, w_exp_gate, w_exp_up, w_exp_down, norm_final_g):
    batch, seq, d = x.shape
    depth = w_in.shape[0]
    t = batch * seq
    q_lora = q_norm_g.shape[1]
    kv_lora = kv_norm_g.shape[1]
    cap = max(1, CAPACITY_FACTOR * seq // N_EXPERTS)
    assert cap % LANES == 0 and seq % LANES == 0 and d % (2 * LANES) == 0

    tm_in = _tile(seq, INPROJ_TILE)
    tm_post = _tile(seq, POST_TILE)
    tq = _tile(seq, ATTN_Q_TILE)
    tk = _tile(seq, ATTN_K_TILE)
    tc = _tile(cap, EXPERT_CHUNK)

    inv = 1.0 / (ROPE_THETA ** (np.arange(0, QK_ROPE, 2, dtype=np.float64) / QK_ROPE))
    ang = np.arange(seq, dtype=np.float64)[:, None] * inv[None, :]
    cos2 = np.concatenate([np.cos(ang), np.cos(ang)], axis=-1)
    sin2 = np.concatenate([np.sin(ang), np.sin(ang)], axis=-1)
    qf = (QK_DIM ** -0.5) * math.log2(math.e)
    kc, ks = jnp.asarray(cos2, F32), jnp.asarray(sin2, F32)
    qc, qs = jnp.asarray(qf * cos2.T, F32), jnp.asarray(qf * sin2.T, F32)

    o_cq = 3 * d
    o_ckv = o_cq + q_lora
    o_kr = o_ckv + kv_lora
    o_gl = o_kr + QK_ROPE
    cols = {"xc": 0, "gb": d, "gc": 2 * d, "cq": o_cq, "ckv": o_ckv}

    x2d = x.reshape(t, d)
    mem2d = mem.reshape(-1, d)
    row = lambda a: a.reshape(1, -1)

    for l in range(depth):
        win = w_in[l].astype(BF16)
        wgl = win[:, o_gl:]
        wkr = jnp.concatenate([win[:, o_kr:o_gl], _rot_cols(win[:, o_kr:o_gl])], axis=1)
        wq3 = w_uq[l].reshape(q_lora, MLA_HEADS, QK_DIM)
        wq = wq3.transpose(1, 2, 0).astype(BF16)
        wqr = _rot_cols(wq3[..., QK_NOPE:]).transpose(1, 2, 0).astype(BF16)
        wkv3 = w_ukv[l].reshape(kv_lora, MLA_HEADS, QK_NOPE + V_HEAD)
        wk = wkv3[..., :QK_NOPE].transpose(1, 0, 2).astype(BF16)
        wv = wkv3[..., QK_NOPE:].transpose(1, 2, 0).astype(BF16)

        wr_hi = w_router[l].T.astype(BF16)
        wr_lo = (w_router[l].T - wr_hi.astype(F32)).astype(BF16)

        v, gb, gates, qt, k, vt = _inproj(
            x2d, batch, seq, row(norm_mix_g[l]), win, wgl, wkr, cols, row(q_norm_g[l]), row(kv_norm_g[l]),
            row(b_gate[l]), wq, wqr, wk, wv, qc, qs, kc, ks, qf, tm_in)
        w_exp = [w_exp_gate[l], w_exp_up[l], w_exp_down[l]]
        w_flat = [w.reshape(-1, w.shape[-1]) for w in w_exp]
        n_attn_steps = batch * MLA_HEADS * (seq // tq)
        fused_cast = all(_side_cast_fits(w, n_attn_steps) for w in w_flat)
        att, w_bf = _attention(qt, k, vt, tq, tk, w_flat if fused_cast else [])
        att = att.reshape(t, MLA_HEADS * V_HEAD)
        wgb, wub, wdb = ([wb.reshape(w.shape) for wb, w in zip(w_bf, w_exp)] if fused_cast
                         else [w.astype(BF16) for w in w_exp])
        mk, mv = _memkv(mem2d, row(norm_memkv_g[l]), w_mem_kv[l].astype(BF16))
        x2, hp, aff = _post(
            x2d, batch, seq, v, gb, gates, att, conv_w[l], row(conv_b[l]), w_conv_out[l].astype(BF16),
            w_mla_out[l].astype(BF16), w_mix_out[l].astype(BF16), row(norm_mem_g[l]),
            w_mem_q[l].astype(BF16), mk, mv, w_mem_out[l].astype(BF16), row(norm_moe_g[l]),
            wr_hi, wr_lo, tm_post)
        idx, gsel = _route(aff.reshape(batch * N_EXPERTS, seq), cap)
        idx_flat = idx.reshape(-1)
        xg = _gather(idx_flat, hp.reshape(batch, seq, d // 2), cap)
        x2d = _experts(idx_flat, gsel, xg, wgb, wub, wdb, x2, row(norm_final_g),
                       l == depth - 1, seq, tc).reshape(t, d)
    return x2d.reshape(batch, seq, d)
```

```python
import functools
import math

import jax
import jax.numpy as jnp
import numpy as np
from jax import lax
from jax.experimental import pallas as pl
from jax.experimental.pallas import tpu as pltpu

MLA_HEADS = 8
QK_NOPE = 128
QK_ROPE = 64
QK_DIM = QK_NOPE + QK_ROPE
V_HEAD = 128
V_AUG = V_HEAD + 16
ROPE_THETA = 10000.0
MEM_HEADS = 4
N_EXPERTS = 16
CAPACITY_FACTOR = 2
EPS = 1e-6

LANES = 128
SUBLANES = 8
BF16_SUBLANES = 16
VMEM_LIMIT_BYTES = 56 * 1024 * 1024
EXPERTS_VMEM_LIMIT_BYTES = 60 * 1024 * 1024

INPROJ_TILE = 512
POST_TILE = 512
POST_SUBTILE = 256
ATTN_Q_TILE = 1024
ATTN_K_TILE = 512
ATTN_SCORE_SLOTS = 3
ATTN_BODY_ROUNDS = 2
EXPERT_CHUNK = 256
SCATTER_GROUP_OVERLAPPED = 2
SCATTER_GROUP_TAIL = 4
OUT_TILE_ROWS = 256
SIDE_CAST_SLICE_BYTES = 1024 * 1024

F32 = jnp.float32
BF16 = jnp.bfloat16


def _const_spec(shape):
    nd = len(shape)
    return pl.BlockSpec(shape, lambda *_: (0,) * nd, pipeline_mode=pl.Buffered(1))


def _rms(x, g):
    return x * lax.rsqrt(jnp.mean(x * x, axis=-1, keepdims=True) + EPS) * g


def _dot(a, b):
    return jnp.dot(a, b, preferred_element_type=F32)


def _dot_t(a, b):
    return lax.dot_general(a, b, (((1,), (1,)), ((), ())), preferred_element_type=F32)


def _inproj_kernel(cols, q_scale, x_ref, g_ref, win_ref, wgl_ref, wkr_ref, qg_ref, kvg_ref, bg_ref,
                   wq_ref, wqr_ref, wk_ref, wv_ref, qc_ref, qs_ref, kc_ref, ks_ref,
                   v_ref, gb_ref, gates_ref, qt_ref, k_ref, vt_ref):
    d = x_ref.shape[1]
    hb = _rms(x_ref[...], g_ref[...]).astype(BF16)

    def proj(name, width):
        lo = cols[name]
        return _dot(hb, win_ref[:, lo:lo + width])

    v_ref[...] = (proj("gc", d) * proj("xc", d)).astype(BF16)
    gb_ref[...] = proj("gb", d).astype(BF16)
    gates_ref[...] = jax.nn.sigmoid(_dot(hb, wgl_ref[...]) + bg_ref[...]).astype(BF16)

    q_lora = qg_ref.shape[1]
    kv_lora = kvg_ref.shape[1]
    qn = _rms(proj("cq", q_lora), qg_ref[...]).astype(BF16)
    kvn = _rms(proj("ckv", kv_lora), kvg_ref[...]).astype(BF16)
    kr2 = _dot(hb, wkr_ref[...])
    k_rope = kr2[:, :QK_ROPE] * kc_ref[...] + kr2[:, QK_ROPE:] * ks_ref[...]
    qc = qc_ref[...]
    qs = qs_ref[...]
    pad_row = lax.broadcasted_iota(jnp.int32, (V_AUG - V_HEAD, x_ref.shape[0]), 0)
    ones_rows = jnp.where(pad_row == 0, 1.0, 0.0).astype(BF16)
    for h in range(MLA_HEADS):
        qt_h = _dot_t(wq_ref[h], qn)
        rope = qt_h[QK_NOPE:] * qc + _dot_t(wqr_ref[h], qn) * qs
        qt_ref[0, h] = jnp.concatenate([qt_h[:QK_NOPE] * q_scale, rope], axis=0).astype(BF16)
        k_ref[0, h] = jnp.concatenate([_dot(kvn, wk_ref[h]), k_rope], axis=-1).astype(BF16)
        vt_ref[0, h, :V_HEAD, :] = _dot_t(wv_ref[h], kvn).astype(BF16)
        vt_ref[0, h, V_HEAD:, :] = ones_rows


def _inproj(x2d, batch, seq, g, win, wgl, wkr, cols, qg, kvg, bg, wq, wqr, wk, wv, qc, qs, kc, ks, q_scale, tm):
    t, d = x2d.shape
    nb = seq // tm
    tok = lambda i: (i, 0)
    pos = lambda i: (i % nb, 0)
    pos_t = lambda i: (0, i % nb)
    head_blk = lambda i: (i // nb, 0, i % nb, 0)
    head_blk_t = lambda i: (i // nb, 0, 0, i % nb)
    in_specs = [
        pl.BlockSpec((tm, d), tok),
        _const_spec(g.shape), _const_spec(win.shape), _const_spec(wgl.shape), _const_spec(wkr.shape),
        _const_spec(qg.shape), _const_spec(kvg.shape),
        _const_spec(bg.shape), _const_spec(wq.shape), _const_spec(wqr.shape), _const_spec(wk.shape),
        _const_spec(wv.shape),
        pl.BlockSpec((QK_ROPE, tm), pos_t), pl.BlockSpec((QK_ROPE, tm), pos_t),
        pl.BlockSpec((tm, QK_ROPE), pos), pl.BlockSpec((tm, QK_ROPE), pos),
    ]
    out_shape = [
        jax.ShapeDtypeStruct((t, d), BF16),
        jax.ShapeDtypeStruct((t, d), BF16),
        jax.ShapeDtypeStruct((t, 2 * d), BF16),
        jax.ShapeDtypeStruct((batch, MLA_HEADS, QK_DIM, seq), BF16),
        jax.ShapeDtypeStruct((batch, MLA_HEADS, seq, QK_DIM), BF16),
        jax.ShapeDtypeStruct((batch, MLA_HEADS, V_AUG, seq), BF16),
    ]
    out_specs = [
        pl.BlockSpec((tm, d), tok), pl.BlockSpec((tm, d), tok), pl.BlockSpec((tm, 2 * d), tok),
        pl.BlockSpec((1, MLA_HEADS, QK_DIM, tm), head_blk_t),
        pl.BlockSpec((1, MLA_HEADS, tm, QK_DIM), head_blk),
        pl.BlockSpec((1, MLA_HEADS, V_AUG, tm), head_blk_t),
    ]
    return pl.pallas_call(
        functools.partial(_inproj_kernel, cols, q_scale),
        grid=(t // tm,), in_specs=in_specs, out_specs=out_specs, out_shape=out_shape,
        compiler_params=pltpu.CompilerParams(dimension_semantics=("arbitrary",),
                                             vmem_limit_bytes=VMEM_LIMIT_BYTES),
        name="inproj",
    )(x2d, g, win, wgl, wkr, qg, kvg, bg, wq, wqr, wk, wv, qc, qs, kc, ks)


def _attn_kernel(tk, n_slots, n_cast, qt_ref, k_ref, vt_ref, *refs):
    cast_in, o_ref, cast_out, s_ref = refs[:n_cast], refs[n_cast], refs[n_cast + 1:2 * n_cast + 1], refs[-1]
    for src, dst in zip(cast_in, cast_out):
        dst[...] = src[...].astype(BF16)
    qt = qt_ref[0, 0]
    tq = qt.shape[1]
    nk = k_ref.shape[2] // tk

    def scores(j, slot):
        start = pl.multiple_of(j * tk, tk)
        s = _dot(k_ref[0, 0, pl.ds(start, tk), :], qt)
        s_ref[slot] = s
        return jnp.max(s, axis=0, keepdims=True)

    def update(j, slot, mx, m, acc):
        start = pl.multiple_of(j * tk, tk)
        m_new = jnp.maximum(m, mx)
        p = jnp.exp2(s_ref[slot] - m_new)
        acc = jnp.exp2(m - m_new) * acc + _dot(vt_ref[0, 0, :, pl.ds(start, tk)], p.astype(BF16))
        return m_new, acc

    ahead = n_slots - 1

    def chunks(j0, count, pending, m, acc):
        pending = list(pending)
        for u in range(count):
            if isinstance(j0, int) and j0 + u + ahead >= nk:
                pass
            else:
                pending.append(scores(j0 + u + ahead, (u + ahead) % n_slots))
            m, acc = update(j0 + u, u % n_slots, pending.pop(0), m, acc)
        return tuple(pending), m, acc

    body = n_slots * ATTN_BODY_ROUNDS
    n_iter = max(nk - ahead, 0) // body
    pending = tuple(scores(j, j) for j in range(min(ahead, nk)))
    m0 = jnp.full((1, tq), -jnp.inf, F32)
    acc0 = jnp.zeros((vt_ref.shape[2], tq), F32)
    pending, m, acc = lax.fori_loop(0, n_iter, lambda i, c: chunks(body * i, body, *c),
                                    (pending, m0, acc0))
    _, _, acc = chunks(body * n_iter, nk - body * n_iter, pending, m, acc)
    o_ref[0] = (acc[:V_HEAD] / acc[V_HEAD:V_HEAD + 1]).T.astype(BF16)


def _side_cast_fits(w, n_steps):
    rows, cols = w.shape
    slice_rows = rows // n_steps
    return (rows % n_steps == 0 and slice_rows % BF16_SUBLANES == 0
            and slice_rows * cols * 4 <= SIDE_CAST_SLICE_BYTES)


def _attention(qt, k, vt, tq, tk, side_casts):
    batch, heads, seq, _ = k.shape
    nk = seq // tk
    nq = seq // tq
    n_steps = batch * heads * nq
    step = lambda b, h, i: ((b * heads + h) * nq + i, 0)
    cast_specs = [pl.BlockSpec((w.shape[0] // n_steps, w.shape[1]), step) for w in side_casts]
    outs = pl.pallas_call(
        functools.partial(_attn_kernel, tk, ATTN_SCORE_SLOTS, len(side_casts)),
        grid=(batch, heads, nq),
        in_specs=[
            pl.BlockSpec((1, 1, QK_DIM, tq), lambda b, h, i: (b, h, 0, i)),
            pl.BlockSpec((1, 1, seq, QK_DIM), lambda b, h, i: (b, h, 0, 0)),
            pl.BlockSpec((1, 1, V_AUG, seq), lambda b, h, i: (b, h, 0, 0)),
        ] + cast_specs,
        out_specs=[pl.BlockSpec((1, tq, V_HEAD), lambda b, h, i: (b, i, h))] + cast_specs,
        out_shape=[jax.ShapeDtypeStruct((batch, seq, heads * V_HEAD), BF16)]
                  + [jax.ShapeDtypeStruct(w.shape, BF16) for w in side_casts],
        scratch_shapes=[pltpu.VMEM((ATTN_SCORE_SLOTS, tk, tq), F32)],
        compiler_params=pltpu.CompilerParams(
            dimension_semantics=("arbitrary", "arbitrary", "arbitrary"),
            vmem_limit_bytes=VMEM_LIMIT_BYTES),
        name="mla_attention",
    )(qt, k, vt, *side_casts)
    return outs[0], outs[1:]


def _memkv_kernel(m_ref, g_ref, w_ref, k_ref, v_ref):
    d = m_ref.shape[1]
    kv = _dot(_rms(m_ref[...], g_ref[...]).astype(BF16), w_ref[...])
    k_ref[...] = kv[:, :d].astype(BF16)
    v_ref[...] = kv[:, d:].astype(BF16)


def _memkv(mem2d, g, w):
    n, d = mem2d.shape
    return pl.pallas_call(
        _memkv_kernel,
        out_shape=[jax.ShapeDtypeStruct((n, d), BF16), jax.ShapeDtypeStruct((n, d), BF16)],
        compiler_params=pltpu.CompilerParams(vmem_limit_bytes=VMEM_LIMIT_BYTES),
        name="memkv",
    )(mem2d, g, w)


def _pack_bf16_pairs(lo, hi):
    return pltpu.bitcast(pltpu.pack_elementwise([lo, hi], packed_dtype=BF16), jnp.uint32)


def _unpack_bf16_pair(words, index):
    return pltpu.unpack_elementwise(words, index=index, packed_dtype=BF16, unpacked_dtype=F32).astype(BF16)


def _post_kernel(nb, sub, x_ref, v_ref, vprev_ref, vnext_ref, gb_ref, gates_ref, att_ref,
                 cw_ref, cb_ref, wco_ref, wmo_ref, wmix_ref, gmem_ref, wmq_ref, mk_ref, mv_ref, wmout_ref,
                 gmoe_ref, wrh_ref, wrl_ref,
                 x2_ref, hp_ref, aff_ref):
    i = pl.program_id(0)
    tm, d = x_ref.shape
    v = v_ref[...].astype(F32)
    first = (i % nb) == 0
    last = (i % nb) == nb - 1
    halo_prev = jnp.where(first, 0.0, vprev_ref[...].astype(F32)[BF16_SUBLANES - 1:BF16_SUBLANES, :])
    halo_next = jnp.where(last, 0.0, vnext_ref[...].astype(F32)[0:1, :])
    row = lax.broadcasted_iota(jnp.int32, (tm, d), 0)
    v_prev = jnp.where(row == 0, halo_prev, pltpu.roll(v, 1, 0))
    v_next = jnp.where(row == tm - 1, halo_next, pltpu.roll(v, tm - 1, 0))
    conv = cw_ref[0:1, :] * v_prev + cw_ref[1:2, :] * v + cw_ref[2:3, :] * v_next + cb_ref[...]
    conv_in = (gb_ref[...].astype(F32) * conv).astype(BF16)
    hd = d // MEM_HEADS
    tiles = [pl.ds(r0, sub) for r0 in range(0, tm, sub)]
    y_conv = [_dot(conv_in[r0:r0 + sub], wco_ref[...]) for r0 in range(0, tm, sub)]
    y_mla = [_dot(att_ref[rows, :], wmo_ref[...]) for rows in tiles]
    mixed = [(gates_ref[rows, :d].astype(F32) * yc + gates_ref[rows, d:].astype(F32) * ym).astype(BF16)
             for rows, yc, ym in zip(tiles, y_conv, y_mla)]
    x1 = [x_ref[rows, :] + _dot(mx, wmix_ref[...]) for rows, mx in zip(tiles, mixed)]
    qm = [(_dot((x * gmem_ref[...]).astype(BF16), wmq_ref[...])
           * (lax.rsqrt(jnp.mean(x * x, axis=-1, keepdims=True) + EPS) * (hd ** -0.5))).astype(BF16) for x in x1]
    att_m = []
    for q in qm:
        outs = []
        for h in range(MEM_HEADS):
            sl = slice(h * hd, (h + 1) * hd)
            s = _dot_t(q[:, sl], mk_ref[:, sl])
            p = jnp.exp(s - jnp.max(s, axis=-1, keepdims=True))
            inv_l = 1.0 / jnp.sum(p, axis=-1, keepdims=True)
            outs.append((_dot(p.astype(BF16), mv_ref[:, sl]) * inv_l).astype(BF16))
        att_m.append(jnp.concatenate(outs, axis=-1))
    x2 = [x + _dot(o, wmout_ref[...]) for x, o in zip(x1, att_m)]
    for rows, x in zip(tiles, x2):
        x2_ref[rows, :] = x
        h3 = _rms(x, gmoe_ref[...])
        hp_ref[rows, :] = _pack_bf16_pairs(h3[:, :d // 2], h3[:, d // 2:])
        h_hi = h3.astype(BF16)
        h_lo = (h3 - h_hi.astype(F32)).astype(BF16)
        logits = (_dot_t(wrh_ref[...], h_hi) + _dot_t(wrh_ref[...], h_lo)
                  + _dot_t(wrl_ref[...], h_hi) + _dot_t(wrl_ref[...], h_lo))
        e = jnp.exp(logits - jnp.max(logits, axis=0, keepdims=True))
        aff_ref[0, :, rows] = e / jnp.sum(e, axis=0, keepdims=True)


def _post(x2d, batch, seq, v, gb, gates, att, cw, cb, wco, wmo, wmix, gmem, wmq, mk, mv, wmout, gmoe, wrh, wrl, tm):
    t, d = x2d.shape
    nb = seq // tm
    hb = tm // BF16_SUBLANES
    n_halo = t // BF16_SUBLANES
    tok = lambda i: (i, 0)
    mem_len = mk.shape[0] // batch
    in_specs = [
        pl.BlockSpec((tm, d), tok), pl.BlockSpec((tm, d), tok),
        pl.BlockSpec((BF16_SUBLANES, d), lambda i: (jnp.maximum(i * hb - 1, 0), 0)),
        pl.BlockSpec((BF16_SUBLANES, d), lambda i: (jnp.minimum((i + 1) * hb, n_halo - 1), 0)),
        pl.BlockSpec((tm, d), tok), pl.BlockSpec((tm, 2 * d), tok), pl.BlockSpec((tm, d), tok),
        _const_spec(cw.shape), _const_spec(cb.shape), _const_spec(wco.shape), _const_spec(wmo.shape),
        _const_spec(wmix.shape), _const_spec(gmem.shape), _const_spec(wmq.shape),
        pl.BlockSpec((mem_len, d), lambda i: (i // nb, 0)),
        pl.BlockSpec((mem_len, d), lambda i: (i // nb, 0)),
        _const_spec(wmout.shape), _const_spec(gmoe.shape), _const_spec(wrh.shape), _const_spec(wrl.shape),
    ]
    out_shape = [
        jax.ShapeDtypeStruct((t, d), F32),
        jax.ShapeDtypeStruct((t, d // 2), jnp.uint32),
        jax.ShapeDtypeStruct((batch, N_EXPERTS, seq), F32),
    ]
    out_specs = [
        pl.BlockSpec((tm, d), tok), pl.BlockSpec((tm, d // 2), tok),
        pl.BlockSpec((1, N_EXPERTS, tm), lambda i: (i // nb, 0, i % nb)),
    ]
    return pl.pallas_call(
        functools.partial(_post_kernel, nb, min(tm, POST_SUBTILE)),
        grid=(t // tm,), in_specs=in_specs, out_specs=out_specs, out_shape=out_shape,
        compiler_params=pltpu.CompilerParams(dimension_semantics=("arbitrary",),
                                             vmem_limit_bytes=VMEM_LIMIT_BYTES),
        name="post",
    )(x2d, v, v, v, gb, gates, att, cw, cb, wco, wmo, wmix, gmem, wmq, mk, mv, wmout, gmoe, wrh, wrl)


def _select_kernel(cap, aff_ref, key_ref, offs_ref):
    rows, seq = aff_ref.shape
    n_chunks = seq // LANES
    aff = aff_ref[...]

    def count_ge(x):
        return jnp.sum((aff >= x).astype(F32), axis=-1, keepdims=True)

    def search(b, t):
        cand = t | (jnp.int32(1) << (30 - b))
        return jnp.where(count_ge(pltpu.bitcast(cand, F32)) >= cap, cand, t)

    thr = lax.fori_loop(0, 31, search, jnp.zeros((rows, 1), jnp.int32))

    def refine(_, lohi):
        lo, hi = lohi
        mid = 0.5 * (lo + hi)
        take = count_ge(mid) >= cap
        return jnp.where(take, mid, lo), jnp.where(take, hi, mid)

    lo, hi = lax.fori_loop(0, 30, refine, (pltpu.bitcast(thr, F32), pltpu.bitcast(thr + 1, F32)))
    gt = aff >= hi
    eq = (aff >= lo) & (aff < hi)
    need = cap - jnp.sum(gt.astype(F32), axis=-1, keepdims=True)

    tri = (lax.broadcasted_iota(jnp.int32, (LANES, LANES), 0)
           <= lax.broadcasted_iota(jnp.int32, (LANES, LANES), 1)).astype(BF16)
    ones = jnp.ones((LANES, LANES), BF16)
    lane = lax.broadcasted_iota(jnp.int32, (rows, LANES), 1)
    run_eq = jnp.zeros((rows, LANES), F32)
    run_sel = jnp.zeros((rows, LANES), F32)
    offs = jnp.zeros((rows, LANES), F32)
    for j in range(n_chunks):
        sl = slice(j * LANES, (j + 1) * LANES)
        eq_j = eq[:, sl]
        eq_b = eq_j.astype(F32).astype(BF16)
        eq_rank = _dot(eq_b, tri) + run_eq
        run_eq = run_eq + _dot(eq_b, ones)
        sel_j = gt[:, sl] | (eq_j & (eq_rank <= need))
        sel_b = sel_j.astype(F32).astype(BF16)
        pos = _dot(sel_b, tri) + run_sel
        run_sel = run_sel + _dot(sel_b, ones)
        key_ref[:, sl] = jnp.where(sel_j, pos, 0.0)
        offs = jnp.where(lane == j, run_sel, offs)
    offs_ref[...] = offs.astype(jnp.int32)


def _compact_kernel(cap, n_chunks, offs_ref, key_ref, aff_ref, idx_ref, gsel_ref):
    r = pl.program_id(0)
    n_cblk = cap // LANES
    slot = lax.broadcasted_iota(jnp.int32, (LANES, LANES), 0) + 1
    lane = lax.broadcasted_iota(jnp.int32, (LANES, LANES), 1)
    obase = r * LANES
    j0 = jnp.int32(0)
    for cb in range(n_cblk):
        want = (slot + cb * LANES).astype(F32)
        j0 = lax.while_loop(lambda j: (j < n_chunks - 1) & (offs_ref[obase + j] <= cb * LANES),
                            lambda j: j + 1, j0)
        j1 = lax.while_loop(lambda j: (j < n_chunks - 1) & (offs_ref[obase + j] < (cb + 1) * LANES),
                            lambda j: j + 1, j0)

        def per_chunk(j, carry):
            tok_sel, g_sel = carry
            start = pl.multiple_of(j * LANES, LANES)
            hit = key_ref[0, :, pl.ds(start, LANES)] == want
            tok_sel = jnp.where(hit, lane + j * LANES, tok_sel)
            g_sel = jnp.where(hit, aff_ref[0, :, pl.ds(start, LANES)], g_sel)
            return tok_sel, g_sel

        init = (jnp.zeros((LANES, LANES), jnp.int32), jnp.zeros((LANES, LANES), F32))
        tok_sel, g_sel = lax.fori_loop(j0, j1 + 1, per_chunk, init)
        idx_ref[0, cb:cb + 1, :] = jnp.sum(tok_sel.T, axis=0, keepdims=True)
        gsel_ref[0, cb:cb + 1, :] = jnp.sum(g_sel.T, axis=0, keepdims=True)
        j0 = j1


def _route(aff2d, cap):
    rows, seq = aff2d.shape
    n_chunks = seq // LANES
    n_cblk = cap // LANES
    assert n_chunks <= LANES
    key, offs = pl.pallas_call(
        functools.partial(_select_kernel, cap),
        out_shape=[jax.ShapeDtypeStruct((rows, seq), F32), jax.ShapeDtypeStruct((rows, LANES), jnp.int32)],
        compiler_params=pltpu.CompilerParams(vmem_limit_bytes=VMEM_LIMIT_BYTES),
        name="route_select",
    )(aff2d)
    row_blk = lambda r, offs: (r, 0, 0)
    grid_spec = pltpu.PrefetchScalarGridSpec(
        num_scalar_prefetch=1, grid=(rows,),
        in_specs=[pl.BlockSpec((1, 1, seq), row_blk), pl.BlockSpec((1, 1, seq), row_blk)],
        out_specs=[pl.BlockSpec((1, n_cblk, LANES), row_blk), pl.BlockSpec((1, n_cblk, LANES), row_blk)],
    )
    return pl.pallas_call(
        functools.partial(_compact_kernel, cap, n_chunks),
        grid_spec=grid_spec,
        out_shape=[jax.ShapeDtypeStruct((rows, n_cblk, LANES), jnp.int32),
                   jax.ShapeDtypeStruct((rows, n_cblk, LANES), F32)],
        compiler_params=pltpu.CompilerParams(dimension_semantics=("arbitrary",),
                                             vmem_limit_bytes=VMEM_LIMIT_BYTES),
        name="route_compact",
    )(offs.reshape(-1), key.reshape(rows, 1, seq), aff2d.reshape(rows, 1, seq))


def _gather_kernel(cap, idx_ref, hp_ref, xg_ref):
    b = pl.program_id(0)
    e = pl.program_id(1)
    base = (b * N_EXPERTS + e) * cap
    for c in range(cap):
        g, k = divmod(c, SUBLANES)
        xg_ref[0, 0, g, k:k + 1, :] = hp_ref[0, pl.ds(idx_ref[base + c], 1), :]


def _gather(idx_flat, hp, cap):
    batch, seq, half = hp.shape
    grid_spec = pltpu.PrefetchScalarGridSpec(
        num_scalar_prefetch=1,
        grid=(batch, N_EXPERTS),
        in_specs=[pl.BlockSpec((1, seq, half), lambda b, e, idx: (b, 0, 0))],
        out_specs=pl.BlockSpec((1, 1, cap // SUBLANES, SUBLANES, half), lambda b, e, idx: (b, e, 0, 0, 0)),
    )
    xg = pl.pallas_call(
        functools.partial(_gather_kernel, cap),
        grid_spec=grid_spec,
        out_shape=jax.ShapeDtypeStruct((batch, N_EXPERTS, cap // SUBLANES, SUBLANES, half), hp.dtype),
        compiler_params=pltpu.CompilerParams(dimension_semantics=("arbitrary", "arbitrary"),
                                             vmem_limit_bytes=VMEM_LIMIT_BYTES),
        name="gather",
    )(idx_flat, hp)
    return xg.reshape(batch, N_EXPERTS, cap, half)


def _experts_kernel(cap, tc, apply_norm, out_rows, idx_ref, xg_ref, gsel_ref, wg_ref, wu_ref, wd_ref, x2_ref, gf_ref,
                    o_hbm, acc_ref, y_ref, sem):
    b = pl.program_id(0)
    e = pl.program_id(1)
    base = (b * N_EXPERTS + e) * cap
    n_chunks = cap // tc

    @pl.when(e == 0)
    def _():
        acc_ref[...] = jnp.zeros_like(acc_ref)

    slab = x2_ref.shape[0]
    slab_rows = pl.ds(pl.multiple_of(e * slab, slab), slab)
    acc_ref[slab_rows, :] = acc_ref[slab_rows, :] + x2_ref[...]

    diag = (lax.broadcasted_iota(jnp.int32, (LANES, LANES), 0)
            == lax.broadcasted_iota(jnp.int32, (LANES, LANES), 1))

    def gate_column(ci):
        cols = []
        for blk in range(ci * tc // LANES, (ci + 1) * tc // LANES):
            g_row = gsel_ref[0, blk:blk + 1, :]
            cols.append(jnp.sum(jnp.where(diag, g_row, 0.0), axis=1, keepdims=True))
        return jnp.concatenate(cols, axis=0)

    def ffn(ci):
        words = xg_ref[0, 0, ci * tc:(ci + 1) * tc, :]
        x = jnp.concatenate([_unpack_bf16_pair(words, 0), _unpack_bf16_pair(words, 1)], axis=-1)
        gate = _dot(x, wg_ref[0])
        up = _dot(x, wu_ref[0])
        act = (gate * jax.nn.sigmoid(gate) * up * gate_column(ci)).astype(BF16)
        y_ref[ci % 2] = _dot(act, wd_ref[0])

    def scatter(ci, group):
        for r0 in range(0, tc, group):
            toks = [idx_ref[base + ci * tc + r0 + k] for k in range(group)]
            rows = [acc_ref[pl.ds(tok, 1), :] + y_ref[ci % 2, r0 + k:r0 + k + 1, :]
                    for k, tok in enumerate(toks)]
            for tok, row in zip(toks, rows):
                acc_ref[pl.ds(tok, 1), :] = row

    for ci in range(n_chunks):
        ffn(ci)
        if ci > 0:
            scatter(ci - 1, SCATTER_GROUP_OVERLAPPED)
    scatter(n_chunks - 1, SCATTER_GROUP_TAIL)

    @pl.when(e == N_EXPERTS - 1)
    def _():
        n_tiles = acc_ref.shape[0] // out_rows

        def out_copy(i):
            rows = pl.ds(pl.multiple_of(i * out_rows, out_rows), out_rows)
            return rows, pltpu.make_async_copy(acc_ref.at[rows], o_hbm.at[b, rows], sem)

        def emit(i, _):
            rows, copy = out_copy(i)
            if apply_norm:
                acc_ref[rows, :] = _rms(acc_ref[rows, :], gf_ref[...])
            copy.start()
            return 0

        def drain(i, _):
            out_copy(i)[1].wait()
            return 0

        lax.fori_loop(0, n_tiles, emit, 0)
        lax.fori_loop(0, n_tiles, drain, 0)


def _experts(idx_flat, gsel, xg, wg, wu, wd, x2, gf, apply_norm, seq, tc):
    batch, _, cap, half = xg.shape
    d = 2 * half
    ff = wg.shape[2]
    slab = seq // N_EXPERTS
    assert tc % LANES == 0 and gsel.shape == (batch * N_EXPERTS, cap // LANES, LANES)
    grid_spec = pltpu.PrefetchScalarGridSpec(
        num_scalar_prefetch=1,
        grid=(batch, N_EXPERTS),
        in_specs=[
            pl.BlockSpec((1, 1, cap, half), lambda b, e, idx: (b, e, 0, 0)),
            pl.BlockSpec((1, cap // LANES, LANES), lambda b, e, idx: (b * N_EXPERTS + e, 0, 0)),
            pl.BlockSpec((1, d, ff), lambda b, e, idx: (e, 0, 0)),
            pl.BlockSpec((1, d, ff), lambda b, e, idx: (e, 0, 0)),
            pl.BlockSpec((1, ff, d), lambda b, e, idx: (e, 0, 0)),
            pl.BlockSpec((slab, d), lambda b, e, idx: (b * N_EXPERTS + e, 0)),
            pl.BlockSpec(gf.shape, lambda b, e, idx: (0, 0)),
        ],
        out_specs=pl.BlockSpec(memory_space=pl.ANY),
        scratch_shapes=[pltpu.VMEM((seq, d), F32), pltpu.VMEM((2, tc, d), F32), pltpu.SemaphoreType.DMA],
    )
    return pl.pallas_call(
        functools.partial(_experts_kernel, cap, tc, apply_norm, min(seq, OUT_TILE_ROWS)),
        grid_spec=grid_spec,
        out_shape=jax.ShapeDtypeStruct((batch, seq, d), F32),
        compiler_params=pltpu.CompilerParams(dimension_semantics=("arbitrary", "arbitrary"),
                                             vmem_limit_bytes=EXPERTS_VMEM_LIMIT_BYTES),
        name="experts",
    )(idx_flat, xg, gsel, wg, wu, wd, x2, gf)


def _rot_cols(w):
    half = w.shape[-1] // 2
    return jnp.concatenate([-w[..., half:], w[..., :half]], axis=-1)


def _tile(n, pref):
    return pref if n % pref == 0 else n


def kernel(x, mem, norm_mix_g, w_in, conv_w, conv_b, w_conv_out, q_norm_g, w_uq, kv_norm_g, w_ukv,
           w_mla_out, b_gate, w_mix_out, norm_mem_g, norm_memkv_g, w_mem_q, w_mem_kv, w_mem_out,
           norm_moe_g, w_router, w_exp_gate, w_exp_up, w_exp_down, norm_final_g):
    batch, seq, d = x.shape
    depth = w_in.shape[0]
    t = batch * seq
    q_lora = q_norm_g.shape[1]
    kv_lora = kv_norm_g.shape[1]
    cap = max(1, CAPACITY_FACTOR * seq // N_EXPERTS)
    assert cap % LANES == 0 and seq % LANES == 0 and d % (2 * LANES) == 0

    tm_in = _tile(seq, INPROJ_TILE)
    tm_post = _tile(seq, POST_TILE)
    tq = _tile(seq, ATTN_Q_TILE)
    tk = _tile(seq, ATTN_K_TILE)
    tc = _tile(cap, EXPERT_CHUNK)

    inv = 1.0 / (ROPE_THETA ** (np.arange(0, QK_ROPE, 2, dtype=np.float64) / QK_ROPE))
    ang = np.arange(seq, dtype=np.float64)[:, None] * inv[None, :]
    cos2 = np.concatenate([np.cos(ang), np.cos(ang)], axis=-1)
    sin2 = np.concatenate([np.sin(ang), np.sin(ang)], axis=-1)
    qf = (QK_DIM ** -0.5) * math.log2(math.e)
    kc, ks = jnp.asarray(cos2, F32), jnp.asarray(sin2, F32)
    qc, qs = jnp.asarray(qf * cos2.T, F32), jnp.asarray(qf * sin2.T, F32)

    o_cq = 3 * d
    o_ckv = o_cq + q_lora
    o_kr = o_ckv + kv_lora
    o_gl = o_kr + QK_ROPE
    cols = {"xc": 0, "gb": d, "gc": 2 * d, "cq": o_cq, "ckv": o_ckv}

    x2d = x.reshape(t, d)
    mem2d = mem.reshape(-1, d)
    row = lambda a: a.reshape(1, -1)

    for l in range(depth):
        win = w_in[l].astype(BF16)
        wgl = win[:, o_gl:]
        wkr = jnp.concatenate([win[:, o_kr:o_gl], _rot_cols(win[:, o_kr:o_gl])], axis=1)
        wq3 = w_uq[l].reshape(q_lora, MLA_HEADS, QK_DIM)
        wq = wq3.transpose(1, 2, 0).astype(BF16)
        wqr = _rot_cols(wq3[..., QK_NOPE:]).transpose(1, 2, 0).astype(BF16)
        wkv3 = w_ukv[l].reshape(kv_lora, MLA_HEADS, QK_NOPE + V_HEAD)
        wk = wkv3[..., :QK_NOPE].transpose(1, 0, 2).astype(BF16)
        wv = wkv3[..., QK_NOPE:].transpose(1, 2, 0).astype(BF16)

        wr_hi = w_router[l].T.astype(BF16)
        wr_lo = (w_router[l].T - wr_hi.astype(F32)).astype(BF16)

        v, gb, gates, qt, k, vt = _inproj(
            x2d, batch, seq, row(norm_mix_g[l]), win, wgl, wkr, cols, row(q_norm_g[l]), row(kv_norm_g[l]),
            row(b_gate[l]), wq, wqr, wk, wv, qc, qs, kc, ks, qf, tm_in)
        w_exp = [w_exp_gate[l], w_exp_up[l], w_exp_down[l]]
        w_flat = [w.reshape(-1, w.shape[-1]) for w in w_exp]
        n_attn_steps = batch * MLA_HEADS * (seq // tq)
        fused_cast = all(_side_cast_fits(w, n_attn_steps) for w in w_flat)
        att, w_bf = _attention(qt, k, vt, tq, tk, w_flat if fused_cast else [])
        att = att.reshape(t, MLA_HEADS * V_HEAD)
        wgb, wub, wdb = ([wb.reshape(w.shape) for wb, w in zip(w_bf, w_exp)] if fused_cast
                         else [w.astype(BF16) for w in w_exp])
        mk, mv = _memkv(mem2d, row(norm_memkv_g[l]), w_mem_kv[l].astype(BF16))
        x2, hp, aff = _post(
            x2d, batch, seq, v, gb, gates, att, conv_w[l], row(conv_b[l]), w_conv_out[l].astype(BF16),
            w_mla_out[l].astype(BF16), w_mix_out[l].astype(BF16), row(norm_mem_g[l]),
            w_mem_q[l].astype(BF16), mk, mv, w_mem_out[l].astype(BF16), row(norm_moe_g[l]),
            wr_hi, wr_lo, tm_post)
        idx, gsel = _route(aff.reshape(batch * N_EXPERTS, seq), cap)
        idx_flat = idx.reshape(-1)
        xg = _gather(idx_flat, hp.reshape(batch, seq, d // 2), cap)
        x2d = _experts(idx_flat, gsel, xg, wgb, wub, wdb, x2, row(norm_final_g),
                       l == depth - 1, seq, tc).reshape(t, d)
    return x2d.reshape(batch, seq, d)
```
